```python
import math
import jax, jax.numpy as jnp
from jax import lax
import numpy as np

D_MODEL = 1024
BATCH = 32
SEQ = 256
DEPTH = 2
DEC_BATCH = 4
DEC_SEQ = 2048
PAST_LEN = 512

GRID_W = 64
N_MIXERS = 2
N_S5_LAYERS = (DEPTH + 1) // 2
N_LRU_LAYERS = DEPTH // 2
S5_GROUP = 16
S5_GROUPS = D_MODEL // S5_GROUP
S5_STATE = 64
LRU_WIDTH = D_MODEL
LRU_BLOCKS = 16
LRU_BLOCK = LRU_WIDTH // LRU_BLOCKS
LRU_C = 8.0
CONV_W = 4
CONV_LEFT = 2
D_FF = 2816
ALPHA = (2.0 * DEPTH) ** 0.25
BETA = (8.0 * DEPTH) ** -0.25
LN_EPS = 1e-5
F32 = jnp.float32

kernel_name = "hybrid_s5_rglru_macaron_diffusion_step"


def layer_norm(x, g, b):
    xf = x.astype(F32)
    mu = jnp.mean(xf, -1, keepdims=True)
    var = jnp.mean(jnp.square(xf - mu), -1, keepdims=True)
    y = (xf - mu) * lax.rsqrt(var + LN_EPS) * g.astype(F32) + b.astype(F32)
    return y.astype(x.dtype)


def adaln_params(cond, w, b):
    m = jax.nn.silu(cond) @ w + b
    return m.reshape(cond.shape[0], 1, 3, 3, D_MODEL)


def swiglu(h, w_up, w_down):
    a, g = jnp.split(h @ w_up, 2, axis=-1)
    return (jax.nn.silu(g) * a) @ w_down


def cmul(ar, ai, br, bi):
    return ar * br - ai * bi, ar * bi + ai * br


def complex_linear_combine(e1, e2):
    a1r, a1i, b1r, b1i = e1
    a2r, a2i, b2r, b2i = e2
    ar, ai = cmul(a1r, a1i, a2r, a2i)
    br, bi = cmul(a2r, a2i, b1r, b1i)
    return ar, ai, br + b2r, bi + b2i


def real_linear_combine(e1, e2):
    return e1[0] * e2[0], e2[0] * e1[1] + e2[1]


def grid_pos_embed(rows):
    t = jnp.arange(rows * GRID_W)
    row = (t // GRID_W).astype(F32)
    col = (t % GRID_W).astype(F32)
    quarter = D_MODEL // 4
    omega = 1.0 / (10000.0 ** (jnp.arange(quarter, dtype=F32) / quarter))

    def emb(p):
        ang = p[:, None] * omega[None, :]
        return jnp.concatenate([jnp.sin(ang), jnp.cos(ang)], -1)

    return jnp.concatenate([emb(row), emb(col)], -1)


def s5_discretize(a_re, a_im, log_dt, b_re, b_im):
    dt = jnp.exp(log_dt)[:, None]
    mag = jnp.exp(dt * a_re)
    abar_re = mag * jnp.cos(dt * a_im)
    abar_im = mag * jnp.sin(dt * a_im)
    den = a_re * a_re + a_im * a_im
    nr, ni = abar_re - 1.0, abar_im
    q_re = (nr * a_re + ni * a_im) / den
    q_im = (ni * a_re - nr * a_im) / den
    bb_re, bb_im = cmul(q_re[..., None], q_im[..., None], b_re, b_im)
    return abar_re, abar_im, bb_re, bb_im


def s5_direction(u, abar_re, abar_im, bb_re, bb_im, s0, reverse):
    xr = jnp.einsum('blgi,gpi->blgp', u, bb_re)
    xi = jnp.einsum('blgi,gpi->blgp', u, bb_im)
    if s0 is not None:
        ir, ii = cmul(abar_re, abar_im, s0[0], s0[1])
        idx = -1 if reverse else 0
        xr = xr.at[:, idx].add(ir)
        xi = xi.at[:, idx].add(ii)
    ar = jnp.broadcast_to(abar_re, xr.shape)
    ai = jnp.broadcast_to(abar_im, xr.shape)
    _, _, sr, si = lax.associative_scan(complex_linear_combine, (ar, ai, xr, xi), reverse=reverse, axis=1)
    return sr, si


def s5_mixer(h, p, j, init):
    (w_in, a_re, a_im, log_dt, b_re, b_im, c_re, c_im, d_skip, w_glu) = p
    bt, l, _ = h.shape
    u = (h @ w_in[j]).astype(F32)
    ug = u.reshape(bt, l, S5_GROUPS, S5_GROUP)
    srs, sis, finals = [], [], []
    for d, rev in ((0, False), (1, True)):
        abr, abi, bbr, bbi = s5_discretize(a_re[j, d].astype(F32), a_im[j, d].astype(F32),
                                           log_dt[j, d].astype(F32), b_re[j].astype(F32), b_im[j].astype(F32))
        s0 = None if init is None else (init[:, d, 0].astype(F32), init[:, d, 1].astype(F32))
        sr, si = s5_direction(ug, abr, abi, bbr, bbi, s0, rev)
        srs.append(sr)
        sis.append(si)
        if init is None:
            k = 0 if rev else -1
            finals.append(jnp.stack([sr[:, k], si[:, k]], 1))
    s_re = srs[0] + srs[1]
    s_im = sis[0] + sis[1]
    y = (jnp.einsum('blgp,gip->blgi', s_re, c_re[j].astype(F32))
         - jnp.einsum('blgp,gip->blgi', s_im, c_im[j].astype(F32)))
    y = y.reshape(bt, l, D_MODEL) + d_skip[j].astype(F32) * u
    y = jax.nn.gelu(y).astype(h.dtype)
    a, g = jnp.split(y @ w_glu[j], 2, axis=-1)
    out = a * jax.nn.sigmoid(g)
    state = jnp.stack(finals, 1).astype(h.dtype) if init is None else None
    return out, state


def centred_dwconv(x, w, b):
    l = x.shape[1]
    xp = jnp.pad(x, ((0, 0), (CONV_LEFT, CONV_W - 1 - CONV_LEFT), (0, 0)))
    out = b + xp[:, 0:l] * w[0]
    for k in range(1, CONV_W):
        out = out + xp[:, k:k + l] * w[k]
    return out


def rglru_direction(xc, wa, ba, wx, bx, lam, h0, reverse):
    bt, l, _ = xc.shape
    xb = xc.reshape(bt, l, LRU_BLOCKS, LRU_BLOCK)
    r = jax.nn.sigmoid(jnp.einsum('blhi,hij->blhj', xb, wa).reshape(bt, l, LRU_WIDTH) + ba)
    gi = jax.nn.sigmoid(jnp.einsum('blhi,hij->blhj', xb, wx).reshape(bt, l, LRU_WIDTH) + bx)
    log_a = -LRU_C * r * jax.nn.softplus(-lam)
    a = jnp.exp(log_a)
    b = jnp.sqrt(-jnp.expm1(2.0 * log_a)) * (gi * xc)
    if h0 is not None:
        idx = -1 if reverse else 0
        b = b.at[:, idx].add(a[:, idx] * h0)
    _, hs = lax.associative_scan(real_linear_combine, (a, b), reverse=reverse, axis=1)
    return hs


def rglru_mixer(h, p, j, init):
    (w_in, conv_w, conv_b, w_a, b_a, w_x, b_x, lam, w_out) = p
    xr, gate = jnp.split(h @ w_in[j], 2, axis=-1)
    xc = centred_dwconv(xr.astype(F32), conv_w[j].astype(F32), conv_b[j].astype(F32))
    hs, finals = [], []
    for d, rev in ((0, False), (1, True)):
        h0 = None if init is None else init[:, d].astype(F32)
        hd = rglru_direction(xc, w_a[j, d].astype(F32), b_a[j, d].astype(F32), w_x[j, d].astype(F32),
                             b_x[j, d].astype(F32), lam[j, d].astype(F32), h0, rev)
        hs.append(hd)
        if init is None:
            finals.append(hd[:, 0] if rev else hd[:, -1])
    y = (hs[0] + hs[1]).astype(h.dtype) * jax.nn.gelu(gate)
    out = y @ w_out[j]
    state = jnp.stack(finals, 1).astype(h.dtype) if init is None else None
    return out, state


def trunk(x, cond, init_s5, init_lru, shared, s5p, lrup):
    ada_w, ada_b, ln_g, ln_b, ffn_w_up, ffn_w_down = shared
    states_s5, states_lru = [], []
    for i in range(DEPTH):
        mod = adaln_params(cond, ada_w[i], ada_b[i])
        sh, sc, gt = mod[:, :, 0, 0], mod[:, :, 0, 1], mod[:, :, 0, 2]
        f = swiglu(x * (1.0 + sc) + sh, ffn_w_up[i, 0], ffn_w_down[i, 0])
        x = layer_norm(ALPHA * x + 0.5 * gt * f, ln_g[i, 0], ln_b[i, 0])
        sh, sc, gt = mod[:, :, 1, 0], mod[:, :, 1, 1], mod[:, :, 1, 2]
        hm = x * (1.0 + sc) + sh
        j = i // N_MIXERS
        if i % N_MIXERS == 0:
            m, st = s5_mixer(hm, s5p, j, None if init_s5 is None else init_s5[:, j])
            if st is not None:
                states_s5.append(st)
        else:
            m, st = rglru_mixer(hm, lrup, j, None if init_lru is None else init_lru[:, j])
            if st is not None:
                states_lru.append(st)
        x = layer_norm(ALPHA * x + gt * m, ln_g[i, 1], ln_b[i, 1])
        sh, sc, gt = mod[:, :, 2, 0], mod[:, :, 2, 1], mod[:, :, 2, 2]
        f = swiglu(x * (1.0 + sc) + sh, ffn_w_up[i, 1], ffn_w_down[i, 1])
        x = layer_norm(ALPHA * x + 0.5 * gt * f, ln_g[i, 2], ln_b[i, 2])
    return x, states_s5, states_lru


def setup_inputs(seed: int = 0) -> dict:
    key = jax.random.key(seed)
    ks = jax.random.split(key, 40)
    nrm = lambda k, s, sc: jax.random.normal(k, s, F32) * sc
    D = D_MODEL
    G, P, I = S5_GROUPS, S5_STATE, S5_GROUP
    inp = {}
    inp["x_prompt"] = nrm(ks[0], (BATCH, SEQ, D), 1.0)
    inp["x_sample"] = nrm(ks[1], (DEC_BATCH, DEC_SEQ, D), 1.0)
    inp["c"] = nrm(ks[2], (DEC_BATCH, D), 1.0)
    inp["state_s5"] = nrm(ks[3], (DEC_BATCH, N_S5_LAYERS, 2, 2, G, P), 0.3)
    inp["state_lru"] = nrm(ks[4], (DEC_BATCH, N_LRU_LAYERS, 2, LRU_WIDTH), 0.5)
    inp["c_ctx"] = nrm(ks[5], (D,), 1.0)
    inp["ada_w"] = nrm(ks[6], (DEPTH, D, 9 * D), 0.5 * D ** -0.5)
    inp["ada_b"] = nrm(ks[7], (DEPTH, 9 * D), 0.02)
    inp["ln_g"] = 1.0 + nrm(ks[8], (DEPTH, 3, D), 0.02)
    inp["ln_b"] = nrm(ks[9], (DEPTH, 3, D), 0.02)
    inp["ffn_w_up"] = nrm(ks[10], (DEPTH, 2, D, 2 * D_FF), D ** -0.5)
    inp["ffn_w_down"] = nrm(ks[11], (DEPTH, 2, D_FF, D), BETA * D_FF ** -0.5)
    inp["s5_w_in"] = nrm(ks[12], (N_S5_LAYERS, D, D), D ** -0.5)
    inp["s5_a_re"] = -0.5 + nrm(ks[13], (N_S5_LAYERS, 2, G, P), 0.01)
    inp["s5_a_im"] = math.pi * jnp.arange(P, dtype=F32) + nrm(ks[14], (N_S5_LAYERS, 2, G, P), 0.01)
    inp["s5_log_dt"] = jax.random.uniform(ks[15], (N_S5_LAYERS, 2, G), F32, math.log(1e-3), math.log(1e-1))
    inp["s5_b_re"] = nrm(ks[16], (N_S5_LAYERS, G, P, I), (2.0 * I) ** -0.5)
    inp["s5_b_im"] = nrm(ks[17], (N_S5_LAYERS, G, P, I), (2.0 * I) ** -0.5)
    inp["s5_c_re"] = nrm(ks[18], (N_S5_LAYERS, G, I, P), (2.0 * P) ** -0.5)
    inp["s5_c_im"] = nrm(ks[19], (N_S5_LAYERS, G, I, P), (2.0 * P) ** -0.5)
    inp["s5_d"] = nrm(ks[20], (N_S5_LAYERS, D), 1.0)
    inp["s5_w_glu"] = nrm(ks[21], (N_S5_LAYERS, D, 2 * D), BETA * D ** -0.5)
    inp["lru_w_in"] = nrm(ks[22], (N_LRU_LAYERS, D, 2 * LRU_WIDTH), D ** -0.5)
    inp["lru_conv_w"] = nrm(ks[23], (N_LRU_LAYERS, CONV_W, LRU_WIDTH), CONV_W ** -0.5)
    inp["lru_conv_b"] = nrm(ks[24], (N_LRU_LAYERS, LRU_WIDTH), 0.02)
    inp["lru_w_a"] = nrm(ks[25], (N_LRU_LAYERS, 2, LRU_BLOCKS, LRU_BLOCK, LRU_BLOCK), LRU_BLOCK ** -0.5)
    inp["lru_b_a"] = nrm(ks[26], (N_LRU_LAYERS, 2, LRU_WIDTH), 0.02)
    inp["lru_w_x"] = nrm(ks[27], (N_LRU_LAYERS, 2, LRU_BLOCKS, LRU_BLOCK, LRU_BLOCK), LRU_BLOCK ** -0.5)
    inp["lru_b_x"] = nrm(ks[28], (N_LRU_LAYERS, 2, LRU_WIDTH), 0.02)
    u = jax.random.uniform(ks[29], (N_LRU_LAYERS, 2, LRU_WIDTH), F32, 0.9, 0.999)
    s = u ** (1.0 / LRU_C)
    inp["lru_lambda"] = jnp.log(s) - jnp.log1p(-s)
    inp["lru_w_out"] = nrm(ks[30], (N_LRU_LAYERS, LRU_WIDTH, D), BETA * LRU_WIDTH ** -0.5)
    return inp


def reference(x_prompt, x_sample, c, state_s5, state_lru, c_ctx, ada_w, ada_b, ln_g, ln_b,
              ffn_w_up, ffn_w_down, s5_w_in, s5_a_re, s5_a_im, s5_log_dt, s5_b_re, s5_b_im,
              s5_c_re, s5_c_im, s5_d, s5_w_glu, lru_w_in, lru_conv_w, lru_conv_b, lru_w_a, lru_b_a,
              lru_w_x, lru_b_x, lru_lambda, lru_w_out):
    shared = (ada_w, ada_b, ln_g, ln_b, ffn_w_up, ffn_w_down)
    s5p = (s5_w_in, s5_a_re, s5_a_im, s5_log_dt, s5_b_re, s5_b_im, s5_c_re, s5_c_im, s5_d, s5_w_glu)
    lrup = (lru_w_in, lru_conv_w, lru_conv_b, lru_w_a, lru_b_a, lru_w_x, lru_b_x, lru_lambda, lru_w_out)

    y_prompt, st_s5, st_lru = trunk(x_prompt, c_ctx[None, :], None, None, shared, s5p, lrup)
    new_state_s5 = jnp.stack(st_s5, 1)
    new_state_lru = jnp.stack(st_lru, 1)

    rows = x_sample.shape[1] // GRID_W
    xs = x_sample + grid_pos_embed(rows).astype(x_sample.dtype)[None]
    y_sample, _, _ = trunk(xs, c, state_s5, state_lru, shared, s5p, lrup)
    return (y_prompt, y_sample, new_state_s5, new_state_lru)
```

```python
import functools

import jax
import jax.numpy as jnp
from jax import lax
from jax.experimental import pallas as pl
from jax.experimental.pallas import tpu as pltpu

F32 = jnp.float32
BF16 = jnp.bfloat16

GRID_W = 64
S5_GROUP = 16
S5_STATE = 64
LRU_BLOCK = 64
LRU_C = 8.0
CONV_W = 4
CONV_LEFT = 2
LN_EPS = 1e-5

LANES = 128
SUBLANES = 8
VMEM_LIMIT_BYTES = 56 * 1024 * 1024

S5_CHUNK = 16
S5_CW = S5_CHUNK * S5_GROUP
S5_SW = 4 * S5_STATE
TOKEN_TILE = 512
LRU_ROW_CHUNK = 256
HIGHEST = lax.Precision.HIGHEST


def _params(*sem):
    return pltpu.CompilerParams(dimension_semantics=sem, vmem_limit_bytes=VMEM_LIMIT_BYTES)


def _const_spec(shape):
    nd = len(shape)
    return pl.BlockSpec(shape, lambda *_: (0,) * nd, pipeline_mode=pl.Buffered(1))


def _layer_norm(z, g, b):
    mu = jnp.mean(z, axis=-1, keepdims=True)
    zc = z - mu
    var = jnp.mean(zc * zc, axis=-1, keepdims=True)
    return zc * lax.rsqrt(var + LN_EPS) * g + b


def _modulate(x, mod_ref, sub):
    sh = mod_ref[0, 3 * sub:3 * sub + 1, :]
    sc = mod_ref[0, 3 * sub + 1:3 * sub + 2, :]
    return x * (1.0 + sc) + sh


def _gate(mod_ref, sub):
    return mod_ref[0, 3 * sub + 2:3 * sub + 3, :]


def _mod_spec(n_cond, d, tm, seq_len):
    if n_cond == 1:
        return pl.BlockSpec((1, 9, d), lambda i: (0, 0, 0))
    return pl.BlockSpec((1, 9, d), lambda i: ((i * tm) // seq_len, 0, 0))


def _mod_kernel(c_ref, w_ref, b_ref, o_ref):
    s = jax.nn.silu(c_ref[...]).astype(BF16)
    o_ref[0] = jnp.dot(s, w_ref[0].astype(BF16), preferred_element_type=F32) + b_ref[0]


def _modulation(cond8, ada_w, ada_b):
    depth, d, n9 = ada_w.shape
    tn = n9 // 4
    return pl.pallas_call(
        _mod_kernel,
        grid=(depth, n9 // tn),
        in_specs=[pl.BlockSpec((SUBLANES, d), lambda l, n: (0, 0)),
                  pl.BlockSpec((1, d, tn), lambda l, n: (l, 0, n)),
                  pl.BlockSpec((1, 1, tn), lambda l, n: (l, 0, n))],
        out_specs=pl.BlockSpec((1, SUBLANES, tn), lambda l, n: (l, 0, n)),
        out_shape=jax.ShapeDtypeStruct((depth, SUBLANES, n9), F32),
        compiler_params=_params("arbitrary", "arbitrary"),
        name="adaln_modulation",
    )(cond8, ada_w, ada_b.reshape(depth, 1, n9))


def _ffn_kernel(*refs, sub, alpha, d_ff, ck, n_prompt_rows, has_pos):
    if has_pos:
        x_ref, pos_ref, mod_ref, wup_ref, wdn_ref, g_ref, b_ref, o_ref, act_ref = refs
        x = x_ref[...] + pos_ref[...]
    else:
        x_ref, mod_ref, wup_ref, wdn_ref, g_ref, b_ref, o_ref, act_ref = refs
        x = x_ref[...]
    del n_prompt_rows
    h = _modulate(x, mod_ref, sub).astype(BF16)
    for c0 in range(0, d_ff, ck):
        a = jnp.dot(h, wup_ref[:, c0:c0 + ck], preferred_element_type=F32)
        g = jnp.dot(h, wup_ref[:, d_ff + c0:d_ff + c0 + ck], preferred_element_type=F32)
        act_ref[:, c0:c0 + ck] = (jax.nn.silu(g) * a).astype(BF16)
    f = jnp.dot(act_ref[...], wdn_ref[...], preferred_element_type=F32)
    z = alpha * x + (0.5 * _gate(mod_ref, sub)) * f
    o_ref[...] = _layer_norm(z, g_ref[...], b_ref[...])


def _ffn(x, mod, w_up, w_dn, ln_g, ln_b, *, sub, alpha, seq_len, pos=None):
    n, d = x.shape
    d_ff = w_dn.shape[0]
    tm = min(TOKEN_TILE, n)
    ck = 256 if d_ff % 256 == 0 else d_ff
    row = pl.BlockSpec((tm, d), lambda i: (i, 0))
    in_specs = [row]
    args = [x]
    if pos is not None:
        per_seq = seq_len // tm
        in_specs.append(pl.BlockSpec((tm, d), lambda i: (i % per_seq, 0)))
        args.append(pos)
    in_specs += [_mod_spec(mod.shape[0], d, tm, seq_len), _const_spec(w_up.shape), _const_spec(w_dn.shape),
                 _const_spec((1, d)), _const_spec((1, d))]
    args += [mod, w_up, w_dn, ln_g.reshape(1, d), ln_b.reshape(1, d)]
    return pl.pallas_call(
        functools.partial(_ffn_kernel, sub=sub, alpha=alpha, d_ff=d_ff, ck=ck, n_prompt_rows=0,
                          has_pos=pos is not None),
        grid=(n // tm,),
        in_specs=in_specs,
        out_specs=row,
        out_shape=jax.ShapeDtypeStruct((n, d), F32),
        scratch_shapes=[pltpu.VMEM((tm, d_ff), BF16)],
        compiler_params=_params("arbitrary"),
        name="ffn_sublayer",
    )(*args)


def _inproj_kernel(x_ref, mod_ref, w_ref, o_ref, *, sub):
    h = _modulate(x_ref[...], mod_ref, sub).astype(BF16)
    o_ref[...] = jnp.dot(h, w_ref[...], preferred_element_type=F32)


def _inproj(x, mod, w, *, sub, seq_len):
    n, d = x.shape
    dn = w.shape[1]
    tm = min(TOKEN_TILE, n)
    return pl.pallas_call(
        functools.partial(_inproj_kernel, sub=sub),
        grid=(n // tm,),
        in_specs=[pl.BlockSpec((tm, d), lambda i: (i, 0)), _mod_spec(mod.shape[0], d, tm, seq_len),
                  _const_spec(w.shape)],
        out_specs=pl.BlockSpec((tm, dn), lambda i: (i, 0)),
        out_shape=jax.ShapeDtypeStruct((n, dn), F32),
        compiler_params=_params("arbitrary"),
        name="mixer_in_proj",
    )(x, mod, w)


def _s5_discretize(are, aim, ldt):
    dt = jnp.exp(ldt)
    mag = jnp.exp(dt * are)
    abr = mag * jnp.cos(dt * aim)
    abi = mag * jnp.sin(dt * aim)
    den = are * are + aim * aim
    nr, ni = abr - 1.0, abi
    qr = (nr * are + ni * aim) / den
    qi = (ni * are - nr * aim) / den
    return abr, abi, qr, qi


def _s5_ops_kernel(arow_ref, acol_ref, btr_ref, bti_ref, ctr_ref, cti_ref,
                   m_ref, mst_ref, min_ref, a16_ref):
    t_chunk, grp, w = S5_CHUNK, S5_GROUP, S5_CW
    arow, acol = arow_ref[0], acol_ref[0]
    btr, bti = btr_ref[0], bti_ref[0]
    ctr, cti = ctr_ref[0], cti_ref[0]
    colblk = lax.broadcasted_iota(jnp.int32, (w, w), 1) // grp
    colblk_p = lax.broadcasted_iota(jnp.int32, (S5_STATE, w), 1) // grp

    lag_tabs, decay, carry_in = [], [], []
    for d in range(2):
        abr, abi, qr, qi = _s5_discretize(arow[d:d + 1], arow[2 + d:3 + d], arow[4 + d:5 + d])
        lr = [qr * btr - qi * bti]
        li = [qr * bti + qi * btr]
        er, ei = abr, abi
        for _ in range(t_chunk - 1):
            lr.append(lr[-1] * abr - li[-1] * abi)
            li.append(lr[-2] * abi + li[-1] * abr)
            er, ei = er * abr - ei * abi, er * abi + ei * abr
        decay.append((er, ei))
        if d == 0:
            lr, li = lr[::-1], li[::-1]
        lag_tabs.append((jnp.concatenate(lr, axis=0), jnp.concatenate(li, axis=0)))

        cbr, cbi, _, _ = _s5_discretize(acol[:, d:d + 1], acol[:, 2 + d:3 + d], acol[:, 4 + d:5 + d])
        pr, pi = cbr, cbi
        er_t = jnp.zeros((S5_STATE, w), F32)
        ei_t = jnp.zeros((S5_STATE, w), F32)
        for n in range(1, t_chunk + 1):
            blk = n - 1 if d == 0 else t_chunk - n
            er_t = jnp.where(colblk_p == blk, pr, er_t)
            ei_t = jnp.where(colblk_p == blk, pi, ei_t)
            pr, pi = pr * cbr - pi * cbi, pr * cbi + pi * cbr
        carry_in.append((ctr * er_t - cti * ei_t, ctr * ei_t + cti * er_t))

    acc = jnp.zeros((w, w), F32)
    for d in range(2):
        l_re, l_im = lag_tabs[d]
        tab = (jnp.dot(l_re, ctr, preferred_element_type=F32, precision=HIGHEST)
               - jnp.dot(l_im, cti, preferred_element_type=F32, precision=HIGHEST))
        for tp in range(t_chunk):
            if d == 0:
                cut = (t_chunk - 1 - tp) * grp
                shifted = tab[cut:] if cut == 0 else jnp.concatenate([tab[cut:], jnp.zeros((cut, w), F32)], 0)
            else:
                cut = tp * grp
                shifted = tab if cut == 0 else jnp.concatenate([jnp.zeros((cut, w), F32), tab[:w - cut]], 0)
            acc = acc + jnp.where(colblk == tp, shifted, 0.0)
    m_ref[0] = acc.astype(BF16)

    (fr, fi), (br, bi) = lag_tabs
    mst_ref[0] = jnp.concatenate([fr, br, fi, bi], axis=1).astype(BF16)
    (wfr, wfi), (wbr, wbi) = carry_in
    min_ref[0] = jnp.concatenate([wfr, wbr, -wfi, -wbi], axis=0).astype(BF16)
    (efr, efi), (ebr, ebi) = decay
    a16_ref[0] = jnp.concatenate([efr, ebr, efi, ebi], axis=1)


def _s5_operators(a_re, a_im, log_dt, b_re, b_im, c_re, c_im):
    _, g, p = a_re.shape
    ldt = jnp.broadcast_to(log_dt[:, :, None], (2, g, p))
    rows = jnp.concatenate([a_re, a_im, ldt, jnp.zeros((2, g, p), F32)], axis=0)
    arow = rows.transpose(1, 0, 2)
    acol = rows.transpose(1, 2, 0)
    bt_re, bt_im = b_re.transpose(0, 2, 1), b_im.transpose(0, 2, 1)
    ct_re = jnp.tile(c_re.transpose(0, 2, 1), (1, 1, S5_CHUNK))
    ct_im = jnp.tile(c_im.transpose(0, 2, 1), (1, 1, S5_CHUNK))
    per_g = lambda *s: pl.BlockSpec((1,) + s, lambda i: (i, 0, 0))
    w = S5_CW
    return pl.pallas_call(
        _s5_ops_kernel,
        grid=(g,),
        in_specs=[per_g(SUBLANES, p), per_g(p, SUBLANES), per_g(S5_GROUP, p), per_g(S5_GROUP, p),
                  per_g(p, w), per_g(p, w)],
        out_specs=[per_g(w, w), per_g(w, S5_SW), per_g(S5_SW, w), per_g(1, S5_SW)],
        out_shape=[jax.ShapeDtypeStruct((g, w, w), BF16), jax.ShapeDtypeStruct((g, w, S5_SW), BF16),
                   jax.ShapeDtypeStruct((g, S5_SW, w), BF16), jax.ShapeDtypeStruct((g, 1, S5_SW), F32)],
        compiler_params=_params("arbitrary"),
        name="s5_operators",
    )(arow, acol, bt_re, bt_im, ct_re, ct_im)


def _s5_mix_kernel(u_ref, m_ref, mst_ref, min_ref, a16_ref, s0_ref, y_ref, fin_ref,
                   loc_ref, sa_ref, sb_ref, *, n_seq, ncs):
    half = S5_SW // 2
    ub = u_ref[0].astype(BF16)
    loc = jnp.dot(ub, mst_ref[0], preferred_element_type=F32)
    loc_ref[0] = loc[:, :half]
    loc_ref[1] = loc[:, half:]
    a = a16_ref[0]
    ar, ai = a[:, :half], a[:, half:]
    is_fwd = lax.broadcasted_iota(jnp.int32, (n_seq, half), 1) < S5_STATE
    s0 = s0_ref[0]

    def step(k, carry):
        sr, si = carry
        rows_f = pl.ds(k, n_seq, stride=ncs)
        rows_b = pl.ds(ncs - 1 - k, n_seq, stride=ncs)
        sa_ref[0, rows_f, :] = sr
        sa_ref[1, rows_f, :] = si
        sb_ref[0, rows_b, :] = sr
        sb_ref[1, rows_b, :] = si
        lr = jnp.where(is_fwd, loc_ref[0, rows_f, :], loc_ref[0, rows_b, :])
        li = jnp.where(is_fwd, loc_ref[1, rows_f, :], loc_ref[1, rows_b, :])
        return ar * sr - ai * si + lr, ar * si + ai * sr + li

    sr, si = lax.fori_loop(0, ncs, step, (s0[:, :half], s0[:, half:]))
    fin_ref[0] = jnp.concatenate([sr, si], axis=1)
    sel = lax.broadcasted_iota(jnp.int32, sa_ref.shape[1:], 1) < S5_STATE
    s_in = jnp.concatenate([jnp.where(sel, sa_ref[0], sb_ref[0]), jnp.where(sel, sa_ref[1], sb_ref[1])], axis=1)
    y_ref[0] = (jnp.dot(ub, m_ref[0], preferred_element_type=F32)
                + jnp.dot(s_in.astype(BF16), min_ref[0], preferred_element_type=F32))


def _s5_mix(u_chunks, ops, s0, *, n_seq, ncs):
    g, nc, w = u_chunks.shape
    m, mst, m_in, a16 = ops
    per_g = lambda *s: pl.BlockSpec((1,) + s, lambda i: (i, 0, 0))
    return pl.pallas_call(
        functools.partial(_s5_mix_kernel, n_seq=n_seq, ncs=ncs),
        grid=(g,),
        in_specs=[per_g(nc, w), per_g(w, w), per_g(w, S5_SW), per_g(S5_SW, w), per_g(1, S5_SW),
                  per_g(n_seq, S5_SW)],
        out_specs=[per_g(nc, w), per_g(n_seq, S5_SW)],
        out_shape=[jax.ShapeDtypeStruct((g, nc, w), F32), jax.ShapeDtypeStruct((g, n_seq, S5_SW), F32)],
        scratch_shapes=[pltpu.VMEM((2, nc, S5_SW // 2), F32)] * 3,
        compiler_params=_params("arbitrary"),
        name="s5_chunk_scan",
    )(u_chunks, m, mst, m_in, a16, s0)


def _s5_out_kernel(y_ref, u_ref, x_ref, mod_ref, d_ref, w_ref, g_ref, b_ref, o_ref, *, alpha):
    d = x_ref.shape[1]
    y = y_ref[...] + d_ref[...] * u_ref[...]
    z = jnp.dot(jax.nn.gelu(y).astype(BF16), w_ref[...], preferred_element_type=F32)
    out = z[:, :d] * jax.nn.sigmoid(z[:, d:])
    o_ref[...] = _layer_norm(alpha * x_ref[...] + _gate(mod_ref, 1) * out, g_ref[...], b_ref[...])


def _s5_out(y, u, x, mod, d_skip, w_glu, ln_g, ln_b, *, alpha, seq_len):
    n, d = x.shape
    tm = min(TOKEN_TILE, n)
    row = pl.BlockSpec((tm, d), lambda i: (i, 0))
    return pl.pallas_call(
        functools.partial(_s5_out_kernel, alpha=alpha),
        grid=(n // tm,),
        in_specs=[row, row, row, _mod_spec(mod.shape[0], d, tm, seq_len), _const_spec((1, d)),
                  _const_spec(w_glu.shape), _const_spec((1, d)), _const_spec((1, d))],
        out_specs=row,
        out_shape=jax.ShapeDtypeStruct((n, d), F32),
        compiler_params=_params("arbitrary"),
        name="s5_out_proj",
    )(y, u, x, mod, d_skip.reshape(1, d), w_glu, ln_g.reshape(1, d), ln_b.reshape(1, d))


def _s5_mixer(x, mod, w_in, ops, d_skip, w_glu, ln_g, ln_b, init, *, alpha, n_seq, seq_len):
    n, d = x.shape
    g = d // S5_GROUP
    ncs = seq_len // S5_CHUNK
    nc = n_seq * ncs
    u = _inproj(x, mod, w_in, sub=1, seq_len=seq_len)
    u_chunks = u.reshape(nc, S5_CHUNK, g, S5_GROUP).transpose(2, 0, 1, 3).reshape(g, nc, S5_CW)
    if init is None:
        s0 = jnp.zeros((g, n_seq, S5_SW), F32)
    else:
        s0 = init.transpose(3, 0, 2, 1, 4).reshape(g, n_seq, S5_SW)
    y_chunks, fin = _s5_mix(u_chunks, ops, s0, n_seq=n_seq, ncs=ncs)
    y = y_chunks.reshape(g, nc, S5_CHUNK, S5_GROUP).transpose(1, 2, 0, 3).reshape(n, d)
    x_new = _s5_out(y, u, x, mod, d_skip, w_glu, ln_g, ln_b, alpha=alpha, seq_len=seq_len)
    state = fin.reshape(g, n_seq, 2, 2, S5_STATE).transpose(1, 3, 2, 0, 4)
    return x_new, state


def _softplus(x):
    return jnp.maximum(x, 0.0) + jnp.log1p(jnp.exp(-jnp.abs(x)))


def _lru_scan_kernel(xr_ref, gate_ref, cw_ref, cb_ref, w4_ref, b4_ref, lam_ref, h0_ref, y_ref, fin_ref,
                     af_ref, bf_ref, ab_ref, bb_ref, *, n_seq, seq_len, rc):
    n = n_seq * seq_len
    c = LANES
    sp = _softplus(-lam_ref[0])
    cw, cb = cw_ref[0], cb_ref[0]
    w4, b4 = w4_ref[0], b4_ref[0]

    def gates(i, _):
        r0 = pl.multiple_of(i * rc, rc)
        cur = xr_ref[pl.ds(r0, rc), :]
        prev = xr_ref[pl.ds(pl.multiple_of(jnp.maximum(r0 - SUBLANES, 0), SUBLANES), SUBLANES), :]
        nxt = xr_ref[pl.ds(pl.multiple_of(jnp.minimum(r0 + rc, n - SUBLANES), SUBLANES), SUBLANES), :]
        ext = jnp.concatenate([prev, cur, nxt], axis=0)
        t = (r0 + lax.broadcasted_iota(jnp.int32, (rc, c), 0)) % seq_len
        taps = []
        for k in range(CONV_W):
            off = k - CONV_LEFT
            v = ext[SUBLANES + off:SUBLANES + off + rc]
            ok = (t + off >= 0) & (t + off < seq_len)
            taps.append(jnp.where(ok, v, 0.0) * cw[k:k + 1])
        xc = cb + taps[0]
        for v in taps[1:]:
            xc = xc + v
        pre = jnp.dot(xc.astype(BF16), w4, preferred_element_type=F32) + b4
        for d, (a_ref, b_ref) in enumerate(((af_ref, bf_ref), (ab_ref, bb_ref))):
            r = jax.nn.sigmoid(pre[:, 2 * d * c:(2 * d + 1) * c])
            gi = jax.nn.sigmoid(pre[:, (2 * d + 1) * c:(2 * d + 2) * c])
            log_a = -LRU_C * r * sp[d:d + 1]
            a = jnp.exp(log_a)
            a_ref[pl.ds(r0, rc), :] = a
            b_ref[pl.ds(r0, rc), :] = jnp.sqrt(-jnp.tanh(log_a) * (a * a + 1.0)) * (gi * xc)
        return 0

    lax.fori_loop(0, n // rc, gates, 0)

    def step(k, carry):
        hf, hb = carry
        rows_f = pl.ds(k, n_seq, stride=seq_len)
        rows_b = pl.ds(seq_len - 1 - k, n_seq, stride=seq_len)
        hf = af_ref[rows_f, :] * hf + bf_ref[rows_f, :]
        hb = ab_ref[rows_b, :] * hb + bb_ref[rows_b, :]
        bf_ref[rows_f, :] = hf
        bb_ref[rows_b, :] = hb
        return hf, hb

    hf, hb = lax.fori_loop(0, seq_len, step, (h0_ref[0], h0_ref[1]))
    fin_ref[0] = hf
    fin_ref[1] = hb

    def combine(i, _):
        rows = pl.ds(pl.multiple_of(i * rc, rc), rc)
        y_ref[rows, :] = ((bf_ref[rows, :] + bb_ref[rows, :]) * jax.nn.gelu(gate_ref[rows, :])).astype(BF16)
        return 0

    lax.fori_loop(0, n // rc, combine, 0)


def _lru_scan(xg, conv_w, conv_b, w4, b4, lam, h0, *, n_seq, seq_len):
    n, d2 = xg.shape
    d = d2 // 2
    nt = d // LANES
    rc = min(LRU_ROW_CHUNK, seq_len)
    stripe = lambda off: pl.BlockSpec((n, LANES), lambda j: (0, j + off))
    per_t = lambda *s: pl.BlockSpec((1,) + s, lambda j: (j, 0, 0))
    return pl.pallas_call(
        functools.partial(_lru_scan_kernel, n_seq=n_seq, seq_len=seq_len, rc=rc),
        grid=(nt,),
        in_specs=[stripe(0), stripe(nt), per_t(CONV_W, LANES), per_t(1, LANES), per_t(LANES, 4 * LANES),
                  per_t(1, 4 * LANES), per_t(2, LANES), pl.BlockSpec((2, n_seq, LANES), lambda j: (0, 0, j))],
        out_specs=[pl.BlockSpec((n, LANES), lambda j: (0, j)),
                   pl.BlockSpec((2, n_seq, LANES), lambda j: (0, 0, j))],
        out_shape=[jax.ShapeDtypeStruct((n, d), BF16), jax.ShapeDtypeStruct((2, n_seq, d), F32)],
        scratch_shapes=[pltpu.VMEM((n, LANES), F32)] * 4,
        compiler_params=_params("arbitrary"),
        name="rglru_scan",
    )(xg, xg, conv_w, conv_b, w4, b4, lam, h0)


def _lru_out_kernel(y_ref, x_ref, mod_ref, w_ref, g_ref, b_ref, o_ref, *, alpha):
    out = jnp.dot(y_ref[...], w_ref[...], preferred_element_type=F32)
    o_ref[...] = _layer_norm(alpha * x_ref[...] + _gate(mod_ref, 1) * out, g_ref[...], b_ref[...])


def _lru_out(y, x, mod, w_out, ln_g, ln_b, *, alpha, seq_len):
    n, d = x.shape
    tm = min(TOKEN_TILE, n)
    row = pl.BlockSpec((tm, d), lambda i: (i, 0))
    return pl.pallas_call(
        functools.partial(_lru_out_kernel, alpha=alpha),
        grid=(n // tm,),
        in_specs=[row, row, _mod_spec(mod.shape[0], d, tm, seq_len), _const_spec(w_out.shape),
                  _const_spec((1, d)), _const_spec((1, d))],
        out_specs=row,
        out_shape=jax.ShapeDtypeStruct((n, d), F32),
        compiler_params=_params("arbitrary"),
        name="rglru_out_proj",
    )(y, x, mod, w_out, ln_g.reshape(1, d), ln_b.reshape(1, d))


def _lru_stripe_weights(conv_w, conv_b, w_a, b_a, w_x, b_x, lam):
    d = conv_w.shape[1]
    nt = d // LANES
    per = LANES // LRU_BLOCK

    def blockdiag(w):
        wb = w.reshape(nt, per, LRU_BLOCK, LRU_BLOCK)
        eye = jnp.eye(per, dtype=w.dtype)
        return jnp.einsum('tpij,pq->tpiqj', wb, eye).reshape(nt, LANES, LANES)

    w4 = jnp.concatenate([blockdiag(w_a[0]), blockdiag(w_x[0]), blockdiag(w_a[1]), blockdiag(w_x[1])], axis=2)
    b4 = jnp.concatenate([v.reshape(nt, 1, LANES) for v in (b_a[0], b_x[0], b_a[1], b_x[1])], axis=2)
    cw = conv_w.reshape(CONV_W, nt, LANES).transpose(1, 0, 2)
    cb = conv_b.reshape(nt, 1, LANES)
    lam_t = lam.reshape(2, nt, LANES).transpose(1, 0, 2)
    return cw, cb, w4.astype(BF16), b4, lam_t


def _lru_mixer(x, mod, w_in, stripe_w, w_out, ln_g, ln_b, init, *, alpha, n_seq, seq_len):
    n, d = x.shape
    xg = _inproj(x, mod, w_in, sub=1, seq_len=seq_len)
    h0 = jnp.zeros((2, n_seq, d), F32) if init is None else init.transpose(1, 0, 2)
    y, fin = _lru_scan(xg, *stripe_w, h0, n_seq=n_seq, seq_len=seq_len)
    x_new = _lru_out(y, x, mod, w_out, ln_g, ln_b, alpha=alpha, seq_len=seq_len)
    return x_new, fin.transpose(1, 0, 2)


def _grid_pos_embed(rows, d):
    t = jnp.arange(rows * GRID_W)
    row = (t // GRID_W).astype(F32)
    col = (t % GRID_W).astype(F32)
    quarter = d // 4
    omega = 1.0 / (10000.0 ** (jnp.arange(quarter, dtype=F32) / quarter))

    def emb(p):
        ang = p[:, None] * omega[None, :]
        return jnp.concatenate([jnp.sin(ang), jnp.cos(ang)], -1)

    return jnp.concatenate([emb(row), emb(col)], -1)


def _trunk(x3, mod, init_s5, init_lru, wts, pos):
    n_seq, seq_len, d = x3.shape
    depth = mod.shape[0]
    alpha = (2.0 * depth) ** 0.25
    x = x3.reshape(n_seq * seq_len, d)
    st_s5, st_lru = [], []
    for i in range(depth):
        j = i // 2
        m = mod[i]
        x = _ffn(x, m, wts["up"][i][0], wts["dn"][i][0], wts["ln_g"][i, 0], wts["ln_b"][i, 0],
                 sub=0, alpha=alpha, seq_len=seq_len, pos=pos if i == 0 else None)
        if i % 2 == 0:
            init = None if init_s5 is None else init_s5[:, j]
            x, st = _s5_mixer(x, m, wts["s5_w_in"][j], wts["s5_ops"][j], wts["s5_d"][j], wts["s5_w_glu"][j],
                              wts["ln_g"][i, 1], wts["ln_b"][i, 1], init,
                              alpha=alpha, n_seq=n_seq, seq_len=seq_len)
            st_s5.append(st)
        else:
            init = None if init_lru is None else init_lru[:, j]
            x, st = _lru_mixer(x, m, wts["lru_w_in"][j], wts["lru_stripe"][j], wts["lru_w_out"][j],
                               wts["ln_g"][i, 1], wts["ln_b"][i, 1], init,
                               alpha=alpha, n_seq=n_seq, seq_len=seq_len)
            st_lru.append(st)
        x = _ffn(x, m, wts["up"][i][1], wts["dn"][i][1], wts["ln_g"][i, 2], wts["ln_b"][i, 2],
                 sub=2, alpha=alpha, seq_len=seq_len)
    return x.reshape(n_seq, seq_len, d), st_s5, st_lru


def kernel(x_prompt, x_sample, c, state_s5, state_lru, c_ctx, ada_w, ada_b, ln_g, ln_b, ffn_w_up, ffn_w_down, s5_w_in, s5_a_re, s5_a_im, s5_log_dt, s5_b_re, s5_b_im, s5_c_re, s5_c_im, s5_d, s5_w_glu, lru_w_in, lru_conv_w, lru_conv_b, lru_w_a, lru_b_a, lru_w_x, lru_b_x, lru_lambda, lru_w_out):
    depth, d, _ = ada_w.shape
    n_dec = c.shape[0]
    assert 1 + n_dec <= SUBLANES
    cond8 = jnp.concatenate([c_ctx[None, :], c, jnp.zeros((SUBLANES - 1 - n_dec, d), F32)], axis=0)
    mod = _modulation(cond8, ada_w, ada_b).reshape(depth, SUBLANES, 9, d)
    mod_ctx, mod_dec = mod[:, 0:1], mod[:, 1:1 + n_dec]

    up = ffn_w_up.astype(BF16)
    dn = ffn_w_down.astype(BF16)
    wts = {
        "up": [[up[i, s] for s in range(2)] for i in range(depth)],
        "dn": [[dn[i, s] for s in range(2)] for i in range(depth)],
        "ln_g": ln_g, "ln_b": ln_b,
        "s5_w_in": s5_w_in.astype(BF16), "s5_d": s5_d, "s5_w_glu": s5_w_glu.astype(BF16),
        "s5_ops": [_s5_operators(s5_a_re[j], s5_a_im[j], s5_log_dt[j], s5_b_re[j], s5_b_im[j],
                                 s5_c_re[j], s5_c_im[j]) for j in range(s5_w_in.shape[0])],
        "lru_w_in": lru_w_in.astype(BF16), "lru_w_out": lru_w_out.astype(BF16),
        "lru_stripe": [_lru_stripe_weights(lru_conv_w[j], lru_conv_b[j], lru_w_a[j], lru_b_a[j], lru_w_x[j],
                                           lru_b_x[j], lru_lambda[j]) for j in range(lru_w_in.shape[0])],
    }

    y_prompt, st_s5, st_lru = _trunk(x_prompt, mod_ctx, None, None, wts, None)
    new_state_s5 = jnp.stack(st_s5, 1).astype(x_prompt.dtype)
    new_state_lru = jnp.stack(st_lru, 1).astype(x_prompt.dtype)

    pos = _grid_pos_embed(x_sample.shape[1] // GRID_W, d).astype(x_sample.dtype)
    y_sample, _, _ = _trunk(x_sample, mod_dec, state_s5, state_lru, wts, pos)
    return (y_prompt, y_sample, new_state_s5, new_state_lru)
```

```python
import functools

import jax
import jax.numpy as jnp
from jax import lax
from jax.experimental import pallas as pl
from jax.experimental.pallas import tpu as pltpu

F32 = jnp.float32
BF16 = jnp.bfloat16

GRID_W = 64
S5_GROUP = 16
S5_STATE = 64
LRU_BLOCK = 64
LRU_C = 8.0
CONV_W = 4
CONV_LEFT = 2
LN_EPS = 1e-5

LANES = 128
SUBLANES = 8
VMEM_LIMIT_BYTES = 56 * 1024 * 1024

S5_CHUNK = 16
S5_CW = S5_CHUNK * S5_GROUP
S5_SW = 4 * S5_STATE
TOKEN_TILE = 512
LRU_SEGMENT = 256
HIGHEST = lax.Precision.HIGHEST


def _params(*sem):
    return pltpu.CompilerParams(dimension_semantics=sem, vmem_limit_bytes=VMEM_LIMIT_BYTES)


def _const_spec(shape):
    nd = len(shape)
    return pl.BlockSpec(shape, lambda *_: (0,) * nd, pipeline_mode=pl.Buffered(1))


def _layer_norm(z, g, b):
    mu = jnp.mean(z, axis=-1, keepdims=True)
    zc = z - mu
    var = jnp.mean(zc * zc, axis=-1, keepdims=True)
    return zc * lax.rsqrt(var + LN_EPS) * g + b


def _modulate(x, mod_ref, sub):
    sh = mod_ref[0, 3 * sub:3 * sub + 1, :]
    sc = mod_ref[0, 3 * sub + 1:3 * sub + 2, :]
    return x * (1.0 + sc) + sh


def _gate(mod_ref, sub):
    return mod_ref[0, 3 * sub + 2:3 * sub + 3, :]


def _mod_spec(n_cond, d, tm, seq_len):
    if n_cond == 1:
        return pl.BlockSpec((1, 9, d), lambda i: (0, 0, 0))
    return pl.BlockSpec((1, 9, d), lambda i: ((i * tm) // seq_len, 0, 0))


def _mod_kernel(c_ref, w_ref, b_ref, o_ref):
    s = jax.nn.silu(c_ref[...]).astype(BF16)
    o_ref[0] = jnp.dot(s, w_ref[0].astype(BF16), preferred_element_type=F32) + b_ref[0]


def _modulation(cond8, ada_w, ada_b):
    depth, d, n9 = ada_w.shape
    tn = n9 // 4
    return pl.pallas_call(
        _mod_kernel,
        grid=(depth, n9 // tn),
        in_specs=[pl.BlockSpec((SUBLANES, d), lambda l, n: (0, 0)),
                  pl.BlockSpec((1, d, tn), lambda l, n: (l, 0, n)),
                  pl.BlockSpec((1, 1, tn), lambda l, n: (l, 0, n))],
        out_specs=pl.BlockSpec((1, SUBLANES, tn), lambda l, n: (l, 0, n)),
        out_shape=jax.ShapeDtypeStruct((depth, SUBLANES, n9), F32),
        compiler_params=_params("arbitrary", "arbitrary"),
        name="adaln_modulation",
    )(cond8, ada_w, ada_b.reshape(depth, 1, n9))


def _ffn_kernel(*refs, sub, alpha, d_ff, ck, n_prompt_rows, has_pos):
    if has_pos:
        x_ref, pos_ref, mod_ref, wup_ref, wdn_ref, g_ref, b_ref, o_ref, act_ref = refs
        x = x_ref[...] + pos_ref[...]
    else:
        x_ref, mod_ref, wup_ref, wdn_ref, g_ref, b_ref, o_ref, act_ref = refs
        x = x_ref[...]
    del n_prompt_rows
    h = _modulate(x, mod_ref, sub).astype(BF16)
    for c0 in range(0, d_ff, ck):
        a = jnp.dot(h, wup_ref[:, c0:c0 + ck], preferred_element_type=F32)
        g = jnp.dot(h, wup_ref[:, d_ff + c0:d_ff + c0 + ck], preferred_element_type=F32)
        act_ref[:, c0:c0 + ck] = (jax.nn.silu(g) * a).astype(BF16)
    f = jnp.dot(act_ref[...], wdn_ref[...], preferred_element_type=F32)
    z = alpha * x + (0.5 * _gate(mod_ref, sub)) * f
    o_ref[...] = _layer_norm(z, g_ref[...], b_ref[...])


def _ffn(x, mod, w_up, w_dn, ln_g, ln_b, *, sub, alpha, seq_len, pos=None):
    n, d = x.shape
    d_ff = w_dn.shape[0]
    tm = min(TOKEN_TILE, n)
    ck = 256 if d_ff % 256 == 0 else d_ff
    row = pl.BlockSpec((tm, d), lambda i: (i, 0))
    in_specs = [row]
    args = [x]
    if pos is not None:
        per_seq = seq_len // tm
        in_specs.append(pl.BlockSpec((tm, d), lambda i: (i % per_seq, 0)))
        args.append(pos)
    in_specs += [_mod_spec(mod.shape[0], d, tm, seq_len), _const_spec(w_up.shape), _const_spec(w_dn.shape),
                 _const_spec((1, d)), _const_spec((1, d))]
    args += [mod, w_up, w_dn, ln_g.reshape(1, d), ln_b.reshape(1, d)]
    return pl.pallas_call(
        functools.partial(_ffn_kernel, sub=sub, alpha=alpha, d_ff=d_ff, ck=ck, n_prompt_rows=0,
                          has_pos=pos is not None),
        grid=(n // tm,),
        in_specs=in_specs,
        out_specs=row,
        out_shape=jax.ShapeDtypeStruct((n, d), F32),
        scratch_shapes=[pltpu.VMEM((tm, d_ff), BF16)],
        compiler_params=_params("arbitrary"),
        name="ffn_sublayer",
    )(*args)


def _inproj_kernel(x_ref, mod_ref, w_ref, o_ref, *, sub):
    h = _modulate(x_ref[...], mod_ref, sub).astype(BF16)
    o_ref[...] = jnp.dot(h, w_ref[...], preferred_element_type=F32)


def _inproj(x, mod, w, *, sub, seq_len):
    n, d = x.shape
    dn = w.shape[1]
    tm = min(TOKEN_TILE, n)
    return pl.pallas_call(
        functools.partial(_inproj_kernel, sub=sub),
        grid=(n // tm,),
        in_specs=[pl.BlockSpec((tm, d), lambda i: (i, 0)), _mod_spec(mod.shape[0], d, tm, seq_len),
                  _const_spec(w.shape)],
        out_specs=pl.BlockSpec((tm, dn), lambda i: (i, 0)),
        out_shape=jax.ShapeDtypeStruct((n, dn), F32),
        compiler_params=_params("arbitrary"),
        name="mixer_in_proj",
    )(x, mod, w)


def _s5_discretize(are, aim, ldt):
    dt = jnp.exp(ldt)
    mag = jnp.exp(dt * are)
    abr = mag * jnp.cos(dt * aim)
    abi = mag * jnp.sin(dt * aim)
    den = are * are + aim * aim
    nr, ni = abr - 1.0, abi
    qr = (nr * are + ni * aim) / den
    qi = (ni * are - nr * aim) / den
    return abr, abi, qr, qi


def _s5_ops_kernel(arow_ref, acol_ref, btr_ref, bti_ref, ctr_ref, cti_ref,
                   m_ref, mst_ref, min_ref, a16_ref):
    t_chunk, grp, w = S5_CHUNK, S5_GROUP, S5_CW
    arow, acol = arow_ref[0], acol_ref[0]
    btr, bti = btr_ref[0], bti_ref[0]
    ctr, cti = ctr_ref[0], cti_ref[0]
    colblk = lax.broadcasted_iota(jnp.int32, (w, w), 1) // grp
    colblk_p = lax.broadcasted_iota(jnp.int32, (S5_STATE, w), 1) // grp

    lag_tabs, decay, carry_in = [], [], []
    for d in range(2):
        abr, abi, qr, qi = _s5_discretize(arow[d:d + 1], arow[2 + d:3 + d], arow[4 + d:5 + d])
        lr = [qr * btr - qi * bti]
        li = [qr * bti + qi * btr]
        er, ei = abr, abi
        for _ in range(t_chunk - 1):
            lr.append(lr[-1] * abr - li[-1] * abi)
            li.append(lr[-2] * abi + li[-1] * abr)
            er, ei = er * abr - ei * abi, er * abi + ei * abr
        decay.append((er, ei))
        if d == 0:
            lr, li = lr[::-1], li[::-1]
        lag_tabs.append((jnp.concatenate(lr, axis=0), jnp.concatenate(li, axis=0)))

        cbr, cbi, _, _ = _s5_discretize(acol[:, d:d + 1], acol[:, 2 + d:3 + d], acol[:, 4 + d:5 + d])
        pr, pi = cbr, cbi
        er_t = jnp.zeros((S5_STATE, w), F32)
        ei_t = jnp.zeros((S5_STATE, w), F32)
        for n in range(1, t_chunk + 1):
            blk = n - 1 if d == 0 else t_chunk - n
            er_t = jnp.where(colblk_p == blk, pr, er_t)
            ei_t = jnp.where(colblk_p == blk, pi, ei_t)
            pr, pi = pr * cbr - pi * cbi, pr * cbi + pi * cbr
        carry_in.append((ctr * er_t - cti * ei_t, ctr * ei_t + cti * er_t))

    acc = jnp.zeros((w, w), F32)
    for d in range(2):
        l_re, l_im = lag_tabs[d]
        tab = (jnp.dot(l_re, ctr, preferred_element_type=F32, precision=HIGHEST)
               - jnp.dot(l_im, cti, preferred_element_type=F32, precision=HIGHEST))
        for tp in range(t_chunk):
            if d == 0:
                cut = (t_chunk - 1 - tp) * grp
                shifted = tab[cut:] if cut == 0 else jnp.concatenate([tab[cut:], jnp.zeros((cut, w), F32)], 0)
            else:
                cut = tp * grp
                shifted = tab if cut == 0 else jnp.concatenate([jnp.zeros((cut, w), F32), tab[:w - cut]], 0)
            acc = acc + jnp.where(colblk == tp, shifted, 0.0)
    m_ref[0] = acc.astype(BF16)

    (fr, fi), (br, bi) = lag_tabs
    mst_ref[0] = jnp.concatenate([fr, br, fi, bi], axis=1).astype(BF16)
    (wfr, wfi), (wbr, wbi) = carry_in
    min_ref[0] = jnp.concatenate([wfr, wbr, -wfi, -wbi], axis=0).astype(BF16)
    (efr, efi), (ebr, ebi) = decay
    a16_ref[0] = jnp.concatenate([efr, ebr, efi, ebi], axis=1)


def _s5_operators(a_re, a_im, log_dt, b_re, b_im, c_re, c_im):
    _, g, p = a_re.shape
    ldt = jnp.broadcast_to(log_dt[:, :, None], (2, g, p))
    rows = jnp.concatenate([a_re, a_im, ldt, jnp.zeros((2, g, p), F32)], axis=0)
    arow = rows.transpose(1, 0, 2)
    acol = rows.transpose(1, 2, 0)
    bt_re, bt_im = b_re.transpose(0, 2, 1), b_im.transpose(0, 2, 1)
    ct_re = jnp.tile(c_re.transpose(0, 2, 1), (1, 1, S5_CHUNK))
    ct_im = jnp.tile(c_im.transpose(0, 2, 1), (1, 1, S5_CHUNK))
    per_g = lambda *s: pl.BlockSpec((1,) + s, lambda i: (i, 0, 0))
    w = S5_CW
    return pl.pallas_call(
        _s5_ops_kernel,
        grid=(g,),
        in_specs=[per_g(SUBLANES, p), per_g(p, SUBLANES), per_g(S5_GROUP, p), per_g(S5_GROUP, p),
                  per_g(p, w), per_g(p, w)],
        out_specs=[per_g(w, w), per_g(w, S5_SW), per_g(S5_SW, w), per_g(1, S5_SW)],
        out_shape=[jax.ShapeDtypeStruct((g, w, w), BF16), jax.ShapeDtypeStruct((g, w, S5_SW), BF16),
                   jax.ShapeDtypeStruct((g, S5_SW, w), BF16), jax.ShapeDtypeStruct((g, 1, S5_SW), F32)],
        compiler_params=_params("arbitrary"),
        name="s5_operators",
    )(arow, acol, bt_re, bt_im, ct_re, ct_im)


def _lane_block_transpose(vs):
    blk = lax.broadcasted_iota(jnp.int32, vs[0].shape, 1) // S5_GROUP
    for dist in (4, 2, 1):
        keep = (blk & dist) == 0
        out = list(vs)
        for r in range(len(vs)):
            if r & dist:
                continue
            lo, hi = vs[r], vs[r + dist]
            out[r] = jnp.where(keep, lo, pltpu.roll(hi, dist * S5_GROUP, axis=1))
            out[r + dist] = jnp.where(keep, pltpu.roll(lo, LANES - dist * S5_GROUP, axis=1), hi)
        vs = out
    return vs


def _s5_in_kernel(x_ref, mod_ref, w_ref, o_ref):
    nch = x_ref.shape[0]
    d = w_ref.shape[0]
    sh = mod_ref[0, 3:4, :]
    sc = mod_ref[0, 4:5, :]
    h = jnp.concatenate([(x_ref[:, t * d:(t + 1) * d] * (1.0 + sc) + sh).astype(BF16)
                         for t in range(S5_CHUNK)], axis=0)
    u = jnp.dot(h, w_ref[...], preferred_element_type=F32)
    per = LANES // S5_GROUP
    for v in range(d // LANES):
        for th in range(S5_CHUNK // per):
            src = [u[(per * th + r) * nch:(per * th + r + 1) * nch, v * LANES:(v + 1) * LANES] for r in range(per)]
            dst = _lane_block_transpose(src)
            for q in range(per):
                o_ref[per * v + q, :, th * LANES:(th + 1) * LANES] = dst[q]


def _s5_in(x, mod, w, *, seq_len):
    n, d = x.shape
    g = d // S5_GROUP
    nch = TOKEN_TILE // S5_CHUNK
    nc = n // S5_CHUNK
    return pl.pallas_call(
        _s5_in_kernel,
        grid=(nc // nch,),
        in_specs=[pl.BlockSpec((nch, S5_CHUNK * d), lambda i: (i, 0)),
                  _mod_spec(mod.shape[0], d, TOKEN_TILE, seq_len), _const_spec(w.shape)],
        out_specs=pl.BlockSpec((g, nch, S5_CW), lambda i: (0, i, 0)),
        out_shape=jax.ShapeDtypeStruct((g, nc, S5_CW), F32),
        compiler_params=_params("arbitrary"),
        name="s5_in_proj",
    )(x.reshape(nc, S5_CHUNK * d), mod, w)


def _s5_mix_kernel(u_ref, m_ref, mst_ref, min_ref, a16_ref, dsk_ref, s0_ref, y_ref, fin_ref,
                   loc_ref, sa_ref, sb_ref, *, n_seq, ncs):
    half = S5_SW // 2
    u = u_ref[0]
    ub = u.astype(BF16)
    loc = jnp.dot(ub, mst_ref[0], preferred_element_type=F32)
    loc_ref[0] = loc[:, :half]
    loc_ref[1] = loc[:, half:]
    a = a16_ref[0]
    ar, ai = a[:, :half], a[:, half:]
    is_fwd = lax.broadcasted_iota(jnp.int32, (n_seq, half), 1) < S5_STATE
    s0 = s0_ref[0]

    def step(k, carry):
        sr, si = carry
        rows_f = pl.ds(k, n_seq, stride=ncs)
        rows_b = pl.ds(ncs - 1 - k, n_seq, stride=ncs)
        sa_ref[0, rows_f, :] = sr
        sa_ref[1, rows_f, :] = si
        sb_ref[0, rows_b, :] = sr
        sb_ref[1, rows_b, :] = si
        lr = jnp.where(is_fwd, loc_ref[0, rows_f, :], loc_ref[0, rows_b, :])
        li = jnp.where(is_fwd, loc_ref[1, rows_f, :], loc_ref[1, rows_b, :])
        return ar * sr - ai * si + lr, ar * si + ai * sr + li

    sr, si = lax.fori_loop(0, ncs, step, (s0[:, :half], s0[:, half:]))
    fin_ref[0] = jnp.concatenate([sr, si], axis=1)
    sel = lax.broadcasted_iota(jnp.int32, sa_ref.shape[1:], 1) < S5_STATE
    s_in = jnp.concatenate([jnp.where(sel, sa_ref[0], sb_ref[0]), jnp.where(sel, sa_ref[1], sb_ref[1])], axis=1)
    y_ref[0] = (jnp.dot(ub, m_ref[0], preferred_element_type=F32)
                + jnp.dot(s_in.astype(BF16), min_ref[0], preferred_element_type=F32)
                + dsk_ref[0] * u)


def _s5_mix(u_chunks, ops, d_skip, s0, *, n_seq, ncs):
    g, nc, w = u_chunks.shape
    m, mst, m_in, a16 = ops
    dsk = jnp.tile(d_skip.reshape(g, 1, S5_GROUP), (1, 1, S5_CHUNK))
    per_g = lambda *s: pl.BlockSpec((1,) + s, lambda i: (i, 0, 0))
    return pl.pallas_call(
        functools.partial(_s5_mix_kernel, n_seq=n_seq, ncs=ncs),
        grid=(g,),
        in_specs=[per_g(nc, w), per_g(w, w), per_g(w, S5_SW), per_g(S5_SW, w), per_g(1, S5_SW), per_g(1, w),
                  per_g(n_seq, S5_SW)],
        out_specs=[per_g(nc, w), per_g(n_seq, S5_SW)],
        out_shape=[jax.ShapeDtypeStruct((g, nc, w), F32), jax.ShapeDtypeStruct((g, n_seq, S5_SW), F32)],
        scratch_shapes=[pltpu.VMEM((2, nc, S5_SW // 2), F32)] * 3,
        compiler_params=_params("arbitrary"),
        name="s5_chunk_scan",
    )(u_chunks, m, mst, m_in, a16, dsk, s0)


def _s5_out_kernel(y_ref, x_ref, mod_ref, w_ref, g_ref, b_ref, o_ref, *, alpha):
    nch = x_ref.shape[0]
    d = w_ref.shape[0]
    per = LANES // S5_GROUP
    phase = [[None] * (d // LANES) for _ in range(S5_CHUNK)]
    for v in range(d // LANES):
        for th in range(S5_CHUNK // per):
            src = [y_ref[per * v + q, :, th * LANES:(th + 1) * LANES] for q in range(per)]
            dst = _lane_block_transpose(src)
            for r in range(per):
                phase[per * th + r][v] = dst[r]
    y = jnp.concatenate([jnp.concatenate(p, axis=1) for p in phase], axis=0)
    z = jnp.dot(jax.nn.gelu(y).astype(BF16), w_ref[...], preferred_element_type=F32)
    out = z[:, :d] * jax.nn.sigmoid(z[:, d:])
    x = jnp.concatenate([x_ref[:, t * d:(t + 1) * d] for t in range(S5_CHUNK)], axis=0)
    res = _layer_norm(alpha * x + _gate(mod_ref, 1) * out, g_ref[...], b_ref[...])
    for t in range(S5_CHUNK):
        o_ref[:, t * d:(t + 1) * d] = res[t * nch:(t + 1) * nch]


def _s5_out(y_chunks, x, mod, w_glu, ln_g, ln_b, *, alpha, seq_len):
    n, d = x.shape
    g = d // S5_GROUP
    nch = TOKEN_TILE // S5_CHUNK
    nc = n // S5_CHUNK
    row = pl.BlockSpec((nch, S5_CHUNK * d), lambda i: (i, 0))
    return pl.pallas_call(
        functools.partial(_s5_out_kernel, alpha=alpha),
        grid=(nc // nch,),
        in_specs=[pl.BlockSpec((g, nch, S5_CW), lambda i: (0, i, 0)), row,
                  _mod_spec(mod.shape[0], d, TOKEN_TILE, seq_len), _const_spec(w_glu.shape),
                  _const_spec((1, d)), _const_spec((1, d))],
        out_specs=row,
        out_shape=jax.ShapeDtypeStruct((nc, S5_CHUNK * d), F32),
        compiler_params=_params("arbitrary"),
        name="s5_out_proj",
    )(y_chunks, x.reshape(nc, S5_CHUNK * d), mod, w_glu, ln_g.reshape(1, d), ln_b.reshape(1, d)).reshape(n, d)


def _s5_mixer(x, mod, w_in, ops, d_skip, w_glu, ln_g, ln_b, init, *, alpha, n_seq, seq_len):
    n, d = x.shape
    g = d // S5_GROUP
    ncs = seq_len // S5_CHUNK
    u_chunks = _s5_in(x, mod, w_in, seq_len=seq_len)
    if init is None:
        s0 = jnp.zeros((g, n_seq, S5_SW), F32)
    else:
        s0 = init.transpose(3, 0, 2, 1, 4).reshape(g, n_seq, S5_SW)
    y_chunks, fin = _s5_mix(u_chunks, ops, d_skip, s0, n_seq=n_seq, ncs=ncs)
    x_new = _s5_out(y_chunks, x, mod, w_glu, ln_g, ln_b, alpha=alpha, seq_len=seq_len)
    state = fin.reshape(g, n_seq, 2, 2, S5_STATE).transpose(1, 3, 2, 0, 4)
    return x_new, state


def _softplus(x):
    return jnp.maximum(x, 0.0) + jnp.log1p(jnp.exp(-jnp.abs(x)))


def _sigmoid(x):
    return 0.5 * jnp.tanh(0.5 * x) + 0.5


def _lru_scan_kernel(xr_ref, gate_ref, cw_ref, cb_ref, w4_ref, b4_ref, lam_ref, h0_ref, y_ref, fin_ref,
                     af_ref, bf_ref, ab_ref, bb_ref, end_ref, *, n_seq, seq_len, seg):
    n = n_seq * seq_len
    n_seg = seq_len // seg
    ns = n_seq * n_seg
    pitch = seg + SUBLANES
    c = LANES
    chained = n_seg > 1
    sp = _softplus(-lam_ref[0])
    cw, cb = cw_ref[0], cb_ref[0]
    w4, b4 = w4_ref[0], b4_ref[0]

    def gates(s, _):
        r0 = pl.multiple_of(s * seg, seg)
        p0 = pl.multiple_of(s * pitch, SUBLANES)
        cur = xr_ref[pl.ds(r0, seg), :]
        prev = xr_ref[pl.ds(pl.multiple_of(jnp.maximum(r0 - SUBLANES, 0), SUBLANES), SUBLANES), :]
        nxt = xr_ref[pl.ds(pl.multiple_of(jnp.minimum(r0 + seg, n - SUBLANES), SUBLANES), SUBLANES), :]
        prev = jnp.where(s % n_seg == 0, 0.0, prev)
        nxt = jnp.where(s % n_seg == n_seg - 1, 0.0, nxt)
        ext = jnp.concatenate([prev, cur, nxt], axis=0)
        xc = cb + ext[SUBLANES - CONV_LEFT:SUBLANES - CONV_LEFT + seg] * cw[0:1]
        for k in range(1, CONV_W):
            off = SUBLANES + k - CONV_LEFT
            xc = xc + ext[off:off + seg] * cw[k:k + 1]
        pre = jnp.dot(xc.astype(BF16), w4, preferred_element_type=F32) + b4
        for d, (a_ref, b_ref) in enumerate(((af_ref, bf_ref), (ab_ref, bb_ref))):
            r = _sigmoid(pre[:, 2 * d * c:(2 * d + 1) * c])
            gi = _sigmoid(pre[:, (2 * d + 1) * c:(2 * d + 2) * c])
            log_a = -LRU_C * r * sp[d:d + 1]
            a = jnp.exp(log_a)
            a_ref[pl.ds(p0, seg), :] = a
            b_ref[pl.ds(p0, seg), :] = jnp.sqrt(-jnp.tanh(log_a) * (a * a + 1.0)) * (gi * xc)
        return 0

    lax.fori_loop(0, ns, gates, 0)

    def step(k, carry):
        hf, hb, pf, pb = carry
        rows_f = pl.ds(k, ns, stride=pitch)
        rows_b = pl.ds(seg - 1 - k, ns, stride=pitch)
        af, ab = af_ref[rows_f, :], ab_ref[rows_b, :]
        hf = af * hf + bf_ref[rows_f, :]
        hb = ab * hb + bb_ref[rows_b, :]
        bf_ref[rows_f, :] = hf
        bb_ref[rows_b, :] = hb
        if chained:
            pf, pb = pf * af, pb * ab
            af_ref[rows_f, :] = pf
            ab_ref[rows_b, :] = pb
        return hf, hb, pf, pb

    one = jnp.ones((ns, c), F32)
    if chained:
        init = (jnp.zeros((ns, c), F32), jnp.zeros((ns, c), F32), one, one)
    else:
        init = (h0_ref[0], h0_ref[1], one, one)
    hf, hb, pf, pb = lax.fori_loop(0, seg, step, init, unroll=2)

    if chained:
        end_ref[0], end_ref[1], end_ref[2], end_ref[3] = hf, hb, pf, pb
        hin_f, hin_b = h0_ref[0], h0_ref[1]
        for j in range(n_seg):
            jf, jb = pl.ds(j, n_seq, stride=n_seg), pl.ds(n_seg - 1 - j, n_seq, stride=n_seg)
            end_ref[4, jf, :] = hin_f
            end_ref[5, jb, :] = hin_b
            hin_f = end_ref[2, jf, :] * hin_f + end_ref[0, jf, :]
            hin_b = end_ref[3, jb, :] * hin_b + end_ref[1, jb, :]
        fin_ref[0] = hin_f
        fin_ref[1] = hin_b
    else:
        fin_ref[0] = hf
        fin_ref[1] = hb

    def combine(s, _):
        rows = pl.ds(pl.multiple_of(s * seg, seg), seg)
        prow = pl.ds(pl.multiple_of(s * pitch, SUBLANES), seg)
        h = bf_ref[prow, :] + bb_ref[prow, :]
        if chained:
            h = h + af_ref[prow, :] * end_ref[4, pl.ds(s, 1), :] + ab_ref[prow, :] * end_ref[5, pl.ds(s, 1), :]
        y_ref[rows, :] = (h * jax.nn.gelu(gate_ref[rows, :])).astype(BF16)
        return 0

    lax.fori_loop(0, ns, combine, 0)


def _lru_scan(xg, conv_w, conv_b, w4, b4, lam, h0, *, n_seq, seq_len):
    n, d2 = xg.shape
    d = d2 // 2
    nt = d // LANES
    seg = min(LRU_SEGMENT, seq_len)
    ns = n // seg
    stripe = lambda off: pl.BlockSpec((n, LANES), lambda j: (0, j + off))
    per_t = lambda *s: pl.BlockSpec((1,) + s, lambda j: (j, 0, 0))
    return pl.pallas_call(
        functools.partial(_lru_scan_kernel, n_seq=n_seq, seq_len=seq_len, seg=seg),
        grid=(nt,),
        in_specs=[stripe(0), stripe(nt), per_t(CONV_W, LANES), per_t(1, LANES), per_t(LANES, 4 * LANES),
                  per_t(1, 4 * LANES), per_t(2, LANES), pl.BlockSpec((2, n_seq, LANES), lambda j: (0, 0, j))],
        out_specs=[pl.BlockSpec((n, LANES), lambda j: (0, j)),
                   pl.BlockSpec((2, n_seq, LANES), lambda j: (0, 0, j))],
        out_shape=[jax.ShapeDtypeStruct((n, d), BF16), jax.ShapeDtypeStruct((2, n_seq, d), F32)],
        scratch_shapes=[pltpu.VMEM((ns * (seg + SUBLANES), LANES), F32)] * 4 + [pltpu.VMEM((6, ns, LANES), F32)],
        compiler_params=_params("arbitrary"),
        name="rglru_scan",
    )(xg, xg, conv_w, conv_b, w4, b4, lam, h0)


def _lru_out_kernel(y_ref, x_ref, mod_ref, w_ref, g_ref, b_ref, o_ref, *, alpha):
    out = jnp.dot(y_ref[...], w_ref[...], preferred_element_type=F32)
    o_ref[...] = _layer_norm(alpha * x_ref[...] + _gate(mod_ref, 1) * out, g_ref[...], b_ref[...])


def _lru_out(y, x, mod, w_out, ln_g, ln_b, *, alpha, seq_len):
    n, d = x.shape
    tm = min(TOKEN_TILE, n)
    row = pl.BlockSpec((tm, d), lambda i: (i, 0))
    return pl.pallas_call(
        functools.partial(_lru_out_kernel, alpha=alpha),
        grid=(n // tm,),
        in_specs=[row, row, _mod_spec(mod.shape[0], d, tm, seq_len), _const_spec(w_out.shape),
                  _const_spec((1, d)), _const_spec((1, d))],
        out_specs=row,
        out_shape=jax.ShapeDtypeStruct((n, d), F32),
        compiler_params=_params("arbitrary"),
        name="rglru_out_proj",
    )(y, x, mod, w_out, ln_g.reshape(1, d), ln_b.reshape(1, d))


def _lru_stripe_weights(conv_w, conv_b, w_a, b_a, w_x, b_x, lam):
    d = conv_w.shape[1]
    nt = d // LANES
    per = LANES // LRU_BLOCK

    def blockdiag(w):
        wb = w.reshape(nt, per, LRU_BLOCK, LRU_BLOCK)
        eye = jnp.eye(per, dtype=w.dtype)
        return jnp.einsum('tpij,pq->tpiqj', wb, eye).reshape(nt, LANES, LANES)

    w4 = jnp.concatenate([blockdiag(w_a[0]), blockdiag(w_x[0]), blockdiag(w_a[1]), blockdiag(w_x[1])], axis=2)
    b4 = jnp.concatenate([v.reshape(nt, 1, LANES) for v in (b_a[0], b_x[0], b_a[1], b_x[1])], axis=2)
    cw = conv_w.reshape(CONV_W, nt, LANES).transpose(1, 0, 2)
    cb = conv_b.reshape(nt, 1, LANES)
    lam_t = lam.reshape(2, nt, LANES).transpose(1, 0, 2)
    return cw, cb, w4.astype(BF16), b4, lam_t


def _lru_mixer(x, mod, w_in, stripe_w, w_out, ln_g, ln_b, init, *, alpha, n_seq, seq_len):
    n, d = x.shape
    xg = _inproj(x, mod, w_in, sub=1, seq_len=seq_len)
    h0 = jnp.zeros((2, n_seq, d), F32) if init is None else init.transpose(1, 0, 2)
    y, fin = _lru_scan(xg, *stripe_w, h0, n_seq=n_seq, seq_len=seq_len)
    x_new = _lru_out(y, x, mod, w_out, ln_g, ln_b, alpha=alpha, seq_len=seq_len)
    return x_new, fin.transpose(1, 0, 2)


def _grid_pos_embed(rows, d):
    t = jnp.arange(rows * GRID_W)
    row = (t // GRID_W).astype(F32)
    col = (t % GRID_W).astype(F32)
    quarter = d // 4
    omega = 1.0 / (10000.0 ** (jnp.arange(quarter, dtype=F32) / quarter))

    def emb(p):
        ang = p[:, None] * omega[None, :]
        return jnp.concatenate([jnp.sin(ang), jnp.cos(ang)], -1)

    return jnp.concatenate([emb(row), emb(col)], -1)


def _trunk(x3, mod, init_s5, init_lru, wts, pos):
    n_seq, seq_len, d = x3.shape
    depth = mod.shape[0]
    alpha = (2.0 * depth) ** 0.25
    x = x3.reshape(n_seq * seq_len, d)
    st_s5, st_lru = [], []
    for i in range(depth):
        j = i // 2
        m = mod[i]
        x = _ffn(x, m, wts["up"][i][0], wts["dn"][i][0], wts["ln_g"][i, 0], wts["ln_b"][i, 0],
                 sub=0, alpha=alpha, seq_len=seq_len, pos=pos if i == 0 else None)
        if i % 2 == 0:
            init = None if init_s5 is None else init_s5[:, j]
            x, st = _s5_mixer(x, m, wts["s5_w_in"][j], wts["s5_ops"][j], wts["s5_d"][j], wts["s5_w_glu"][j],
                              wts["ln_g"][i, 1], wts["ln_b"][i, 1], init,
                              alpha=alpha, n_seq=n_seq, seq_len=seq_len)
            st_s5.append(st)
        else:
            init = None if init_lru is None else init_lru[:, j]
            x, st = _lru_mixer(x, m, wts["lru_w_in"][j], wts["lru_stripe"][j], wts["lru_w_out"][j],
                               wts["ln_g"][i, 1], wts["ln_b"][i, 1], init,
                               alpha=alpha, n_seq=n_seq, seq_len=seq_len)
            st_lru.append(st)
        x = _ffn(x, m, wts["up"][i][1], wts["dn"][i][1], wts["ln_g"][i, 2], wts["ln_b"][i, 2],
                 sub=2, alpha=alpha, seq_len=seq_len)
    return x.reshape(n_seq, seq_len, d), st_s5, st_lru


def kernel(x_prompt, x_sample, c, state_s5, state_lru, c_ctx, ada_w, ada_b, ln_g, ln_b, ffn_w_up, ffn_w_down, s5_w_in, s5_a_re, s5_a_im, s5_log_dt, s5_b_re, s5_b_im, s5_c_re, s5_c_im, s5_d, s5_w_glu, lru_w_in, lru_conv_w, lru_conv_b, lru_w_a, lru_b_a, lru_w_x, lru_b_x, lru_lambda, lru_w_out):
    depth, d, _ = ada_w.shape
    n_dec = c.shape[0]
    assert 1 + n_dec <= SUBLANES
    cond8 = jnp.concatenate([c_ctx[None, :], c, jnp.zeros((SUBLANES - 1 - n_dec, d), F32)], axis=0)
    mod = _modulation(cond8, ada_w, ada_b).reshape(depth, SUBLANES, 9, d)
    mod_ctx, mod_dec = mod[:, 0:1], mod[:, 1:1 + n_dec]

    up = ffn_w_up.astype(BF16)
    dn = ffn_w_down.astype(BF16)
    wts = {
        "up": [[up[i, s] for s in range(2)] for i in range(depth)],
        "dn": [[dn[i, s] for s in range(2)] for i in range(depth)],
        "ln_g": ln_g, "ln_b": ln_b,
        "s5_w_in": s5_w_in.astype(BF16), "s5_d": s5_d, "s5_w_glu": s5_w_glu.astype(BF16),
        "s5_ops": [_s5_operators(s5_a_re[j], s5_a_im[j], s5_log_dt[j], s5_b_re[j], s5_b_im[j],
                                 s5_c_re[j], s5_c_im[j]) for j in range(s5_w_in.shape[0])],
        "lru_w_in": lru_w_in.astype(BF16), "lru_w_out": lru_w_out.astype(BF16),
        "lru_stripe": [_lru_stripe_weights(lru_conv_w[j], lru_conv_b[j], lru_w_a[j], lru_b_a[j], lru_w_x[j],
                                           lru_b_x[j], lru_lambda[j]) for j in range(lru_w_in.shape[0])],
    }

    y_prompt, st_s5, st_lru = _trunk(x_prompt, mod_ctx, None, None, wts, None)
    new_state_s5 = jnp.stack(st_s5, 1).astype(x_prompt.dtype)
    new_state_lru = jnp.stack(st_lru, 1).astype(x_prompt.dtype)

    pos = _grid_pos_embed(x_sample.shape[1] // GRID_W, d).astype(x_sample.dtype)
    y_sample, _, _ = _trunk(x_sample, mod_dec, state_s5, state_lru, wts, pos)
    return (y_prompt, y_sample, new_state_s5, new_state_lru)
```

```python
import functools

import jax
import jax.numpy as jnp
from jax import lax
from jax.experimental import pallas as pl
from jax.experimental.pallas import tpu as pltpu

F32 = jnp.float32
BF16 = jnp.bfloat16

GRID_W = 64
S5_GROUP = 16
S5_STATE = 64
LRU_BLOCK = 64
LRU_C = 8.0
CONV_W = 4
CONV_LEFT = 2
LN_EPS = 1e-5

LANES = 128
SUBLANES = 8
VMEM_LIMIT_BYTES = 56 * 1024 * 1024

S5_CHUNK = 16
S5_CW = S5_CHUNK * S5_GROUP
S5_SW = 4 * S5_STATE
S5_GROUP_BLOCK = 8
TOKEN_TILE = 512
LRU_SEGMENT = 256
HIGHEST = lax.Precision.HIGHEST


def _params(*sem):
    return pltpu.CompilerParams(dimension_semantics=sem, vmem_limit_bytes=VMEM_LIMIT_BYTES)


def _const_spec(shape):
    nd = len(shape)
    return pl.BlockSpec(shape, lambda *_: (0,) * nd, pipeline_mode=pl.Buffered(1))


def _layer_norm(z, g, b):
    mu = jnp.mean(z, axis=-1, keepdims=True)
    zc = z - mu
    var = jnp.mean(zc * zc, axis=-1, keepdims=True)
    return zc * lax.rsqrt(var + LN_EPS) * g + b


def _modulate(x, mod_ref, sub):
    sh = mod_ref[0, 3 * sub:3 * sub + 1, :]
    sc = mod_ref[0, 3 * sub + 1:3 * sub + 2, :]
    return x * (1.0 + sc) + sh


def _gate(mod_ref, sub):
    return mod_ref[0, 3 * sub + 2:3 * sub + 3, :]


def _mod_spec(n_cond, d, tm, seq_len):
    if n_cond == 1:
        return pl.BlockSpec((1, 9, d), lambda i: (0, 0, 0))
    return pl.BlockSpec((1, 9, d), lambda i: ((i * tm) // seq_len, 0, 0))


def _mod_kernel(c_ref, w_ref, b_ref, o_ref):
    s = jax.nn.silu(c_ref[...]).astype(BF16)
    o_ref[0] = jnp.dot(s, w_ref[0].astype(BF16), preferred_element_type=F32) + b_ref[0]


def _modulation(cond8, ada_w, ada_b):
    depth, d, n9 = ada_w.shape
    tn = n9 // 4
    return pl.pallas_call(
        _mod_kernel,
        grid=(depth, n9 // tn),
        in_specs=[pl.BlockSpec((SUBLANES, d), lambda l, n: (0, 0)),
                  pl.BlockSpec((1, d, tn), lambda l, n: (l, 0, n)),
                  pl.BlockSpec((1, 1, tn), lambda l, n: (l, 0, n))],
        out_specs=pl.BlockSpec((1, SUBLANES, tn), lambda l, n: (l, 0, n)),
        out_shape=jax.ShapeDtypeStruct((depth, SUBLANES, n9), F32),
        compiler_params=_params("arbitrary", "arbitrary"),
        name="adaln_modulation",
    )(cond8, ada_w, ada_b.reshape(depth, 1, n9))


def _store_rows_permuted(res, o_ref, perm_ref, *, to_phase_major):
    tm, d = res.shape
    nch = tm // S5_CHUNK
    for v in range(d // LANES):
        lanes = slice(v * LANES, (v + 1) * LANES)
        if to_phase_major:
            perm_ref[v] = res[:, lanes]
            for t in range(S5_CHUNK):
                o_ref[t * nch:(t + 1) * nch, lanes] = perm_ref[v, pl.ds(t, nch, stride=S5_CHUNK), :]
        else:
            for t in range(S5_CHUNK):
                perm_ref[v, pl.ds(t, nch, stride=S5_CHUNK), :] = res[t * nch:(t + 1) * nch, lanes]
            o_ref[:, lanes] = perm_ref[v]


def _ffn_kernel(*refs, sub, alpha, d_ff, ck, has_pos, row_order):
    if has_pos:
        x_ref, pos_ref, mod_ref, wup_ref, wdn_ref, g_ref, b_ref, o_ref, act_ref, perm_ref = refs
        x = x_ref[...] + pos_ref[...]
    else:
        x_ref, mod_ref, wup_ref, wdn_ref, g_ref, b_ref, o_ref, act_ref, perm_ref = refs
        x = x_ref[...]
    h = _modulate(x, mod_ref, sub).astype(BF16)
    for c0 in range(0, d_ff, ck):
        a = jnp.dot(h, wup_ref[:, c0:c0 + ck], preferred_element_type=F32)
        g = jnp.dot(h, wup_ref[:, d_ff + c0:d_ff + c0 + ck], preferred_element_type=F32)
        act_ref[:, c0:c0 + ck] = (jax.nn.silu(g) * a).astype(BF16)
    f = jnp.dot(act_ref[...], wdn_ref[...], preferred_element_type=F32)
    z = alpha * x + (0.5 * _gate(mod_ref, sub)) * f
    res = _layer_norm(z, g_ref[...], b_ref[...])
    if row_order is None:
        o_ref[...] = res
    else:
        _store_rows_permuted(res, o_ref, perm_ref, to_phase_major=row_order == "to_phase_major")


def _ffn(x, mod, w_up, w_dn, ln_g, ln_b, *, sub, alpha, seq_len, pos=None, row_order=None):
    n, d = x.shape
    d_ff = w_dn.shape[0]
    tm = min(TOKEN_TILE, n)
    ck = 256 if d_ff % 256 == 0 else d_ff
    row = pl.BlockSpec((tm, d), lambda i: (i, 0))
    in_specs = [row]
    args = [x]
    if pos is not None:
        per_seq = seq_len // tm
        in_specs.append(pl.BlockSpec((tm, d), lambda i: (i % per_seq, 0)))
        args.append(pos)
    in_specs += [_mod_spec(mod.shape[0], d, tm, seq_len), _const_spec(w_up.shape), _const_spec(w_dn.shape),
                 _const_spec((1, d)), _const_spec((1, d))]
    args += [mod, w_up, w_dn, ln_g.reshape(1, d), ln_b.reshape(1, d)]
    perm_shape = (d // LANES, tm, LANES) if row_order else (1, SUBLANES, LANES)
    return pl.pallas_call(
        functools.partial(_ffn_kernel, sub=sub, alpha=alpha, d_ff=d_ff, ck=ck, has_pos=pos is not None,
                          row_order=row_order),
        grid=(n // tm,),
        in_specs=in_specs,
        out_specs=row,
        out_shape=jax.ShapeDtypeStruct((n, d), F32),
        scratch_shapes=[pltpu.VMEM((tm, d_ff), BF16), pltpu.VMEM(perm_shape, F32)],
        compiler_params=_params("arbitrary"),
        name="ffn_sublayer",
    )(*args)


def _inproj_kernel(x_ref, mod_ref, w_ref, o_ref, *, sub):
    h = _modulate(x_ref[...], mod_ref, sub).astype(BF16)
    o_ref[...] = jnp.dot(h, w_ref[...], preferred_element_type=F32)


def _inproj(x, mod, w, *, sub, seq_len):
    n, d = x.shape
    dn = w.shape[1]
    tm = min(TOKEN_TILE, n)
    return pl.pallas_call(
        functools.partial(_inproj_kernel, sub=sub),
        grid=(n // tm,),
        in_specs=[pl.BlockSpec((tm, d), lambda i: (i, 0)), _mod_spec(mod.shape[0], d, tm, seq_len),
                  _const_spec(w.shape)],
        out_specs=pl.BlockSpec((tm, dn), lambda i: (i, 0)),
        out_shape=jax.ShapeDtypeStruct((n, dn), F32),
        compiler_params=_params("arbitrary"),
        name="mixer_in_proj",
    )(x, mod, w)


def _s5_discretize(are, aim, ldt):
    dt = jnp.exp(ldt)
    mag = jnp.exp(dt * are)
    abr = mag * jnp.cos(dt * aim)
    abi = mag * jnp.sin(dt * aim)
    den = are * are + aim * aim
    nr, ni = abr - 1.0, abi
    qr = (nr * are + ni * aim) / den
    qi = (ni * are - nr * aim) / den
    return abr, abi, qr, qi


def _s5_ops_kernel(arow_ref, acol_ref, btr_ref, bti_ref, ctr_ref, cti_ref,
                   m_ref, mst_ref, min_ref, a16_ref):
    t_chunk, grp, w = S5_CHUNK, S5_GROUP, S5_CW
    arow, acol = arow_ref[0], acol_ref[0]
    btr, bti = btr_ref[0], bti_ref[0]
    ctr, cti = ctr_ref[0], cti_ref[0]
    colblk = lax.broadcasted_iota(jnp.int32, (w, w), 1) // grp
    colblk_p = lax.broadcasted_iota(jnp.int32, (S5_STATE, w), 1) // grp

    abr2, abi2, qr2, qi2 = _s5_discretize(arow[0:2], arow[2:4], arow[4:6])
    cbr2, cbi2, _, _ = _s5_discretize(acol[:, 0:2], acol[:, 2:4], acol[:, 4:6])
    lag_tabs, decay, carry_in = [], [], []
    for d in range(2):
        abr, abi, qr, qi = abr2[d:d + 1], abi2[d:d + 1], qr2[d:d + 1], qi2[d:d + 1]
        lr = [qr * btr - qi * bti]
        li = [qr * bti + qi * btr]
        er, ei = abr, abi
        for _ in range(t_chunk - 1):
            lr.append(lr[-1] * abr - li[-1] * abi)
            li.append(lr[-2] * abi + li[-1] * abr)
            er, ei = er * abr - ei * abi, er * abi + ei * abr
        decay.append((er, ei))
        if d == 0:
            lr, li = lr[::-1], li[::-1]
        lag_tabs.append((jnp.concatenate(lr, axis=0), jnp.concatenate(li, axis=0)))

        cbr, cbi = cbr2[:, d:d + 1], cbi2[:, d:d + 1]
        pr, pi = cbr, cbi
        er_t = jnp.zeros((S5_STATE, w), F32)
        ei_t = jnp.zeros((S5_STATE, w), F32)
        for n in range(1, t_chunk + 1):
            blk = n - 1 if d == 0 else t_chunk - n
            er_t = jnp.where(colblk_p == blk, pr, er_t)
            ei_t = jnp.where(colblk_p == blk, pi, ei_t)
            pr, pi = pr * cbr - pi * cbi, pr * cbi + pi * cbr
        carry_in.append((ctr * er_t - cti * ei_t, ctr * ei_t + cti * er_t))

    lhs = jnp.concatenate([jnp.concatenate([l_re, -l_im], axis=1) for l_re, l_im in lag_tabs], axis=0)
    tab = jnp.dot(lhs, jnp.concatenate([ctr, cti], axis=0), preferred_element_type=F32, precision=HIGHEST)
    tab_f, tab_b = tab[:w], tab[w:]
    acc = jnp.zeros((w, w), F32)
    for tp in range(t_chunk):
        cut_f = (t_chunk - 1 - tp) * grp
        cut_b = tp * grp
        sh_f = tab_f if cut_f == 0 else jnp.concatenate([tab_f[cut_f:], jnp.zeros((cut_f, w), F32)], 0)
        sh_b = tab_b if cut_b == 0 else jnp.concatenate([jnp.zeros((cut_b, w), F32), tab_b[:w - cut_b]], 0)
        acc = jnp.where(colblk == tp, sh_f + sh_b, acc)
    m_ref[0] = acc.astype(BF16)

    (fr, fi), (br, bi) = lag_tabs
    mst_ref[0] = jnp.concatenate([fr, br, fi, bi], axis=1).astype(BF16)
    (wfr, wfi), (wbr, wbi) = carry_in
    min_ref[0] = jnp.concatenate([wfr, wbr, -wfi, -wbi], axis=0).astype(BF16)
    (efr, efi), (ebr, ebi) = decay
    a16_ref[0] = jnp.concatenate([efr, ebr, efi, ebi], axis=1)


def _s5_operators(a_re, a_im, log_dt, b_re, b_im, c_re, c_im):
    _, g, p = a_re.shape
    ldt = jnp.broadcast_to(log_dt[:, :, None], (2, g, p))
    rows = jnp.concatenate([a_re, a_im, ldt, jnp.zeros((2, g, p), F32)], axis=0)
    arow = rows.transpose(1, 0, 2)
    acol = rows.transpose(1, 2, 0)
    bt_re, bt_im = b_re.transpose(0, 2, 1), b_im.transpose(0, 2, 1)
    ct_re = jnp.tile(c_re.transpose(0, 2, 1), (1, 1, S5_CHUNK))
    ct_im = jnp.tile(c_im.transpose(0, 2, 1), (1, 1, S5_CHUNK))
    per_g = lambda *s: pl.BlockSpec((1,) + s, lambda i: (i, 0, 0))
    w = S5_CW
    return pl.pallas_call(
        _s5_ops_kernel,
        grid=(g,),
        in_specs=[per_g(SUBLANES, p), per_g(p, SUBLANES), per_g(S5_GROUP, p), per_g(S5_GROUP, p),
                  per_g(p, w), per_g(p, w)],
        out_specs=[per_g(w, w), per_g(w, S5_SW), per_g(S5_SW, w), per_g(1, S5_SW)],
        out_shape=[jax.ShapeDtypeStruct((g, w, w), BF16), jax.ShapeDtypeStruct((g, w, S5_SW), BF16),
                   jax.ShapeDtypeStruct((g, S5_SW, w), BF16), jax.ShapeDtypeStruct((g, 1, S5_SW), F32)],
        compiler_params=_params("arbitrary"),
        name="s5_operators",
    )(arow, acol, bt_re, bt_im, ct_re, ct_im)


def _lane_block_transpose(vs):
    blk = lax.broadcasted_iota(jnp.int32, vs[0].shape, 1) // S5_GROUP
    for dist in (4, 2, 1):
        keep = (blk & dist) == 0
        out = list(vs)
        for r in range(len(vs)):
            if r & dist:
                continue
            lo, hi = vs[r], vs[r + dist]
            out[r] = jnp.where(keep, lo, pltpu.roll(hi, dist * S5_GROUP, axis=1))
            out[r + dist] = jnp.where(keep, pltpu.roll(lo, LANES - dist * S5_GROUP, axis=1), hi)
        vs = out
    return vs


def _s5_in_kernel(x_ref, mod_ref, w_ref, o_ref):
    nch = x_ref.shape[0] // S5_CHUNK
    d = w_ref.shape[0]
    h = _modulate(x_ref[...], mod_ref, 1).astype(BF16)
    u = jnp.dot(h, w_ref[...], preferred_element_type=F32)
    per = LANES // S5_GROUP
    for v in range(d // LANES):
        for th in range(S5_CHUNK // per):
            src = [u[(per * th + r) * nch:(per * th + r + 1) * nch, v * LANES:(v + 1) * LANES] for r in range(per)]
            dst = _lane_block_transpose(src)
            for q in range(per):
                o_ref[per * v + q, :, th * LANES:(th + 1) * LANES] = dst[q]


def _s5_in(x, mod, w, *, seq_len):
    n, d = x.shape
    g = d // S5_GROUP
    nch = TOKEN_TILE // S5_CHUNK
    nc = n // S5_CHUNK
    return pl.pallas_call(
        _s5_in_kernel,
        grid=(nc // nch,),
        in_specs=[pl.BlockSpec((TOKEN_TILE, d), lambda i: (i, 0)),
                  _mod_spec(mod.shape[0], d, TOKEN_TILE, seq_len), _const_spec(w.shape)],
        out_specs=pl.BlockSpec((g, nch, S5_CW), lambda i: (0, i, 0)),
        out_shape=jax.ShapeDtypeStruct((g, nc, S5_CW), F32),
        compiler_params=_params("arbitrary"),
        name="s5_in_proj",
    )(x, mod, w)


def _s5_mix_kernel(u_ref, m_ref, mst_ref, min_ref, a16_ref, dsk_ref, s0_ref, y_ref, fin_ref,
                   loc_ref, sa_ref, sb_ref, *, n_seq, ncs):
    gb = u_ref.shape[0]
    half = S5_SW // 2
    pitch = ncs + SUBLANES
    for g in range(gb):
        loc = jnp.dot(u_ref[g].astype(BF16), mst_ref[g], preferred_element_type=F32)
        for s in range(n_seq):
            loc_ref[0, g, s * pitch:s * pitch + ncs, :] = loc[s * ncs:(s + 1) * ncs, :half]
            loc_ref[1, g, s * pitch:s * pitch + ncs, :] = loc[s * ncs:(s + 1) * ncs, half:]
    is_fwd = lax.broadcasted_iota(jnp.int32, (n_seq, half), 1) < S5_STATE
    decay = [(a16_ref[g][:, :half], a16_ref[g][:, half:]) for g in range(gb)]

    def step(k, carry):
        rows_f = pl.ds(k, n_seq, stride=pitch)
        rows_b = pl.ds(ncs - 1 - k, n_seq, stride=pitch)
        out = []
        for g in range(gb):
            sr, si = carry[2 * g], carry[2 * g + 1]
            ar, ai = decay[g]
            sa_ref[0, g, rows_f, :] = sr
            sa_ref[1, g, rows_f, :] = si
            sb_ref[0, g, rows_b, :] = sr
            sb_ref[1, g, rows_b, :] = si
            lr = jnp.where(is_fwd, loc_ref[0, g, rows_f, :], loc_ref[0, g, rows_b, :])
            li = jnp.where(is_fwd, loc_ref[1, g, rows_f, :], loc_ref[1, g, rows_b, :])
            out += [ar * sr - ai * si + lr, ar * si + ai * sr + li]
        return tuple(out)

    init = []
    for g in range(gb):
        init += [s0_ref[g][:, :half], s0_ref[g][:, half:]]
    fin = lax.fori_loop(0, ncs, step, tuple(init))
    sel = lax.broadcasted_iota(jnp.int32, (ncs, half), 1) < S5_STATE
    for g in range(gb):
        fin_ref[g] = jnp.concatenate([fin[2 * g], fin[2 * g + 1]], axis=1)
        parts = []
        for ri in range(2):
            parts.append(jnp.concatenate(
                [jnp.where(sel, sa_ref[ri, g, s * pitch:s * pitch + ncs, :], sb_ref[ri, g, s * pitch:s * pitch + ncs, :])
                 for s in range(n_seq)], axis=0))
        s_in = jnp.concatenate(parts, axis=1).astype(BF16)
        u = u_ref[g]
        y_ref[g] = (jnp.dot(u.astype(BF16), m_ref[g], preferred_element_type=F32)
                    + jnp.dot(s_in, min_ref[g], preferred_element_type=F32)
                    + dsk_ref[g] * u)


def _s5_mix(u_chunks, ops, d_skip, s0, *, n_seq, ncs):
    g, nc, w = u_chunks.shape
    m, mst, m_in, a16 = ops
    gb = S5_GROUP_BLOCK
    dsk = jnp.tile(d_skip.reshape(g, 1, S5_GROUP), (1, 1, S5_CHUNK))
    per_g = lambda *s: pl.BlockSpec((gb,) + s, lambda i: (i, 0, 0))
    rows = n_seq * (ncs + SUBLANES)
    return pl.pallas_call(
        functools.partial(_s5_mix_kernel, n_seq=n_seq, ncs=ncs),
        grid=(g // gb,),
        in_specs=[per_g(nc, w), per_g(w, w), per_g(w, S5_SW), per_g(S5_SW, w), per_g(1, S5_SW), per_g(1, w),
                  per_g(n_seq, S5_SW)],
        out_specs=[per_g(nc, w), per_g(n_seq, S5_SW)],
        out_shape=[jax.ShapeDtypeStruct((g, nc, w), F32), jax.ShapeDtypeStruct((g, n_seq, S5_SW), F32)],
        scratch_shapes=[pltpu.VMEM((2, gb, rows, S5_SW // 2), F32)] * 3,
        compiler_params=_params("arbitrary"),
        name="s5_chunk_scan",
    )(u_chunks, m, mst, m_in, a16, dsk, s0)


def _s5_out_kernel(y_ref, x_ref, mod_ref, w_ref, g_ref, b_ref, o_ref, *, alpha):
    d = w_ref.shape[0]
    per = LANES // S5_GROUP
    phase = [[None] * (d // LANES) for _ in range(S5_CHUNK)]
    for v in range(d // LANES):
        for th in range(S5_CHUNK // per):
            src = [y_ref[per * v + q, :, th * LANES:(th + 1) * LANES] for q in range(per)]
            dst = _lane_block_transpose(src)
            for r in range(per):
                phase[per * th + r][v] = dst[r]
    y = jnp.concatenate([jnp.concatenate(p, axis=1) for p in phase], axis=0)
    z = jnp.dot(jax.nn.gelu(y).astype(BF16), w_ref[...], preferred_element_type=F32)
    out = z[:, :d] * jax.nn.sigmoid(z[:, d:])
    o_ref[...] = _layer_norm(alpha * x_ref[...] + _gate(mod_ref, 1) * out, g_ref[...], b_ref[...])


def _s5_out(y_chunks, x, mod, w_glu, ln_g, ln_b, *, alpha, seq_len):
    n, d = x.shape
    g = d // S5_GROUP
    nch = TOKEN_TILE // S5_CHUNK
    nc = n // S5_CHUNK
    row = pl.BlockSpec((TOKEN_TILE, d), lambda i: (i, 0))
    return pl.pallas_call(
        functools.partial(_s5_out_kernel, alpha=alpha),
        grid=(nc // nch,),
        in_specs=[pl.BlockSpec((g, nch, S5_CW), lambda i: (0, i, 0)), row,
                  _mod_spec(mod.shape[0], d, TOKEN_TILE, seq_len), _const_spec(w_glu.shape),
                  _const_spec((1, d)), _const_spec((1, d))],
        out_specs=row,
        out_shape=jax.ShapeDtypeStruct((n, d), F32),
        compiler_params=_params("arbitrary"),
        name="s5_out_proj",
    )(y_chunks, x, mod, w_glu, ln_g.reshape(1, d), ln_b.reshape(1, d))


def _s5_mixer(x, mod, w_in, ops, d_skip, w_glu, ln_g, ln_b, init, *, alpha, n_seq, seq_len):
    n, d = x.shape
    g = d // S5_GROUP
    ncs = seq_len // S5_CHUNK
    u_chunks = _s5_in(x, mod, w_in, seq_len=seq_len)
    if init is None:
        s0 = jnp.zeros((g, n_seq, S5_SW), F32)
    else:
        s0 = init.transpose(3, 0, 2, 1, 4).reshape(g, n_seq, S5_SW)
    y_chunks, fin = _s5_mix(u_chunks, ops, d_skip, s0, n_seq=n_seq, ncs=ncs)
    x_new = _s5_out(y_chunks, x, mod, w_glu, ln_g, ln_b, alpha=alpha, seq_len=seq_len)
    state = fin.reshape(g, n_seq, 2, 2, S5_STATE).transpose(1, 3, 2, 0, 4)
    return x_new, state


def _softplus(x):
    return jnp.maximum(x, 0.0) + jnp.log1p(jnp.exp(-jnp.abs(x)))


def _sigmoid(x):
    return 0.5 * jnp.tanh(0.5 * x) + 0.5


def _lru_scan_kernel(xr_ref, gate_ref, cw_ref, cb_ref, w4_ref, b4_ref, lam_ref, h0_ref, y_ref, fin_ref,
                     af_ref, bf_ref, ab_ref, bb_ref, end_ref, *, n_seq, seq_len, seg):
    n = n_seq * seq_len
    n_seg = seq_len // seg
    ns = n_seq * n_seg
    pitch = seg + SUBLANES
    c = LANES
    chained = n_seg > 1
    sp = _softplus(-lam_ref[0])
    cw, cb = cw_ref[0], cb_ref[0]
    w4, b4 = w4_ref[0], b4_ref[0]

    def gates(s, _):
        r0 = pl.multiple_of(s * seg, seg)
        p0 = pl.multiple_of(s * pitch, SUBLANES)
        cur = xr_ref[pl.ds(r0, seg), :]
        prev = xr_ref[pl.ds(pl.multiple_of(jnp.maximum(r0 - SUBLANES, 0), SUBLANES), SUBLANES), :]
        nxt = xr_ref[pl.ds(pl.multiple_of(jnp.minimum(r0 + seg, n - SUBLANES), SUBLANES), SUBLANES), :]
        prev = jnp.where(s % n_seg == 0, 0.0, prev)
        nxt = jnp.where(s % n_seg == n_seg - 1, 0.0, nxt)
        ext = jnp.concatenate([prev, cur, nxt], axis=0)
        xc = cb + ext[SUBLANES - CONV_LEFT:SUBLANES - CONV_LEFT + seg] * cw[0:1]
        for k in range(1, CONV_W):
            off = SUBLANES + k - CONV_LEFT
            xc = xc + ext[off:off + seg] * cw[k:k + 1]
        pre = jnp.dot(xc.astype(BF16), w4, preferred_element_type=F32) + b4
        for d, (a_ref, b_ref) in enumerate(((af_ref, bf_ref), (ab_ref, bb_ref))):
            r = _sigmoid(pre[:, 2 * d * c:(2 * d + 1) * c])
            gi = _sigmoid(pre[:, (2 * d + 1) * c:(2 * d + 2) * c])
            log_a = -LRU_C * r * sp[d:d + 1]
            a = jnp.exp(log_a)
            a_ref[pl.ds(p0, seg), :] = a
            b_ref[pl.ds(p0, seg), :] = jnp.sqrt(-jnp.tanh(log_a) * (a * a + 1.0)) * (gi * xc)
        return 0

    lax.fori_loop(0, ns, gates, 0)

    def step(k, carry):
        hf, hb, pf, pb = carry
        rows_f = pl.ds(k, ns, stride=pitch)
        rows_b = pl.ds(seg - 1 - k, ns, stride=pitch)
        af, ab = af_ref[rows_f, :], ab_ref[rows_b, :]
        hf = af * hf + bf_ref[rows_f, :]
        hb = ab * hb + bb_ref[rows_b, :]
        bf_ref[rows_f, :] = hf
        bb_ref[rows_b, :] = hb
        if chained:
            pf, pb = pf * af, pb * ab
            af_ref[rows_f, :] = pf
            ab_ref[rows_b, :] = pb
        return hf, hb, pf, pb

    one = jnp.ones((ns, c), F32)
    if chained:
        init = (jnp.zeros((ns, c), F32), jnp.zeros((ns, c), F32), one, one)
    else:
        init = (h0_ref[0], h0_ref[1], one, one)
    hf, hb, pf, pb = lax.fori_loop(0, seg, step, init, unroll=2)

    if chained:
        end_ref[0], end_ref[1], end_ref[2], end_ref[3] = hf, hb, pf, pb
        hin_f, hin_b = h0_ref[0], h0_ref[1]
        for j in range(n_seg):
            jf, jb = pl.ds(j, n_seq, stride=n_seg), pl.ds(n_seg - 1 - j, n_seq, stride=n_seg)
            end_ref[4, jf, :] = hin_f
            end_ref[5, jb, :] = hin_b
            hin_f = end_ref[2, jf, :] * hin_f + end_ref[0, jf, :]
            hin_b = end_ref[3, jb, :] * hin_b + end_ref[1, jb, :]
        fin_ref[0] = hin_f
        fin_ref[1] = hin_b
    else:
        fin_ref[0] = hf
        fin_ref[1] = hb

    def combine(s, _):
        rows = pl.ds(pl.multiple_of(s * seg, seg), seg)
        prow = pl.ds(pl.multiple_of(s * pitch, SUBLANES), seg)
        h = bf_ref[prow, :] + bb_ref[prow, :]
        if chained:
            h = h + af_ref[prow, :] * end_ref[4, pl.ds(s, 1), :] + ab_ref[prow, :] * end_ref[5, pl.ds(s, 1), :]
        y_ref[rows, :] = (h * jax.nn.gelu(gate_ref[rows, :])).astype(BF16)
        return 0

    lax.fori_loop(0, ns, combine, 0)


def _lru_scan(xg, conv_w, conv_b, w4, b4, lam, h0, *, n_seq, seq_len):
    n, d2 = xg.shape
    d = d2 // 2
    nt = d // LANES
    seg = min(LRU_SEGMENT, seq_len)
    ns = n // seg
    stripe = lambda off: pl.BlockSpec((n, LANES), lambda j: (0, j + off))
    per_t = lambda *s: pl.BlockSpec((1,) + s, lambda j: (j, 0, 0))
    return pl.pallas_call(
        functools.partial(_lru_scan_kernel, n_seq=n_seq, seq_len=seq_len, seg=seg),
        grid=(nt,),
        in_specs=[stripe(0), stripe(nt), per_t(CONV_W, LANES), per_t(1, LANES), per_t(LANES, 4 * LANES),
                  per_t(1, 4 * LANES), per_t(2, LANES), pl.BlockSpec((2, n_seq, LANES), lambda j: (0, 0, j))],
        out_specs=[pl.BlockSpec((n, LANES), lambda j: (0, j)),
                   pl.BlockSpec((2, n_seq, LANES), lambda j: (0, 0, j))],
        out_shape=[jax.ShapeDtypeStruct((n, d), BF16), jax.ShapeDtypeStruct((2, n_seq, d), F32)],
        scratch_shapes=[pltpu.VMEM((ns * (seg + SUBLANES), LANES), F32)] * 4 + [pltpu.VMEM((6, ns, LANES), F32)],
        compiler_params=_params("arbitrary"),
        name="rglru_scan",
    )(xg, xg, conv_w, conv_b, w4, b4, lam, h0)


def _lru_out_kernel(y_ref, x_ref, mod_ref, w_ref, g_ref, b_ref, o_ref, *, alpha):
    out = jnp.dot(y_ref[...], w_ref[...], preferred_element_type=F32)
    o_ref[...] = _layer_norm(alpha * x_ref[...] + _gate(mod_ref, 1) * out, g_ref[...], b_ref[...])


def _lru_out(y, x, mod, w_out, ln_g, ln_b, *, alpha, seq_len):
    n, d = x.shape
    tm = min(TOKEN_TILE, n)
    row = pl.BlockSpec((tm, d), lambda i: (i, 0))
    return pl.pallas_call(
        functools.partial(_lru_out_kernel, alpha=alpha),
        grid=(n // tm,),
        in_specs=[row, row, _mod_spec(mod.shape[0], d, tm, seq_len), _const_spec(w_out.shape),
                  _const_spec((1, d)), _const_spec((1, d))],
        out_specs=row,
        out_shape=jax.ShapeDtypeStruct((n, d), F32),
        compiler_params=_params("arbitrary"),
        name="rglru_out_proj",
    )(y, x, mod, w_out, ln_g.reshape(1, d), ln_b.reshape(1, d))


def _lru_stripe_weights(conv_w, conv_b, w_a, b_a, w_x, b_x, lam):
    d = conv_w.shape[1]
    nt = d // LANES
    per = LANES // LRU_BLOCK

    def blockdiag(w):
        wb = w.reshape(nt, per, LRU_BLOCK, LRU_BLOCK)
        eye = jnp.eye(per, dtype=w.dtype)
        return jnp.einsum('tpij,pq->tpiqj', wb, eye).reshape(nt, LANES, LANES)

    w4 = jnp.concatenate([blockdiag(w_a[0]), blockdiag(w_x[0]), blockdiag(w_a[1]), blockdiag(w_x[1])], axis=2)
    b4 = jnp.concatenate([v.reshape(nt, 1, LANES) for v in (b_a[0], b_x[0], b_a[1], b_x[1])], axis=2)
    cw = conv_w.reshape(CONV_W, nt, LANES).transpose(1, 0, 2)
    cb = conv_b.reshape(nt, 1, LANES)
    lam_t = lam.reshape(2, nt, LANES).transpose(1, 0, 2)
    return cw, cb, w4.astype(BF16), b4, lam_t


def _lru_mixer(x, mod, w_in, stripe_w, w_out, ln_g, ln_b, init, *, alpha, n_seq, seq_len):
    n, d = x.shape
    xg = _inproj(x, mod, w_in, sub=1, seq_len=seq_len)
    h0 = jnp.zeros((2, n_seq, d), F32) if init is None else init.transpose(1, 0, 2)
    y, fin = _lru_scan(xg, *stripe_w, h0, n_seq=n_seq, seq_len=seq_len)
    x_new = _lru_out(y, x, mod, w_out, ln_g, ln_b, alpha=alpha, seq_len=seq_len)
    return x_new, fin.transpose(1, 0, 2)


def _grid_pos_embed(rows, d):
    t = jnp.arange(rows * GRID_W)
    row = (t // GRID_W).astype(F32)
    col = (t % GRID_W).astype(F32)
    quarter = d // 4
    omega = 1.0 / (10000.0 ** (jnp.arange(quarter, dtype=F32) / quarter))

    def emb(p):
        ang = p[:, None] * omega[None, :]
        return jnp.concatenate([jnp.sin(ang), jnp.cos(ang)], -1)

    return jnp.concatenate([emb(row), emb(col)], -1)


def _trunk(x3, mod, init_s5, init_lru, wts, pos):
    n_seq, seq_len, d = x3.shape
    depth = mod.shape[0]
    alpha = (2.0 * depth) ** 0.25
    x = x3.reshape(n_seq * seq_len, d)
    st_s5, st_lru = [], []
    for i in range(depth):
        j = i // 2
        m = mod[i]
        x = _ffn(x, m, wts["up"][i][0], wts["dn"][i][0], wts["ln_g"][i, 0], wts["ln_b"][i, 0],
                 sub=0, alpha=alpha, seq_len=seq_len, pos=pos if i == 0 else None,
                 row_order="to_phase_major" if i % 2 == 0 else None)
        if i % 2 == 0:
            init = None if init_s5 is None else init_s5[:, j]
            x, st = _s5_mixer(x, m, wts["s5_w_in"][j], wts["s5_ops"][j], wts["s5_d"][j], wts["s5_w_glu"][j],
                              wts["ln_g"][i, 1], wts["ln_b"][i, 1], init,
                              alpha=alpha, n_seq=n_seq, seq_len=seq_len)
            st_s5.append(st)
        else:
            init = None if init_lru is None else init_lru[:, j]
            x, st = _lru_mixer(x, m, wts["lru_w_in"][j], wts["lru_stripe"][j], wts["lru_w_out"][j],
                               wts["ln_g"][i, 1], wts["ln_b"][i, 1], init,
                               alpha=alpha, n_seq=n_seq, seq_len=seq_len)
            st_lru.append(st)
        x = _ffn(x, m, wts["up"][i][1], wts["dn"][i][1], wts["ln_g"][i, 2], wts["ln_b"][i, 2],
                 sub=2, alpha=alpha, seq_len=seq_len, row_order="to_natural" if i % 2 == 0 else None)
    return x.reshape(n_seq, seq_len, d), st_s5, st_lru


def kernel(x_prompt, x_sample, c, state_s5, state_lru, c_ctx, ada_w, ada_b, ln_g, ln_b, ffn_w_up, ffn_w_down, s5_w_in, s5_a_re, s5_a_im, s5_log_dt, s5_b_re, s5_b_im, s5_c_re, s5_c_im, s5_d, s5_w_glu, lru_w_in, lru_conv_w, lru_conv_b, lru_w_a, lru_b_a, lru_w_x, lru_b_x, lru_lambda, lru_w_out):
    depth, d, _ = ada_w.shape
    n_dec = c.shape[0]
    assert 1 + n_dec <= SUBLANES
    cond8 = jnp.concatenate([c_ctx[None, :], c, jnp.zeros((SUBLANES - 1 - n_dec, d), F32)], axis=0)
    mod = _modulation(cond8, ada_w, ada_b).reshape(depth, SUBLANES, 9, d)
    mod_ctx, mod_dec = mod[:, 0:1], mod[:, 1:1 + n_dec]

    up = ffn_w_up.astype(BF16)
    dn = ffn_w_down.astype(BF16)
    wts = {
        "up": [[up[i, s] for s in range(2)] for i in range(depth)],
        "dn": [[dn[i, s] for s in range(2)] for i in range(depth)],
        "ln_g": ln_g, "ln_b": ln_b,
        "s5_w_in": s5_w_in.astype(BF16), "s5_d": s5_d, "s5_w_glu": s5_w_glu.astype(BF16),
        "s5_ops": [_s5_operators(s5_a_re[j], s5_a_im[j], s5_log_dt[j], s5_b_re[j], s5_b_im[j],
                                 s5_c_re[j], s5_c_im[j]) for j in range(s5_w_in.shape[0])],
        "lru_w_in": lru_w_in.astype(BF16), "lru_w_out": lru_w_out.astype(BF16),
        "lru_stripe": [_lru_stripe_weights(lru_conv_w[j], lru_conv_b[j], lru_w_a[j], lru_b_a[j], lru_w_x[j],
                                           lru_b_x[j], lru_lambda[j]) for j in range(lru_w_in.shape[0])],
    }

    y_prompt, st_s5, st_lru = _trunk(x_prompt, mod_ctx, None, None, wts, None)
    new_state_s5 = jnp.stack(st_s5, 1).astype(x_prompt.dtype)
    new_state_lru = jnp.stack(st_lru, 1).astype(x_prompt.dtype)

    pos = _grid_pos_embed(x_sample.shape[1] // GRID_W, d).astype(x_sample.dtype)
    y_sample, _, _ = _trunk(x_sample, mod_dec, state_s5, state_lru, wts, pos)
    return (y_prompt, y_sample, new_state_s5, new_state_lru)
```

```python
import functools

import jax
import jax.numpy as jnp
from jax import lax
from jax.experimental import pallas as pl
from jax.experimental.pallas import tpu as pltpu

F32 = jnp.float32
BF16 = jnp.bfloat16

GRID_W = 64
S5_GROUP = 16
S5_STATE = 64
LRU_BLOCK = 64
LRU_C = 8.0
CONV_W = 4
CONV_LEFT = 2
LN_EPS = 1e-5

LANES = 128
SUBLANES = 8
VMEM_LIMIT_BYTES = 56 * 1024 * 1024

S5_CHUNK = 16
S5_CW = S5_CHUNK * S5_GROUP
S5_SW = 4 * S5_STATE
S5_GROUP_BLOCK = 8
TOKEN_TILE = 512
FFN_SUBTILES = 2
LRU_SEGMENT = 256
HIGHEST = lax.Precision.HIGHEST


def _params(*sem):
    return pltpu.CompilerParams(dimension_semantics=sem, vmem_limit_bytes=VMEM_LIMIT_BYTES)


def _const_spec(shape, lead=()):
    idx = tuple(lead) + (0,) * len(shape)
    return pl.BlockSpec((None,) * len(lead) + tuple(shape), lambda *_: idx, pipeline_mode=pl.Buffered(1))


def _layer_norm(z, g, b):
    mu = jnp.mean(z, axis=-1, keepdims=True)
    zc = z - mu
    var = jnp.mean(zc * zc, axis=-1, keepdims=True)
    return zc * lax.rsqrt(var + LN_EPS) * g + b


def _sigmoid(x):
    return 0.5 * jnp.tanh(0.5 * x) + 0.5


def _modulate(x, mod_ref, sub):
    sh = mod_ref[0, 3 * sub:3 * sub + 1, :]
    sc = mod_ref[0, 3 * sub + 1:3 * sub + 2, :]
    return x * (1.0 + sc) + sh


def _gate(mod_ref, sub):
    return mod_ref[0, 3 * sub + 2:3 * sub + 3, :]


def _mod_spec(n_cond, d, tm, seq_len):
    if n_cond == 1:
        return pl.BlockSpec((1, 9, d), lambda i: (0, 0, 0))
    return pl.BlockSpec((1, 9, d), lambda i: ((i * tm) // seq_len, 0, 0))


def _mod_kernel(c_ref, w_ref, b_ref, o_ref):
    s = jax.nn.silu(c_ref[...]).astype(BF16)
    o_ref[0] = jnp.dot(s, w_ref[0].astype(BF16), preferred_element_type=F32) + b_ref[0]


def _modulation(cond8, ada_w, ada_b):
    depth, d, n9 = ada_w.shape
    tn = n9 // 4
    return pl.pallas_call(
        _mod_kernel,
        grid=(depth, n9 // tn),
        in_specs=[pl.BlockSpec((SUBLANES, d), lambda l, n: (0, 0)),
                  pl.BlockSpec((1, d, tn), lambda l, n: (l, 0, n)),
                  pl.BlockSpec((1, 1, tn), lambda l, n: (l, 0, n))],
        out_specs=pl.BlockSpec((1, SUBLANES, tn), lambda l, n: (l, 0, n)),
        out_shape=jax.ShapeDtypeStruct((depth, SUBLANES, n9), F32),
        compiler_params=_params("arbitrary", "arbitrary"),
        name="adaln_modulation",
    )(cond8, ada_w, ada_b.reshape(depth, 1, n9))


def _store_rows_permuted(res, o_ref, perm_ref, *, to_phase_major):
    tm, d = res.shape
    nch = tm // S5_CHUNK
    for v in range(d // LANES):
        lanes = slice(v * LANES, (v + 1) * LANES)
        if to_phase_major:
            perm_ref[v] = res[:, lanes]
            for t in range(S5_CHUNK):
                o_ref[t * nch:(t + 1) * nch, lanes] = perm_ref[v, pl.ds(t, nch, stride=S5_CHUNK), :]
        else:
            for t in range(S5_CHUNK):
                perm_ref[v, pl.ds(t, nch, stride=S5_CHUNK), :] = res[t * nch:(t + 1) * nch, lanes]
            o_ref[:, lanes] = perm_ref[v]


def _ffn_kernel(*refs, sub, alpha, d_ff, ck, has_pos, row_order):
    if has_pos:
        x_ref, pos_ref, mod_ref, wup_ref, wdn_ref, g_ref, b_ref, o_ref, act_ref, perm_ref = refs
    else:
        x_ref, mod_ref, wup_ref, wdn_ref, g_ref, b_ref, o_ref, act_ref, perm_ref = refs
    slabs = perm_ref.shape[0] * TOKEN_TILE // x_ref.shape[0]
    for j in range(x_ref.shape[0] // TOKEN_TILE):
        rows = slice(j * TOKEN_TILE, (j + 1) * TOKEN_TILE)
        x = x_ref[rows, :] + pos_ref[rows, :] if has_pos else x_ref[rows, :]
        h = _modulate(x, mod_ref, sub).astype(BF16)
        for c0 in range(0, d_ff, ck):
            a = jnp.dot(h, wup_ref[:, c0:c0 + ck], preferred_element_type=F32)
            g = jnp.dot(h, wup_ref[:, d_ff + c0:d_ff + c0 + ck], preferred_element_type=F32)
            act_ref[rows, c0:c0 + ck] = (jax.nn.silu(g) * a).astype(BF16)
        f = jnp.dot(act_ref[rows, :], wdn_ref[...], preferred_element_type=F32)
        z = alpha * x + (0.5 * _gate(mod_ref, sub)) * f
        res = _layer_norm(z, g_ref[...], b_ref[...])
        if row_order is None:
            o_ref[rows, :] = res
        else:
            _store_rows_permuted(res, o_ref.at[rows], perm_ref.at[j * slabs:(j + 1) * slabs],
                                 to_phase_major=row_order == "to_phase_major")


def _ffn(x, mod, w_up, w_dn, ln_g, ln_b, *, layer, sub, alpha, seq_len, pos=None, row_order=None):
    n, d = x.shape
    d_ff = w_dn.shape[2]
    which = (layer, sub // 2)
    tm = min(FFN_SUBTILES * TOKEN_TILE, n)
    ck = 256 if d_ff % 256 == 0 else d_ff
    row = pl.BlockSpec((tm, d), lambda i: (i, 0))
    in_specs = [row]
    args = [x]
    if pos is not None:
        per_seq = seq_len // tm
        in_specs.append(pl.BlockSpec((tm, d), lambda i: (i % per_seq, 0)))
        args.append(pos)
    in_specs += [_mod_spec(mod.shape[0], d, tm, seq_len), _const_spec(w_up.shape[2:], which),
                 _const_spec(w_dn.shape[2:], which), _const_spec((1, d)), _const_spec((1, d))]
    args += [mod, w_up, w_dn, ln_g.reshape(1, d), ln_b.reshape(1, d)]
    perm_shape = (tm // TOKEN_TILE * (d // LANES), TOKEN_TILE, LANES) if row_order else (1, SUBLANES, LANES)
    return pl.pallas_call(
        functools.partial(_ffn_kernel, sub=sub, alpha=alpha, d_ff=d_ff, ck=ck, has_pos=pos is not None,
                          row_order=row_order),
        grid=(n // tm,),
        in_specs=in_specs,
        out_specs=row,
        out_shape=jax.ShapeDtypeStruct((n, d), F32),
        scratch_shapes=[pltpu.VMEM((tm, d_ff), BF16), pltpu.VMEM(perm_shape, F32)],
        compiler_params=_params("arbitrary"),
        name="ffn_sublayer",
    )(*args)


def _inproj_kernel(x_ref, mod_ref, w_ref, o_ref, *, sub):
    h = _modulate(x_ref[...], mod_ref, sub).astype(BF16)
    o_ref[...] = jnp.dot(h, w_ref[...], preferred_element_type=F32)


def _inproj(x, mod, w, *, sub, seq_len):
    n, d = x.shape
    dn = w.shape[1]
    tm = min(TOKEN_TILE, n)
    return pl.pallas_call(
        functools.partial(_inproj_kernel, sub=sub),
        grid=(n // tm,),
        in_specs=[pl.BlockSpec((tm, d), lambda i: (i, 0)), _mod_spec(mod.shape[0], d, tm, seq_len),
                  _const_spec(w.shape)],
        out_specs=pl.BlockSpec((tm, dn), lambda i: (i, 0)),
        out_shape=jax.ShapeDtypeStruct((n, dn), F32),
        compiler_params=_params("arbitrary"),
        name="mixer_in_proj",
    )(x, mod, w)


def _s5_discretize(are, aim, ldt):
    dt = jnp.exp(ldt)
    mag = jnp.exp(dt * are)
    abr = mag * jnp.cos(dt * aim)
    abi = mag * jnp.sin(dt * aim)
    den = are * are + aim * aim
    nr, ni = abr - 1.0, abi
    qr = (nr * are + ni * aim) / den
    qi = (ni * are - nr * aim) / den
    return abr, abi, qr, qi


def _s5_ops_kernel(arow_ref, acol_ref, btr_ref, bti_ref, ctr_ref, cti_ref,
                   m_ref, mst_ref, min_ref, a16_ref):
    t_chunk, grp, w = S5_CHUNK, S5_GROUP, S5_CW
    arow, acol = arow_ref[0], acol_ref[0]
    btr, bti = btr_ref[0], bti_ref[0]
    ctr, cti = ctr_ref[0], cti_ref[0]
    colblk = lax.broadcasted_iota(jnp.int32, (w, w), 1) // grp
    colblk_p = lax.broadcasted_iota(jnp.int32, (S5_STATE, w), 1) // grp

    abr2, abi2, qr2, qi2 = _s5_discretize(arow[0:2], arow[2:4], arow[4:6])
    cbr2, cbi2, _, _ = _s5_discretize(acol[:, 0:2], acol[:, 2:4], acol[:, 4:6])
    lag_tabs, decay, carry_in = [], [], []
    for d in range(2):
        abr, abi, qr, qi = abr2[d:d + 1], abi2[d:d + 1], qr2[d:d + 1], qi2[d:d + 1]
        lr = [qr * btr - qi * bti]
        li = [qr * bti + qi * btr]
        er, ei = abr, abi
        for _ in range(t_chunk - 1):
            lr.append(lr[-1] * abr - li[-1] * abi)
            li.append(lr[-2] * abi + li[-1] * abr)
            er, ei = er * abr - ei * abi, er * abi + ei * abr
        decay.append((er, ei))
        if d == 0:
            lr, li = lr[::-1], li[::-1]
        lag_tabs.append((jnp.concatenate(lr, axis=0), jnp.concatenate(li, axis=0)))

        cbr, cbi = cbr2[:, d:d + 1], cbi2[:, d:d + 1]
        pr, pi = cbr, cbi
        er_t = jnp.zeros((S5_STATE, w), F32)
        ei_t = jnp.zeros((S5_STATE, w), F32)
        for n in range(1, t_chunk + 1):
            blk = n - 1 if d == 0 else t_chunk - n
            er_t = jnp.where(colblk_p == blk, pr, er_t)
            ei_t = jnp.where(colblk_p == blk, pi, ei_t)
            pr, pi = pr * cbr - pi * cbi, pr * cbi + pi * cbr
        carry_in.append((ctr * er_t - cti * ei_t, ctr * ei_t + cti * er_t))

    lhs = jnp.concatenate([jnp.concatenate([l_re, -l_im], axis=1) for l_re, l_im in lag_tabs], axis=0)
    tab = jnp.dot(lhs, jnp.concatenate([ctr, cti], axis=0), preferred_element_type=F32, precision=HIGHEST)
    tab_f, tab_b = tab[:w], tab[w:]
    acc = jnp.zeros((w, w), F32)
    for tp in range(t_chunk):
        cut_f = (t_chunk - 1 - tp) * grp
        cut_b = tp * grp
        sh_f = tab_f if cut_f == 0 else jnp.concatenate([tab_f[cut_f:], jnp.zeros((cut_f, w), F32)], 0)
        sh_b = tab_b if cut_b == 0 else jnp.concatenate([jnp.zeros((cut_b, w), F32), tab_b[:w - cut_b]], 0)
        acc = jnp.where(colblk == tp, sh_f + sh_b, acc)
    m_ref[0] = acc.astype(BF16)

    (fr, fi), (br, bi) = lag_tabs
    mst_ref[0] = jnp.concatenate([fr, br, fi, bi], axis=1).astype(BF16)
    (wfr, wfi), (wbr, wbi) = carry_in
    min_ref[0] = jnp.concatenate([wfr, wbr, -wfi, -wbi], axis=0).astype(BF16)
    (efr, efi), (ebr, ebi) = decay
    a16_ref[0] = jnp.concatenate([efr, ebr, efi, ebi], axis=1)


def _s5_operators(a_re, a_im, log_dt, b_re, b_im, c_re, c_im):
    _, g, p = a_re.shape
    ldt = jnp.broadcast_to(log_dt[:, :, None], (2, g, p))
    rows = jnp.concatenate([a_re, a_im, ldt, jnp.zeros((2, g, p), F32)], axis=0)
    arow = rows.transpose(1, 0, 2)
    acol = rows.transpose(1, 2, 0)
    bt_re, bt_im = b_re.transpose(0, 2, 1), b_im.transpose(0, 2, 1)
    ct_re = jnp.tile(c_re.transpose(0, 2, 1), (1, 1, S5_CHUNK))
    ct_im = jnp.tile(c_im.transpose(0, 2, 1), (1, 1, S5_CHUNK))
    per_g = lambda *s: pl.BlockSpec((1,) + s, lambda i: (i, 0, 0))
    w = S5_CW
    return pl.pallas_call(
        _s5_ops_kernel,
        grid=(g,),
        in_specs=[per_g(SUBLANES, p), per_g(p, SUBLANES), per_g(S5_GROUP, p), per_g(S5_GROUP, p),
                  per_g(p, w), per_g(p, w)],
        out_specs=[per_g(w, w), per_g(w, S5_SW), per_g(S5_SW, w), per_g(1, S5_SW)],
        out_shape=[jax.ShapeDtypeStruct((g, w, w), BF16), jax.ShapeDtypeStruct((g, w, S5_SW), BF16),
                   jax.ShapeDtypeStruct((g, S5_SW, w), BF16), jax.ShapeDtypeStruct((g, 1, S5_SW), F32)],
        compiler_params=_params("arbitrary"),
        name="s5_operators",
    )(arow, acol, bt_re, bt_im, ct_re, ct_im)


def _lane_block_transpose(vs):
    blk = lax.broadcasted_iota(jnp.int32, vs[0].shape, 1) // S5_GROUP
    for dist in (4, 2, 1):
        keep = (blk & dist) == 0
        out = list(vs)
        for r in range(len(vs)):
            if r & dist:
                continue
            lo, hi = vs[r], vs[r + dist]
            out[r] = jnp.where(keep, lo, pltpu.roll(hi, dist * S5_GROUP, axis=1))
            out[r + dist] = jnp.where(keep, pltpu.roll(lo, LANES - dist * S5_GROUP, axis=1), hi)
        vs = out
    return vs


def _s5_in_kernel(x_ref, mod_ref, w_ref, o_ref):
    nch = x_ref.shape[0] // S5_CHUNK
    d = w_ref.shape[0]
    h = _modulate(x_ref[...], mod_ref, 1).astype(BF16)
    u = jnp.dot(h, w_ref[...], preferred_element_type=F32)
    per = LANES // S5_GROUP
    for v in range(d // LANES):
        for th in range(S5_CHUNK // per):
            src = [u[(per * th + r) * nch:(per * th + r + 1) * nch, v * LANES:(v + 1) * LANES] for r in range(per)]
            dst = _lane_block_transpose(src)
            for q in range(per):
                o_ref[per * v + q, :, th * LANES:(th + 1) * LANES] = dst[q]


def _s5_in(x, mod, w, *, seq_len):
    n, d = x.shape
    g = d // S5_GROUP
    nch = TOKEN_TILE // S5_CHUNK
    nc = n // S5_CHUNK
    return pl.pallas_call(
        _s5_in_kernel,
        grid=(nc // nch,),
        in_specs=[pl.BlockSpec((TOKEN_TILE, d), lambda i: (i, 0)),
                  _mod_spec(mod.shape[0], d, TOKEN_TILE, seq_len), _const_spec(w.shape)],
        out_specs=pl.BlockSpec((g, nch, S5_CW), lambda i: (0, i, 0)),
        out_shape=jax.ShapeDtypeStruct((g, nc, S5_CW), F32),
        compiler_params=_params("arbitrary"),
        name="s5_in_proj",
    )(x, mod, w)


def _s5_mix_kernel(u_ref, m_ref, mst_ref, min_ref, a16_ref, dsk_ref, s0_ref, y_ref, fin_ref,
                   loc_ref, sa_ref, sb_ref, *, n_seq, ncs):
    gb = u_ref.shape[0]
    half = S5_SW // 2
    pitch = ncs + SUBLANES
    for g in range(gb):
        loc = jnp.dot(u_ref[g].astype(BF16), mst_ref[g], preferred_element_type=F32)
        for s in range(n_seq):
            loc_ref[0, g, s * pitch:s * pitch + ncs, :] = loc[s * ncs:(s + 1) * ncs, :half]
            loc_ref[1, g, s * pitch:s * pitch + ncs, :] = loc[s * ncs:(s + 1) * ncs, half:]
    is_fwd = lax.broadcasted_iota(jnp.int32, (n_seq, half), 1) < S5_STATE
    decay = [(a16_ref[g][:, :half], a16_ref[g][:, half:]) for g in range(gb)]

    def step(k, carry):
        rows_f = pl.ds(k, n_seq, stride=pitch)
        rows_b = pl.ds(ncs - 1 - k, n_seq, stride=pitch)
        out = []
        for g in range(gb):
            sr, si = carry[2 * g], carry[2 * g + 1]
            ar, ai = decay[g]
            sa_ref[0, g, rows_f, :] = sr
            sa_ref[1, g, rows_f, :] = si
            sb_ref[0, g, rows_b, :] = sr
            sb_ref[1, g, rows_b, :] = si
            lr = jnp.where(is_fwd, loc_ref[0, g, rows_f, :], loc_ref[0, g, rows_b, :])
            li = jnp.where(is_fwd, loc_ref[1, g, rows_f, :], loc_ref[1, g, rows_b, :])
            out += [ar * sr - ai * si + lr, ar * si + ai * sr + li]
        return tuple(out)

    init = []
    for g in range(gb):
        init += [s0_ref[g][:, :half], s0_ref[g][:, half:]]
    fin = lax.fori_loop(0, ncs, step, tuple(init))
    sel = lax.broadcasted_iota(jnp.int32, (ncs, half), 1) < S5_STATE
    for g in range(gb):
        fin_ref[g] = jnp.concatenate([fin[2 * g], fin[2 * g + 1]], axis=1)
        parts = []
        for ri in range(2):
            parts.append(jnp.concatenate(
                [jnp.where(sel, sa_ref[ri, g, s * pitch:s * pitch + ncs, :], sb_ref[ri, g, s * pitch:s * pitch + ncs, :])
                 for s in range(n_seq)], axis=0))
        s_in = jnp.concatenate(parts, axis=1).astype(BF16)
        u = u_ref[g]
        y_ref[g] = (jnp.dot(u.astype(BF16), m_ref[g], preferred_element_type=F32)
                    + jnp.dot(s_in, min_ref[g], preferred_element_type=F32)
                    + dsk_ref[g] * u)


def _s5_mix(u_chunks, ops, d_skip, s0, *, n_seq, ncs):
    g, nc, w = u_chunks.shape
    m, mst, m_in, a16 = ops
    gb = S5_GROUP_BLOCK
    dsk = jnp.tile(d_skip.reshape(g, 1, S5_GROUP), (1, 1, S5_CHUNK))
    per_g = lambda *s: pl.BlockSpec((gb,) + s, lambda i: (i, 0, 0))
    rows = n_seq * (ncs + SUBLANES)
    return pl.pallas_call(
        functools.partial(_s5_mix_kernel, n_seq=n_seq, ncs=ncs),
        grid=(g // gb,),
        in_specs=[per_g(nc, w), per_g(w, w), per_g(w, S5_SW), per_g(S5_SW, w), per_g(1, S5_SW), per_g(1, w),
                  per_g(n_seq, S5_SW)],
        out_specs=[per_g(nc, w), per_g(n_seq, S5_SW)],
        out_shape=[jax.ShapeDtypeStruct((g, nc, w), F32), jax.ShapeDtypeStruct((g, n_seq, S5_SW), F32)],
        scratch_shapes=[pltpu.VMEM((2, gb, rows, S5_SW // 2), F32)] * 3,
        compiler_params=_params("arbitrary"),
        name="s5_chunk_scan",
    )(u_chunks, m, mst, m_in, a16, dsk, s0)


def _s5_out_kernel(y_ref, x_ref, mod_ref, w_ref, g_ref, b_ref, o_ref, *, alpha):
    d = w_ref.shape[0]
    per = LANES // S5_GROUP
    phase = [[None] * (d // LANES) for _ in range(S5_CHUNK)]
    for v in range(d // LANES):
        for th in range(S5_CHUNK // per):
            src = [y_ref[per * v + q, :, th * LANES:(th + 1) * LANES] for q in range(per)]
            dst = _lane_block_transpose(src)
            for r in range(per):
                phase[per * th + r][v] = dst[r]
    y = jnp.concatenate([jnp.concatenate(p, axis=1) for p in phase], axis=0)
    z = jnp.dot(jax.nn.gelu(y).astype(BF16), w_ref[...], preferred_element_type=F32)
    out = z[:, :d] * _sigmoid(z[:, d:])
    o_ref[...] = _layer_norm(alpha * x_ref[...] + _gate(mod_ref, 1) * out, g_ref[...], b_ref[...])


def _s5_out(y_chunks, x, mod, w_glu, ln_g, ln_b, *, alpha, seq_len):
    n, d = x.shape
    g = d // S5_GROUP
    nch = TOKEN_TILE // S5_CHUNK
    nc = n // S5_CHUNK
    row = pl.BlockSpec((TOKEN_TILE, d), lambda i: (i, 0))
    return pl.pallas_call(
        functools.partial(_s5_out_kernel, alpha=alpha),
        grid=(nc // nch,),
        in_specs=[pl.BlockSpec((g, nch, S5_CW), lambda i: (0, i, 0)), row,
                  _mod_spec(mod.shape[0], d, TOKEN_TILE, seq_len), _const_spec(w_glu.shape),
                  _const_spec((1, d)), _const_spec((1, d))],
        out_specs=row,
        out_shape=jax.ShapeDtypeStruct((n, d), F32),
        compiler_params=_params("arbitrary"),
        name="s5_out_proj",
    )(y_chunks, x, mod, w_glu, ln_g.reshape(1, d), ln_b.reshape(1, d))


def _s5_mixer(x, mod, w_in, ops, d_skip, w_glu, ln_g, ln_b, init, *, alpha, n_seq, seq_len):
    n, d = x.shape
    g = d // S5_GROUP
    ncs = seq_len // S5_CHUNK
    u_chunks = _s5_in(x, mod, w_in, seq_len=seq_len)
    if init is None:
        s0 = jnp.zeros((g, n_seq, S5_SW), F32)
    else:
        s0 = init.transpose(3, 0, 2, 1, 4).reshape(g, n_seq, S5_SW)
    y_chunks, fin = _s5_mix(u_chunks, ops, d_skip, s0, n_seq=n_seq, ncs=ncs)
    x_new = _s5_out(y_chunks, x, mod, w_glu, ln_g, ln_b, alpha=alpha, seq_len=seq_len)
    state = fin.reshape(g, n_seq, 2, 2, S5_STATE).transpose(1, 3, 2, 0, 4)
    return x_new, state


def _softplus(x):
    return jnp.maximum(x, 0.0) + jnp.log1p(jnp.exp(-jnp.abs(x)))


def _lru_scan_kernel(xr_ref, gate_ref, cw_ref, cb_ref, w4_ref, b4_ref, lam_ref, h0_ref, y_ref, fin_ref,
                     af_ref, bf_ref, ab_ref, bb_ref, end_ref, xpad_ref, *, n_seq, seq_len, seg):
    n = n_seq * seq_len
    n_seg = seq_len // seg
    ns = n_seq * n_seg
    pitch = seg + SUBLANES
    c = LANES
    chained = n_seg > 1
    sp = _softplus(-lam_ref[0])
    cw, cb = cw_ref[0], cb_ref[0]
    w4, b4 = w4_ref[0], b4_ref[0]

    zero_tile = jnp.zeros((SUBLANES, c), F32)
    xpad_ref[0:SUBLANES, :] = zero_tile
    xpad_ref[SUBLANES + n:2 * SUBLANES + n, :] = zero_tile

    def pad_copy(s, _):
        r0 = pl.multiple_of(s * seg, seg)
        xpad_ref[pl.ds(SUBLANES + r0, seg), :] = xr_ref[pl.ds(r0, seg), :]
        return 0

    lax.fori_loop(0, ns, pad_copy, 0)
    row8 = lax.broadcasted_iota(jnp.int32, (SUBLANES, c), 0)

    def gates(s, _):
        r0 = pl.multiple_of(s * seg, seg)
        p0 = pl.multiple_of(s * pitch, SUBLANES)
        seq_start = s % n_seg == 0
        seq_end = s % n_seg == n_seg - 1
        xc = cb
        for k in range(CONV_W):
            off = k - CONV_LEFT
            tap = xpad_ref[pl.ds(SUBLANES + r0 + off, seg), :]
            if off < 0:
                head = jnp.where(seq_start & (row8 < -off), 0.0, tap[:SUBLANES])
                tap = jnp.concatenate([head, tap[SUBLANES:]], axis=0)
            elif off > 0:
                tail = jnp.where(seq_end & (row8 >= SUBLANES - off), 0.0, tap[seg - SUBLANES:])
                tap = jnp.concatenate([tap[:seg - SUBLANES], tail], axis=0)
            xc = xc + tap * cw[k:k + 1]
        pre = jnp.dot(xc.astype(BF16), w4, preferred_element_type=F32) + b4
        for d, (a_ref, b_ref) in enumerate(((af_ref, bf_ref), (ab_ref, bb_ref))):
            r = _sigmoid(pre[:, 2 * d * c:(2 * d + 1) * c])
            gi = _sigmoid(pre[:, (2 * d + 1) * c:(2 * d + 2) * c])
            log_a = -LRU_C * r * sp[d:d + 1]
            a = jnp.exp(log_a)
            a_ref[pl.ds(p0, seg), :] = a
            v = -jnp.tanh(log_a) * (a * a + 1.0)
            root = jnp.where(v > 0.0, v * lax.rsqrt(v), 0.0)
            b_ref[pl.ds(p0, seg), :] = root * (gi * xc)
        return 0

    lax.fori_loop(0, ns, gates, 0)

    def step(k, carry):
        hf, hb, pf, pb = carry
        rows_f = pl.ds(k, ns, stride=pitch)
        rows_b = pl.ds(seg - 1 - k, ns, stride=pitch)
        af, ab = af_ref[rows_f, :], ab_ref[rows_b, :]
        hf = af * hf + bf_ref[rows_f, :]
        hb = ab * hb + bb_ref[rows_b, :]
        bf_ref[rows_f, :] = hf
        bb_ref[rows_b, :] = hb
        if chained:
            pf, pb = pf * af, pb * ab
            af_ref[rows_f, :] = pf
            ab_ref[rows_b, :] = pb
        return hf, hb, pf, pb

    one = jnp.ones((ns, c), F32)
    if chained:
        init = (jnp.zeros((ns, c), F32), jnp.zeros((ns, c), F32), one, one)
    else:
        init = (h0_ref[0], h0_ref[1], one, one)
    hf, hb, pf, pb = lax.fori_loop(0, seg, step, init, unroll=2)

    if chained:
        end_ref[0], end_ref[1], end_ref[2], end_ref[3] = hf, hb, pf, pb
        hin_f, hin_b = h0_ref[0], h0_ref[1]
        for j in range(n_seg):
            jf, jb = pl.ds(j, n_seq, stride=n_seg), pl.ds(n_seg - 1 - j, n_seq, stride=n_seg)
            end_ref[4, jf, :] = hin_f
            end_ref[5, jb, :] = hin_b
            hin_f = end_ref[2, jf, :] * hin_f + end_ref[0, jf, :]
            hin_b = end_ref[3, jb, :] * hin_b + end_ref[1, jb, :]
        fin_ref[0] = hin_f
        fin_ref[1] = hin_b
    else:
        fin_ref[0] = hf
        fin_ref[1] = hb

    def combine(s, _):
        rows = pl.ds(pl.multiple_of(s * seg, seg), seg)
        prow = pl.ds(pl.multiple_of(s * pitch, SUBLANES), seg)
        h = bf_ref[prow, :] + bb_ref[prow, :]
        if chained:
            h = h + af_ref[prow, :] * end_ref[4, pl.ds(s, 1), :] + ab_ref[prow, :] * end_ref[5, pl.ds(s, 1), :]
        y_ref[rows, :] = (h * jax.nn.gelu(gate_ref[rows, :])).astype(BF16)
        return 0

    lax.fori_loop(0, ns, combine, 0)


def _lru_scan(xg, conv_w, conv_b, w4, b4, lam, h0, *, n_seq, seq_len):
    n, d2 = xg.shape
    d = d2 // 2
    nt = d // LANES
    seg = min(LRU_SEGMENT, seq_len)
    ns = n // seg
    stripe = lambda off: pl.BlockSpec((n, LANES), lambda j: (0, j + off))
    per_t = lambda *s: pl.BlockSpec((1,) + s, lambda j: (j, 0, 0))
    return pl.pallas_call(
        functools.partial(_lru_scan_kernel, n_seq=n_seq, seq_len=seq_len, seg=seg),
        grid=(nt,),
        in_specs=[stripe(0), stripe(nt), per_t(CONV_W, LANES), per_t(1, LANES), per_t(LANES, 4 * LANES),
                  per_t(1, 4 * LANES), per_t(2, LANES), pl.BlockSpec((2, n_seq, LANES), lambda j: (0, 0, j))],
        out_specs=[pl.BlockSpec((n, LANES), lambda j: (0, j)),
                   pl.BlockSpec((2, n_seq, LANES), lambda j: (0, 0, j))],
        out_shape=[jax.ShapeDtypeStruct((n, d), BF16), jax.ShapeDtypeStruct((2, n_seq, d), F32)],
        scratch_shapes=([pltpu.VMEM((ns * (seg + SUBLANES), LANES), F32)] * 4
                        + [pltpu.VMEM((6, ns, LANES), F32), pltpu.VMEM((n + 2 * SUBLANES, LANES), F32)]),
        compiler_params=_params("arbitrary"),
        name="rglru_scan",
    )(xg, xg, conv_w, conv_b, w4, b4, lam, h0)


def _lru_out_kernel(y_ref, x_ref, mod_ref, w_ref, g_ref, b_ref, o_ref, *, alpha):
    out = jnp.dot(y_ref[...], w_ref[...], preferred_element_type=F32)
    o_ref[...] = _layer_norm(alpha * x_ref[...] + _gate(mod_ref, 1) * out, g_ref[...], b_ref[...])


def _lru_out(y, x, mod, w_out, ln_g, ln_b, *, alpha, seq_len):
    n, d = x.shape
    tm = min(TOKEN_TILE, n)
    row = pl.BlockSpec((tm, d), lambda i: (i, 0))
    return pl.pallas_call(
        functools.partial(_lru_out_kernel, alpha=alpha),
        grid=(n // tm,),
        in_specs=[row, row, _mod_spec(mod.shape[0], d, tm, seq_len), _const_spec(w_out.shape),
                  _const_spec((1, d)), _const_spec((1, d))],
        out_specs=row,
        out_shape=jax.ShapeDtypeStruct((n, d), F32),
        compiler_params=_params("arbitrary"),
        name="rglru_out_proj",
    )(y, x, mod, w_out, ln_g.reshape(1, d), ln_b.reshape(1, d))


def _lru_stripe_weights(conv_w, conv_b, w_a, b_a, w_x, b_x, lam):
    d = conv_w.shape[1]
    nt = d // LANES
    per = LANES // LRU_BLOCK

    def blockdiag(w):
        wb = w.reshape(nt, per, LRU_BLOCK, LRU_BLOCK)
        eye = jnp.eye(per, dtype=w.dtype)
        return jnp.einsum('tpij,pq->tpiqj', wb, eye).reshape(nt, LANES, LANES)

    w4 = jnp.concatenate([blockdiag(w_a[0]), blockdiag(w_x[0]), blockdiag(w_a[1]), blockdiag(w_x[1])], axis=2)
    b4 = jnp.concatenate([v.reshape(nt, 1, LANES) for v in (b_a[0], b_x[0], b_a[1], b_x[1])], axis=2)
    cw = conv_w.reshape(CONV_W, nt, LANES).transpose(1, 0, 2)
    cb = conv_b.reshape(nt, 1, LANES)
    lam_t = lam.reshape(2, nt, LANES).transpose(1, 0, 2)
    return cw, cb, w4.astype(BF16), b4, lam_t


def _lru_mixer(x, mod, w_in, stripe_w, w_out, ln_g, ln_b, init, *, alpha, n_seq, seq_len):
    n, d = x.shape
    xg = _inproj(x, mod, w_in, sub=1, seq_len=seq_len)
    h0 = jnp.zeros((2, n_seq, d), F32) if init is None else init.transpose(1, 0, 2)
    y, fin = _lru_scan(xg, *stripe_w, h0, n_seq=n_seq, seq_len=seq_len)
    x_new = _lru_out(y, x, mod, w_out, ln_g, ln_b, alpha=alpha, seq_len=seq_len)
    return x_new, fin.transpose(1, 0, 2)


def _grid_pos_embed(rows, d):
    t = jnp.arange(rows * GRID_W)
    row = (t // GRID_W).astype(F32)
    col = (t % GRID_W).astype(F32)
    quarter = d // 4
    omega = 1.0 / (10000.0 ** (jnp.arange(quarter, dtype=F32) / quarter))

    def emb(p):
        ang = p[:, None] * omega[None, :]
        return jnp.concatenate([jnp.sin(ang), jnp.cos(ang)], -1)

    return jnp.concatenate([emb(row), emb(col)], -1)


def _trunk(x3, mod, init_s5, init_lru, wts, pos):
    n_seq, seq_len, d = x3.shape
    depth = mod.shape[0]
    alpha = (2.0 * depth) ** 0.25
    x = x3.reshape(n_seq * seq_len, d)
    st_s5, st_lru = [], []
    for i in range(depth):
        j = i // 2
        m = mod[i]
        x = _ffn(x, m, wts["up"], wts["dn"], wts["ln_g"][i, 0], wts["ln_b"][i, 0],
                 layer=i, sub=0, alpha=alpha, seq_len=seq_len, pos=pos if i == 0 else None,
                 row_order="to_phase_major" if i % 2 == 0 else None)
        if i % 2 == 0:
            init = None if init_s5 is None else init_s5[:, j]
            x, st = _s5_mixer(x, m, wts["s5_w_in"][j], wts["s5_ops"][j], wts["s5_d"][j], wts["s5_w_glu"][j],
                              wts["ln_g"][i, 1], wts["ln_b"][i, 1], init,
                              alpha=alpha, n_seq=n_seq, seq_len=seq_len)
            st_s5.append(st)
        else:
            init = None if init_lru is None else init_lru[:, j]
            x, st = _lru_mixer(x, m, wts["lru_w_in"][j], wts["lru_stripe"][j], wts["lru_w_out"][j],
                               wts["ln_g"][i, 1], wts["ln_b"][i, 1], init,
                               alpha=alpha, n_seq=n_seq, seq_len=seq_len)
            st_lru.append(st)
        x = _ffn(x, m, wts["up"], wts["dn"], wts["ln_g"][i, 2], wts["ln_b"][i, 2],
                 layer=i, sub=2, alpha=alpha, seq_len=seq_len, row_order="to_natural" if i % 2 == 0 else None)
    return x.reshape(n_seq, seq_len, d), st_s5, st_lru


def kernel(x_prompt, x_sample, c, state_s5, state_lru, c_ctx, ada_w, ada_b, ln_g, ln_b, ffn_w_up, ffn_w_down, s5_w_in, s5_a_re, s5_a_im, s5_log_dt, s5_b_re, s5_b_im, s5_c_re, s5_c_im, s5_d, s5_w_glu, lru_w_in, lru_conv_w, lru_conv_b, lru_w_a, lru_b_a, lru_w_x, lru_b_x, lru_lambda, lru_w_out):
    depth, d, _ = ada_w.shape
    n_dec = c.shape[0]
    assert 1 + n_dec <= SUBLANES
    cond8 = jnp.concatenate([c_ctx[None, :], c, jnp.zeros((SUBLANES - 1 - n_dec, d), F32)], axis=0)
    mod = _modulation(cond8, ada_w, ada_b).reshape(depth, SUBLANES, 9, d)
    mod_ctx, mod_dec = mod[:, 0:1], mod[:, 1:1 + n_dec]

    wts = {
        "up": ffn_w_up.astype(BF16), "dn": ffn_w_down.astype(BF16),
        "ln_g": ln_g, "ln_b": ln_b,
        "s5_w_in": s5_w_in.astype(BF16), "s5_d": s5_d, "s5_w_glu": s5_w_glu.astype(BF16),
        "s5_ops": [_s5_operators(s5_a_re[j], s5_a_im[j], s5_log_dt[j], s5_b_re[j], s5_b_im[j],
                                 s5_c_re[j], s5_c_im[j]) for j in range(s5_w_in.shape[0])],
        "lru_w_in": lru_w_in.astype(BF16), "lru_w_out": lru_w_out.astype(BF16),
        "lru_stripe": [_lru_stripe_weights(lru_conv_w[j], lru_conv_b[j], lru_w_a[j], lru_b_a[j], lru_w_x[j],
                                           lru_b_x[j], lru_lambda[j]) for j in range(lru_w_in.shape[0])],
    }

    y_prompt, st_s5, st_lru = _trunk(x_prompt, mod_ctx, None, None, wts, None)
    new_state_s5 = jnp.stack(st_s5, 1).astype(x_prompt.dtype)
    new_state_lru = jnp.stack(st_lru, 1).astype(x_prompt.dtype)

    pos = _grid_pos_embed(x_sample.shape[1] // GRID_W, d).astype(x_sample.dtype)
    y_sample, _, _ = _trunk(x_sample, mod_dec, state_s5, state_lru, wts, pos)
    return (y_prompt, y_sample, new_state_s5, new_state_lru)
```

```python
import functools

import jax
import jax.numpy as jnp
from jax import lax
from jax.experimental import pallas as pl
from jax.experimental.pallas import tpu as pltpu

F32 = jnp.float32
BF16 = jnp.bfloat16

GRID_W = 64
S5_GROUP = 16
S5_STATE = 64
LRU_BLOCK = 64
LRU_C = 8.0
CONV_W = 4
CONV_LEFT = 2
LN_EPS = 1e-5
LOG2_E = 1.4426950408889634

LANES = 128
SUBLANES = 8
VMEM_LIMIT_BYTES = 56 * 1024 * 1024

S5_CHUNK = 16
S5_CW = S5_CHUNK * S5_GROUP
S5_SW = 4 * S5_STATE
S5_GROUP_BLOCK = 8
TOKEN_TILE = 512
FFN_SUBTILES = 2
S5_PROJ_SUBTILES = 2
LRU_SEGMENT = 256


def _params(*sem):
    return pltpu.CompilerParams(dimension_semantics=sem, vmem_limit_bytes=VMEM_LIMIT_BYTES)


def _const_spec(shape, lead=()):
    idx = tuple(lead) + (0,) * len(shape)
    return pl.BlockSpec((None,) * len(lead) + tuple(shape), lambda *_: idx, pipeline_mode=pl.Buffered(1))


def _layer_norm(z, g, b):
    mu = jnp.mean(z, axis=-1, keepdims=True)
    zc = z - mu
    var = jnp.mean(zc * zc, axis=-1, keepdims=True)
    return zc * lax.rsqrt(var + LN_EPS) * g + b


def _sigmoid(x):
    return 0.5 * jnp.tanh(0.5 * x) + 0.5


def _gelu(x):
    c = 0.7978845608028654
    h = 0.5 * x
    return h + h * jnp.tanh(x * ((c * 0.044715) * (x * x) + c))


def _modulate(x, mod_ref, sub):
    sh = mod_ref[0, 3 * sub:3 * sub + 1, :]
    sc = mod_ref[0, 3 * sub + 1:3 * sub + 2, :]
    return x * (1.0 + sc) + sh


def _gate(mod_ref, sub):
    return mod_ref[0, 3 * sub + 2:3 * sub + 3, :]


def _mod_spec(n_cond, d, tm, seq_len):
    if n_cond == 1:
        return pl.BlockSpec((1, 9, d), lambda i: (0, 0, 0))
    return pl.BlockSpec((1, 9, d), lambda i: ((i * tm) // seq_len, 0, 0))


def _mod_kernel(c_ref, w_ref, b_ref, o_ref):
    s = jax.nn.silu(c_ref[...]).astype(BF16)
    o_ref[0] = jnp.dot(s, w_ref[0].astype(BF16), preferred_element_type=F32) + b_ref[0]


def _modulation(cond8, ada_w, ada_b):
    depth, d, n9 = ada_w.shape
    tn = n9 // 4
    return pl.pallas_call(
        _mod_kernel,
        grid=(depth, n9 // tn),
        in_specs=[pl.BlockSpec((SUBLANES, d), lambda l, n: (0, 0)),
                  pl.BlockSpec((1, d, tn), lambda l, n: (l, 0, n)),
                  pl.BlockSpec((1, 1, tn), lambda l, n: (l, 0, n))],
        out_specs=pl.BlockSpec((1, SUBLANES, tn), lambda l, n: (l, 0, n)),
        out_shape=jax.ShapeDtypeStruct((depth, SUBLANES, n9), F32),
        compiler_params=_params("arbitrary", "arbitrary"),
        name="adaln_modulation",
    )(cond8, ada_w, ada_b.reshape(depth, 1, n9))


def _store_rows_permuted(res, o_ref, perm_ref, *, to_phase_major):
    tm, d = res.shape
    nch = tm // S5_CHUNK
    for v in range(d // LANES):
        lanes = slice(v * LANES, (v + 1) * LANES)
        if to_phase_major:
            perm_ref[v] = res[:, lanes]
            for t in range(S5_CHUNK):
                o_ref[t * nch:(t + 1) * nch, lanes] = perm_ref[v, pl.ds(t, nch, stride=S5_CHUNK), :]
        else:
            for t in range(S5_CHUNK):
                perm_ref[v, pl.ds(t, nch, stride=S5_CHUNK), :] = res[t * nch:(t + 1) * nch, lanes]
            o_ref[:, lanes] = perm_ref[v]


def _ffn_kernel(*refs, sub, alpha, d_ff, ck, has_pos, row_order):
    if has_pos:
        x_ref, pos_ref, mod_ref, wup_ref, wdn_ref, g_ref, b_ref, o_ref, act_ref, perm_ref = refs
    else:
        x_ref, mod_ref, wup_ref, wdn_ref, g_ref, b_ref, o_ref, act_ref, perm_ref = refs
    slabs = perm_ref.shape[0] * TOKEN_TILE // x_ref.shape[0]
    for j in range(x_ref.shape[0] // TOKEN_TILE):
        rows = slice(j * TOKEN_TILE, (j + 1) * TOKEN_TILE)
        x = x_ref[rows, :] + pos_ref[rows, :] if has_pos else x_ref[rows, :]
        h = _modulate(x, mod_ref, sub).astype(BF16)
        for c0 in range(0, d_ff, ck):
            a = jnp.dot(h, wup_ref[:, c0:c0 + ck], preferred_element_type=F32)
            g = jnp.dot(h, wup_ref[:, d_ff + c0:d_ff + c0 + ck], preferred_element_type=F32)
            act_ref[rows, c0:c0 + ck] = (jax.nn.silu(g) * a).astype(BF16)
        f = jnp.dot(act_ref[rows, :], wdn_ref[...], preferred_element_type=F32)
        z = alpha * x + (0.5 * _gate(mod_ref, sub)) * f
        res = _layer_norm(z, g_ref[...], b_ref[...])
        if row_order is None:
            o_ref[rows, :] = res
        else:
            _store_rows_permuted(res, o_ref.at[rows], perm_ref.at[j * slabs:(j + 1) * slabs],
                                 to_phase_major=row_order == "to_phase_major")


def _ffn(x, mod, w_up, w_dn, ln_g, ln_b, *, layer, sub, alpha, seq_len, pos=None, row_order=None):
    n, d = x.shape
    d_ff = w_dn.shape[2]
    which = (layer, sub // 2)
    tm = min(FFN_SUBTILES * TOKEN_TILE, n)
    ck = 256 if d_ff % 256 == 0 else d_ff
    row = pl.BlockSpec((tm, d), lambda i: (i, 0))
    in_specs = [row]
    args = [x]
    if pos is not None:
        per_seq = seq_len // tm
        in_specs.append(pl.BlockSpec((tm, d), lambda i: (i % per_seq, 0)))
        args.append(pos)
    in_specs += [_mod_spec(mod.shape[0], d, tm, seq_len), _const_spec(w_up.shape[2:], which),
                 _const_spec(w_dn.shape[2:], which), _const_spec((1, d)), _const_spec((1, d))]
    args += [mod, w_up, w_dn, ln_g.reshape(1, d), ln_b.reshape(1, d)]
    perm_shape = (tm // TOKEN_TILE * (d // LANES), TOKEN_TILE, LANES) if row_order else (1, SUBLANES, LANES)
    return pl.pallas_call(
        functools.partial(_ffn_kernel, sub=sub, alpha=alpha, d_ff=d_ff, ck=ck, has_pos=pos is not None,
                          row_order=row_order),
        grid=(n // tm,),
        in_specs=in_specs,
        out_specs=row,
        out_shape=jax.ShapeDtypeStruct((n, d), F32),
        scratch_shapes=[pltpu.VMEM((tm, d_ff), BF16), pltpu.VMEM(perm_shape, F32)],
        compiler_params=_params("arbitrary"),
        name="ffn_sublayer",
    )(*args)


def _inproj_kernel(x_ref, mod_ref, w_ref, o_ref, *, sub):
    h = _modulate(x_ref[...], mod_ref, sub).astype(BF16)
    o_ref[...] = jnp.dot(h, w_ref[...], preferred_element_type=F32)


def _inproj(x, mod, w, *, sub, seq_len):
    n, d = x.shape
    dn = w.shape[1]
    tm = min(TOKEN_TILE, n)
    return pl.pallas_call(
        functools.partial(_inproj_kernel, sub=sub),
        grid=(n // tm,),
        in_specs=[pl.BlockSpec((tm, d), lambda i: (i, 0)), _mod_spec(mod.shape[0], d, tm, seq_len),
                  _const_spec(w.shape)],
        out_specs=pl.BlockSpec((tm, dn), lambda i: (i, 0)),
        out_shape=jax.ShapeDtypeStruct((n, dn), F32),
        compiler_params=_params("arbitrary"),
        name="mixer_in_proj",
    )(x, mod, w)


def _s5_discretize(are, aim, ldt):
    dt = jnp.exp(ldt)
    mag = jnp.exp(dt * are)
    abr = mag * jnp.cos(dt * aim)
    abi = mag * jnp.sin(dt * aim)
    den = are * are + aim * aim
    nr, ni = abr - 1.0, abi
    qr = (nr * are + ni * aim) / den
    qi = (ni * are - nr * aim) / den
    return abr, abi, qr, qi


def _split_bf16(x):
    hi = x.astype(BF16)
    return hi, (x - hi.astype(F32)).astype(BF16)


def _s5_ops_kernel(arow_ref, btr_ref, bti_ref, ctr_ref, cti_ref, m_ref, mst_ref, min_ref, a16_ref):
    t_chunk, grp, w = S5_CHUNK, S5_GROUP, S5_CW
    arow = arow_ref[0]
    btr, bti = btr_ref[0], bti_ref[0]
    ctr, cti = ctr_ref[0], cti_ref[0]
    colblk = lax.broadcasted_iota(jnp.int32, (w, w), 1) // grp
    pow_row = lax.broadcasted_iota(jnp.int32, (t_chunk, w), 0)
    pow_blk = lax.broadcasted_iota(jnp.int32, (t_chunk, w), 1) // grp
    tn_dims = (((0,), (0,)), ((), ()))

    abr2, abi2, qr2, qi2 = _s5_discretize(arow[0:2], arow[2:4], arow[4:6])
    lag_tabs, decay, carry_in = [], [], []
    for d in range(2):
        abr, abi, qr, qi = abr2[d:d + 1], abi2[d:d + 1], qr2[d:d + 1], qi2[d:d + 1]
        lr = [qr * btr - qi * bti]
        li = [qr * bti + qi * btr]
        pr, pi = [abr], [abi]
        for _ in range(t_chunk - 1):
            lr.append(lr[-1] * abr - li[-1] * abi)
            li.append(lr[-2] * abi + li[-1] * abr)
            pr.append(pr[-1] * abr - pi[-1] * abi)
            pi.append(pr[-2] * abi + pi[-1] * abr)
        decay.append((pr[-1], pi[-1]))
        if d == 0:
            lr, li = lr[::-1], li[::-1]
        lag_tabs.append((jnp.concatenate(lr, axis=0), jnp.concatenate(li, axis=0)))

        place = (pow_blk == (pow_row if d == 0 else t_chunk - 1 - pow_row)).astype(BF16)
        e_t = []
        for tbl in (jnp.concatenate(pr, axis=0), jnp.concatenate(pi, axis=0)):
            hi, lo = _split_bf16(tbl)
            e_t.append(lax.dot_general(hi, place, tn_dims, preferred_element_type=F32)
                       + lax.dot_general(lo, place, tn_dims, preferred_element_type=F32))
        er_t, ei_t = e_t
        carry_in.append((ctr * er_t - cti * ei_t, ctr * ei_t + cti * er_t))

    lhs = jnp.concatenate([jnp.concatenate([l_re, -l_im], axis=1) for l_re, l_im in lag_tabs], axis=0)
    l_hi, l_lo = _split_bf16(lhs)
    c_hi, c_lo = _split_bf16(jnp.concatenate([ctr, cti], axis=0))
    tab = (jnp.dot(l_hi, c_hi, preferred_element_type=F32) + jnp.dot(l_lo, c_hi, preferred_element_type=F32)
           + jnp.dot(l_hi, c_lo, preferred_element_type=F32))
    tab_f, tab_b = tab[:w], tab[w:]
    acc = jnp.zeros((w, w), F32)
    for tp in range(t_chunk):
        cut_f = (t_chunk - 1 - tp) * grp
        cut_b = tp * grp
        sh_f = tab_f if cut_f == 0 else jnp.concatenate([tab_f[cut_f:], jnp.zeros((cut_f, w), F32)], 0)
        sh_b = tab_b if cut_b == 0 else jnp.concatenate([jnp.zeros((cut_b, w), F32), tab_b[:w - cut_b]], 0)
        acc = jnp.where(colblk == tp, sh_f + sh_b, acc)
    m_ref[0] = acc.astype(BF16)

    (fr, fi), (br, bi) = lag_tabs
    mst_ref[0] = jnp.concatenate([fr, br, fi, bi], axis=1).astype(BF16)
    (wfr, wfi), (wbr, wbi) = carry_in
    min_ref[0] = jnp.concatenate([wfr, wbr, -wfi, -wbi], axis=0).astype(BF16)
    (efr, efi), (ebr, ebi) = decay
    a16_ref[0] = jnp.concatenate([efr, ebr, efi, ebi], axis=1)


def _s5_operators(a_re, a_im, log_dt, b_re, b_im, c_re, c_im):
    _, g, p = a_re.shape
    ldt = jnp.broadcast_to(log_dt[:, :, None], (2, g, p))
    arow = jnp.concatenate([a_re, a_im, ldt, jnp.zeros((2, g, p), F32)], axis=0).transpose(1, 0, 2)
    bt_re, bt_im = b_re.transpose(0, 2, 1), b_im.transpose(0, 2, 1)
    ct_re = jnp.tile(c_re.transpose(0, 2, 1), (1, 1, S5_CHUNK))
    ct_im = jnp.tile(c_im.transpose(0, 2, 1), (1, 1, S5_CHUNK))
    per_g = lambda *s: pl.BlockSpec((1,) + s, lambda i: (i, 0, 0))
    w = S5_CW
    return pl.pallas_call(
        _s5_ops_kernel,
        grid=(g,),
        in_specs=[per_g(SUBLANES, p), per_g(S5_GROUP, p), per_g(S5_GROUP, p), per_g(p, w), per_g(p, w)],
        out_specs=[per_g(w, w), per_g(w, S5_SW), per_g(S5_SW, w), per_g(1, S5_SW)],
        out_shape=[jax.ShapeDtypeStruct((g, w, w), BF16), jax.ShapeDtypeStruct((g, w, S5_SW), BF16),
                   jax.ShapeDtypeStruct((g, S5_SW, w), BF16), jax.ShapeDtypeStruct((g, 1, S5_SW), F32)],
        compiler_params=_params("arbitrary"),
        name="s5_operators",
    )(arow, bt_re, bt_im, ct_re, ct_im)


def _lane_block_transpose(vs):
    blk = lax.broadcasted_iota(jnp.int32, vs[0].shape, 1) // S5_GROUP
    for dist in (4, 2, 1):
        keep = (blk & dist) == 0
        out = list(vs)
        for r in range(len(vs)):
            if r & dist:
                continue
            lo, hi = vs[r], vs[r + dist]
            out[r] = jnp.where(keep, lo, pltpu.roll(hi, dist * S5_GROUP, axis=1))
            out[r + dist] = jnp.where(keep, pltpu.roll(lo, LANES - dist * S5_GROUP, axis=1), hi)
        vs = out
    return vs


def _s5_in_kernel(x_ref, mod_ref, w_ref, o_ref):
    nch = TOKEN_TILE // S5_CHUNK
    d = w_ref.shape[0]
    per = LANES // S5_GROUP
    for j in range(x_ref.shape[0] // TOKEN_TILE):
        h = _modulate(x_ref[j * TOKEN_TILE:(j + 1) * TOKEN_TILE, :], mod_ref, 1).astype(BF16)
        u = jnp.dot(h, w_ref[...], preferred_element_type=F32)
        for v in range(d // LANES):
            for th in range(S5_CHUNK // per):
                src = [u[(per * th + r) * nch:(per * th + r + 1) * nch, v * LANES:(v + 1) * LANES]
                       for r in range(per)]
                dst = _lane_block_transpose(src)
                for q in range(per):
                    o_ref[per * v + q, j * nch:(j + 1) * nch, th * LANES:(th + 1) * LANES] = dst[q]


def _s5_in(x, mod, w, *, seq_len):
    n, d = x.shape
    g = d // S5_GROUP
    tm = TOKEN_TILE
    nch = tm // S5_CHUNK
    nc = n // S5_CHUNK
    return pl.pallas_call(
        _s5_in_kernel,
        grid=(n // tm,),
        in_specs=[pl.BlockSpec((tm, d), lambda i: (i, 0)),
                  _mod_spec(mod.shape[0], d, tm, seq_len), _const_spec(w.shape)],
        out_specs=pl.BlockSpec((g, nch, S5_CW), lambda i: (0, i, 0)),
        out_shape=jax.ShapeDtypeStruct((g, nc, S5_CW), F32),
        compiler_params=_params("arbitrary"),
        name="s5_in_proj",
    )(x, mod, w)


def _s5_mix_kernel(u_ref, m_ref, mst_ref, min_ref, a16_ref, dsk_ref, s0_ref, y_ref, fin_ref,
                   loc_ref, sa_ref, sb_ref, *, n_seq, ncs):
    gb = u_ref.shape[0]
    half = S5_SW // 2
    pitch = ncs + SUBLANES
    for g in range(gb):
        loc = jnp.dot(u_ref[g].astype(BF16), mst_ref[g], preferred_element_type=F32)
        for s in range(n_seq):
            loc_ref[0, g, s * pitch:s * pitch + ncs, :] = loc[s * ncs:(s + 1) * ncs, :half]
            loc_ref[1, g, s * pitch:s * pitch + ncs, :] = loc[s * ncs:(s + 1) * ncs, half:]
    is_fwd = lax.broadcasted_iota(jnp.int32, (n_seq, half), 1) < S5_STATE
    decay = [(a16_ref[g][:, :half], a16_ref[g][:, half:]) for g in range(gb)]

    def step(k, carry):
        rows_f = pl.ds(k, n_seq, stride=pitch)
        rows_b = pl.ds(ncs - 1 - k, n_seq, stride=pitch)
        out = []
        for g in range(gb):
            sr, si = carry[2 * g], carry[2 * g + 1]
            ar, ai = decay[g]
            sa_ref[0, g, rows_f, :] = sr
            sa_ref[1, g, rows_f, :] = si
            sb_ref[0, g, rows_b, :] = sr
            sb_ref[1, g, rows_b, :] = si
            lr = jnp.where(is_fwd, loc_ref[0, g, rows_f, :], loc_ref[0, g, rows_b, :])
            li = jnp.where(is_fwd, loc_ref[1, g, rows_f, :], loc_ref[1, g, rows_b, :])
            out += [ar * sr - ai * si + lr, ar * si + ai * sr + li]
        return tuple(out)

    init = []
    for g in range(gb):
        init += [s0_ref[g][:, :half], s0_ref[g][:, half:]]
    fin = lax.fori_loop(0, ncs, step, tuple(init))
    sel = lax.broadcasted_iota(jnp.int32, (ncs, half), 1) < S5_STATE
    for g in range(gb):
        fin_ref[g] = jnp.concatenate([fin[2 * g], fin[2 * g + 1]], axis=1)
        parts = []
        for ri in range(2):
            parts.append(jnp.concatenate(
                [jnp.where(sel, sa_ref[ri, g, s * pitch:s * pitch + ncs, :], sb_ref[ri, g, s * pitch:s * pitch + ncs, :])
                 for s in range(n_seq)], axis=0))
        s_in = jnp.concatenate(parts, axis=1).astype(BF16)
        u = u_ref[g]
        y_ref[g] = (jnp.dot(u.astype(BF16), m_ref[g], preferred_element_type=F32)
                    + jnp.dot(s_in, min_ref[g], preferred_element_type=F32)
                    + dsk_ref[g] * u)


def _s5_mix(u_chunks, ops, d_skip, s0, *, n_seq, ncs):
    g, nc, w = u_chunks.shape
    m, mst, m_in, a16 = ops
    gb = S5_GROUP_BLOCK
    dsk = jnp.tile(d_skip.reshape(g, 1, S5_GROUP), (1, 1, S5_CHUNK))
    per_g = lambda *s: pl.BlockSpec((gb,) + s, lambda i: (i, 0, 0))
    rows = n_seq * (ncs + SUBLANES)
    return pl.pallas_call(
        functools.partial(_s5_mix_kernel, n_seq=n_seq, ncs=ncs),
        grid=(g // gb,),
        in_specs=[per_g(nc, w), per_g(w, w), per_g(w, S5_SW), per_g(S5_SW, w), per_g(1, S5_SW), per_g(1, w),
                  per_g(n_seq, S5_SW)],
        out_specs=[per_g(nc, w), per_g(n_seq, S5_SW)],
        out_shape=[jax.ShapeDtypeStruct((g, nc, w), F32), jax.ShapeDtypeStruct((g, n_seq, S5_SW), F32)],
        scratch_shapes=[pltpu.VMEM((2, gb, rows, S5_SW // 2), F32)] * 3,
        compiler_params=_params("arbitrary"),
        name="s5_chunk_scan",
    )(u_chunks, m, mst, m_in, a16, dsk, s0)


def _s5_out_kernel(y_ref, x_ref, mod_ref, w_ref, g_ref, b_ref, o_ref, *, alpha):
    nch = TOKEN_TILE // S5_CHUNK
    d = w_ref.shape[0]
    per = LANES // S5_GROUP
    for j in range(x_ref.shape[0] // TOKEN_TILE):
        rows = slice(j * TOKEN_TILE, (j + 1) * TOKEN_TILE)
        phase = [[None] * (d // LANES) for _ in range(S5_CHUNK)]
        for v in range(d // LANES):
            for th in range(S5_CHUNK // per):
                src = [y_ref[per * v + q, j * nch:(j + 1) * nch, th * LANES:(th + 1) * LANES] for q in range(per)]
                dst = _lane_block_transpose(src)
                for r in range(per):
                    phase[per * th + r][v] = dst[r]
        y = jnp.concatenate([jnp.concatenate(p, axis=1) for p in phase], axis=0)
        z = jnp.dot(_gelu(y).astype(BF16), w_ref[...], preferred_element_type=F32)
        out = z[:, :d] * _sigmoid(z[:, d:])
        o_ref[rows, :] = _layer_norm(alpha * x_ref[rows, :] + _gate(mod_ref, 1) * out, g_ref[...], b_ref[...])


def _s5_out(y_chunks, x, mod, w_glu, ln_g, ln_b, *, alpha, seq_len):
    n, d = x.shape
    g = d // S5_GROUP
    tm = S5_PROJ_SUBTILES * TOKEN_TILE
    nch = tm // S5_CHUNK
    row = pl.BlockSpec((tm, d), lambda i: (i, 0))
    return pl.pallas_call(
        functools.partial(_s5_out_kernel, alpha=alpha),
        grid=(n // tm,),
        in_specs=[pl.BlockSpec((g, nch, S5_CW), lambda i: (0, i, 0)), row,
                  _mod_spec(mod.shape[0], d, tm, seq_len), _const_spec(w_glu.shape),
                  _const_spec((1, d)), _const_spec((1, d))],
        out_specs=row,
        out_shape=jax.ShapeDtypeStruct((n, d), F32),
        compiler_params=_params("arbitrary"),
        name="s5_out_proj",
    )(y_chunks, x, mod, w_glu, ln_g.reshape(1, d), ln_b.reshape(1, d))


def _s5_mixer(x, mod, w_in, ops, d_skip, w_glu, ln_g, ln_b, init, *, alpha, n_seq, seq_len):
    n, d = x.shape
    g = d // S5_GROUP
    ncs = seq_len // S5_CHUNK
    u_chunks = _s5_in(x, mod, w_in, seq_len=seq_len)
    if init is None:
        s0 = jnp.zeros((g, n_seq, S5_SW), F32)
    else:
        s0 = init.transpose(3, 0, 2, 1, 4).reshape(g, n_seq, S5_SW)
    y_chunks, fin = _s5_mix(u_chunks, ops, d_skip, s0, n_seq=n_seq, ncs=ncs)
    x_new = _s5_out(y_chunks, x, mod, w_glu, ln_g, ln_b, alpha=alpha, seq_len=seq_len)
    state = fin.reshape(g, n_seq, 2, 2, S5_STATE).transpose(1, 3, 2, 0, 4)
    return x_new, state


def _softplus(x):
    return jnp.maximum(x, 0.0) + jnp.log1p(jnp.exp(-jnp.abs(x)))


def _lru_scan_kernel(xr_ref, gate_ref, cw_ref, cb_ref, w4_ref, b4_ref, lam_ref, h0_ref, y_ref, fin_ref,
                     af_ref, bf_ref, ab_ref, bb_ref, end_ref, xpad_ref, *, n_seq, seq_len, seg):
    n = n_seq * seq_len
    n_seg = seq_len // seg
    ns = n_seq * n_seg
    pitch = seg + SUBLANES
    c = LANES
    chained = n_seg > 1
    half_rate = (0.5 * LRU_C) * _softplus(-lam_ref[0])
    cw, cb = cw_ref[0], cb_ref[0]
    w4_half = (0.5 * w4_ref[0].astype(F32)).astype(BF16)
    b4_half = 0.5 * b4_ref[0]

    zero_tile = jnp.zeros((SUBLANES, c), F32)
    xpad_ref[0:SUBLANES, :] = zero_tile
    xpad_ref[SUBLANES + n:2 * SUBLANES + n, :] = zero_tile

    def pad_copy(s, _):
        r0 = pl.multiple_of(s * seg, seg)
        xpad_ref[pl.ds(SUBLANES + r0, seg), :] = xr_ref[pl.ds(r0, seg), :]
        return 0

    lax.fori_loop(0, ns, pad_copy, 0)
    row8 = lax.broadcasted_iota(jnp.int32, (SUBLANES, c), 0)

    def gates(s, _):
        r0 = pl.multiple_of(s * seg, seg)
        p0 = pl.multiple_of(s * pitch, SUBLANES)
        seq_start = s % n_seg == 0
        seq_end = s % n_seg == n_seg - 1
        xc = cb
        for k in range(CONV_W):
            off = k - CONV_LEFT
            tap = xpad_ref[pl.ds(SUBLANES + r0 + off, seg), :]
            if off < 0:
                head = jnp.where(seq_start & (row8 < -off), 0.0, tap[:SUBLANES])
                tap = jnp.concatenate([head, tap[SUBLANES:]], axis=0)
            elif off > 0:
                tail = jnp.where(seq_end & (row8 >= SUBLANES - off), 0.0, tap[seg - SUBLANES:])
                tap = jnp.concatenate([tap[:seg - SUBLANES], tail], axis=0)
            xc = xc + tap * cw[k:k + 1]
        half_pre = jnp.dot(xc.astype(BF16), w4_half, preferred_element_type=F32) + b4_half
        xc_half = 0.5 * xc
        for d, (a_ref, b_ref) in enumerate(((af_ref, bf_ref), (ab_ref, bb_ref))):
            tanh_a = jnp.tanh(half_pre[:, 2 * d * c:(2 * d + 1) * c])
            tanh_x = jnp.tanh(half_pre[:, (2 * d + 1) * c:(2 * d + 2) * c])
            neg_log_a = half_rate[d:d + 1] * tanh_a + half_rate[d:d + 1]
            a = jnp.exp2(neg_log_a * -LOG2_E)
            a_ref[pl.ds(p0, seg), :] = a
            v = jnp.tanh(neg_log_a) * (a * a + 1.0)
            root = jnp.where(v > 0.0, v * lax.rsqrt(v), 0.0)
            b_ref[pl.ds(p0, seg), :] = root * (xc_half * tanh_x + xc_half)
        return 0

    lax.fori_loop(0, ns, gates, 0)

    def step(k, carry):
        hf, hb, pf, pb = carry
        rows_f = pl.ds(k, ns, stride=pitch)
        rows_b = pl.ds(seg - 1 - k, ns, stride=pitch)
        af, ab = af_ref[rows_f, :], ab_ref[rows_b, :]
        hf = af * hf + bf_ref[rows_f, :]
        hb = ab * hb + bb_ref[rows_b, :]
        bf_ref[rows_f, :] = hf
        bb_ref[rows_b, :] = hb
        if chained:
            pf, pb = pf * af, pb * ab
            af_ref[rows_f, :] = pf
            ab_ref[rows_b, :] = pb
        return hf, hb, pf, pb

    one = jnp.ones((ns, c), F32)
    if chained:
        init = (jnp.zeros((ns, c), F32), jnp.zeros((ns, c), F32), one, one)
    else:
        init = (h0_ref[0], h0_ref[1], one, one)
    hf, hb, pf, pb = lax.fori_loop(0, seg, step, init, unroll=2)

    if chained:
        end_ref[0], end_ref[1], end_ref[2], end_ref[3] = hf, hb, pf, pb
        hin_f, hin_b = h0_ref[0], h0_ref[1]
        for j in range(n_seg):
            jf, jb = pl.ds(j, n_seq, stride=n_seg), pl.ds(n_seg - 1 - j, n_seq, stride=n_seg)
            end_ref[4, jf, :] = hin_f
            end_ref[5, jb, :] = hin_b
            hin_f = end_ref[2, jf, :] * hin_f + end_ref[0, jf, :]
            hin_b = end_ref[3, jb, :] * hin_b + end_ref[1, jb, :]
        fin_ref[0] = hin_f
        fin_ref[1] = hin_b
    else:
        fin_ref[0] = hf
        fin_ref[1] = hb

    def combine(s, _):
        rows = pl.ds(pl.multiple_of(s * seg, seg), seg)
        prow = pl.ds(pl.multiple_of(s * pitch, SUBLANES), seg)
        h = bf_ref[prow, :] + bb_ref[prow, :]
        if chained:
            h = h + af_ref[prow, :] * end_ref[4, pl.ds(s, 1), :] + ab_ref[prow, :] * end_ref[5, pl.ds(s, 1), :]
        y_ref[rows, :] = (h * _gelu(gate_ref[rows, :])).astype(BF16)
        return 0

    lax.fori_loop(0, ns, combine, 0)


def _lru_scan(xg, conv_w, conv_b, w4, b4, lam, h0, *, n_seq, seq_len):
    n, d2 = xg.shape
    d = d2 // 2
    nt = d // LANES
    seg = min(LRU_SEGMENT, seq_len)
    ns = n // seg
    stripe = lambda off: pl.BlockSpec((n, LANES), lambda j: (0, j + off))
    per_t = lambda *s: pl.BlockSpec((1,) + s, lambda j: (j, 0, 0))
    return pl.pallas_call(
        functools.partial(_lru_scan_kernel, n_seq=n_seq, seq_len=seq_len, seg=seg),
        grid=(nt,),
        in_specs=[stripe(0), stripe(nt), per_t(CONV_W, LANES), per_t(1, LANES), per_t(LANES, 4 * LANES),
                  per_t(1, 4 * LANES), per_t(2, LANES), pl.BlockSpec((2, n_seq, LANES), lambda j: (0, 0, j))],
        out_specs=[pl.BlockSpec((n, LANES), lambda j: (0, j)),
                   pl.BlockSpec((2, n_seq, LANES), lambda j: (0, 0, j))],
        out_shape=[jax.ShapeDtypeStruct((n, d), BF16), jax.ShapeDtypeStruct((2, n_seq, d), F32)],
        scratch_shapes=([pltpu.VMEM((ns * (seg + SUBLANES), LANES), F32)] * 4
                        + [pltpu.VMEM((6, ns, LANES), F32), pltpu.VMEM((n + 2 * SUBLANES, LANES), F32)]),
        compiler_params=_params("arbitrary"),
        name="rglru_scan",
    )(xg, xg, conv_w, conv_b, w4, b4, lam, h0)


def _lru_out_kernel(y_ref, x_ref, mod_ref, w_ref, g_ref, b_ref, o_ref, *, alpha):
    out = jnp.dot(y_ref[...], w_ref[...], preferred_element_type=F32)
    o_ref[...] = _layer_norm(alpha * x_ref[...] + _gate(mod_ref, 1) * out, g_ref[...], b_ref[...])


def _lru_out(y, x, mod, w_out, ln_g, ln_b, *, alpha, seq_len):
    n, d = x.shape
    tm = min(TOKEN_TILE, n)
    row = pl.BlockSpec((tm, d), lambda i: (i, 0))
    return pl.pallas_call(
        functools.partial(_lru_out_kernel, alpha=alpha),
        grid=(n // tm,),
        in_specs=[row, row, _mod_spec(mod.shape[0], d, tm, seq_len), _const_spec(w_out.shape),
                  _const_spec((1, d)), _const_spec((1, d))],
        out_specs=row,
        out_shape=jax.ShapeDtypeStruct((n, d), F32),
        compiler_params=_params("arbitrary"),
        name="rglru_out_proj",
    )(y, x, mod, w_out, ln_g.reshape(1, d), ln_b.reshape(1, d))


def _lru_stripe_weights(conv_w, conv_b, w_a, b_a, w_x, b_x, lam):
    d = conv_w.shape[1]
    nt = d // LANES
    per = LANES // LRU_BLOCK

    def blockdiag(w):
        wb = w.reshape(nt, per, LRU_BLOCK, LRU_BLOCK)
        eye = jnp.eye(per, dtype=w.dtype)
        return jnp.einsum('tpij,pq->tpiqj', wb, eye).reshape(nt, LANES, LANES)

    w4 = jnp.concatenate([blockdiag(w_a[0]), blockdiag(w_x[0]), blockdiag(w_a[1]), blockdiag(w_x[1])], axis=2)
    b4 = jnp.concatenate([v.reshape(nt, 1, LANES) for v in (b_a[0], b_x[0], b_a[1], b_x[1])], axis=2)
    cw = conv_w.reshape(CONV_W, nt, LANES).transpose(1, 0, 2)
    cb = conv_b.reshape(nt, 1, LANES)
    lam_t = lam.reshape(2, nt, LANES).transpose(1, 0, 2)
    return cw, cb, w4.astype(BF16), b4, lam_t


def _lru_mixer(x, mod, w_in, stripe_w, w_out, ln_g, ln_b, init, *, alpha, n_seq, seq_len):
    n, d = x.shape
    xg = _inproj(x, mod, w_in, sub=1, seq_len=seq_len)
    h0 = jnp.zeros((2, n_seq, d), F32) if init is None else init.transpose(1, 0, 2)
    y, fin = _lru_scan(xg, *stripe_w, h0, n_seq=n_seq, seq_len=seq_len)
    x_new = _lru_out(y, x, mod, w_out, ln_g, ln_b, alpha=alpha, seq_len=seq_len)
    return x_new, fin.transpose(1, 0, 2)


def _grid_pos_embed(rows, d):
    t = jnp.arange(rows * GRID_W)
    row = (t // GRID_W).astype(F32)
    col = (t % GRID_W).astype(F32)
    quarter = d // 4
    omega = 1.0 / (10000.0 ** (jnp.arange(quarter, dtype=F32) / quarter))

    def emb(p):
        ang = p[:, None] * omega[None, :]
        return jnp.concatenate([jnp.sin(ang), jnp.cos(ang)], -1)

    return jnp.concatenate([emb(row), emb(col)], -1)


def _trunk(x3, mod, init_s5, init_lru, wts, pos):
    n_seq, seq_len, d = x3.shape
    depth = mod.shape[0]
    alpha = (2.0 * depth) ** 0.25
    x = x3.reshape(n_seq * seq_len, d)
    st_s5, st_lru = [], []
    for i in range(depth):
        j = i // 2
        m = mod[i]
        x = _ffn(x, m, wts["up"], wts["dn"], wts["ln_g"][i, 0], wts["ln_b"][i, 0],
                 layer=i, sub=0, alpha=alpha, seq_len=seq_len, pos=pos if i == 0 else None,
                 row_order="to_phase_major" if i % 2 == 0 else None)
        if i % 2 == 0:
            init = None if init_s5 is None else init_s5[:, j]
            x, st = _s5_mixer(x, m, wts["s5_w_in"][j], wts["s5_ops"][j], wts["s5_d"][j], wts["s5_w_glu"][j],
                              wts["ln_g"][i, 1], wts["ln_b"][i, 1], init,
                              alpha=alpha, n_seq=n_seq, seq_len=seq_len)
            st_s5.append(st)
        else:
            init = None if init_lru is None else init_lru[:, j]
            x, st = _lru_mixer(x, m, wts["lru_w_in"][j], wts["lru_stripe"][j], wts["lru_w_out"][j],
                               wts["ln_g"][i, 1], wts["ln_b"][i, 1], init,
                               alpha=alpha, n_seq=n_seq, seq_len=seq_len)
            st_lru.append(st)
        x = _ffn(x, m, wts["up"], wts["dn"], wts["ln_g"][i, 2], wts["ln_b"][i, 2],
                 layer=i, sub=2, alpha=alpha, seq_len=seq_len, row_order="to_natural" if i % 2 == 0 else None)
    return x.reshape(n_seq, seq_len, d), st_s5, st_lru


def kernel(x_prompt, x_sample, c, state_s5, state_lru, c_ctx, ada_w, ada_b, ln_g, ln_b, ffn_w_up, ffn_w_down, s5_w_in, s5_a_re, s5_a_im, s5_log_dt, s5_b_re, s5_b_im, s5_c_re, s5_c_im, s5_d, s5_w_glu, lru_w_in, lru_conv_w, lru_conv_b, lru_w_a, lru_b_a, lru_w_x, lru_b_x, lru_lambda, lru_w_out):
    depth, d, _ = ada_w.shape
    n_dec = c.shape[0]
    assert 1 + n_dec <= SUBLANES
    cond8 = jnp.concatenate([c_ctx[None, :], c, jnp.zeros((SUBLANES - 1 - n_dec, d), F32)], axis=0)
    mod = _modulation(cond8, ada_w, ada_b).reshape(depth, SUBLANES, 9, d)
    mod_ctx, mod_dec = mod[:, 0:1], mod[:, 1:1 + n_dec]

    wts = {
        "up": ffn_w_up.astype(BF16), "dn": ffn_w_down.astype(BF16),
        "ln_g": ln_g, "ln_b": ln_b,
        "s5_w_in": s5_w_in.astype(BF16), "s5_d": s5_d, "s5_w_glu": s5_w_glu.astype(BF16),
        "s5_ops": [_s5_operators(s5_a_re[j], s5_a_im[j], s5_log_dt[j], s5_b_re[j], s5_b_im[j],
                                 s5_c_re[j], s5_c_im[j]) for j in range(s5_w_in.shape[0])],
        "lru_w_in": lru_w_in.astype(BF16), "lru_w_out": lru_w_out.astype(BF16),
        "lru_stripe": [_lru_stripe_weights(lru_conv_w[j], lru_conv_b[j], lru_w_a[j], lru_b_a[j], lru_w_x[j],
                                           lru_b_x[j], lru_lambda[j]) for j in range(lru_w_in.shape[0])],
    }

    y_prompt, st_s5, st_lru = _trunk(x_prompt, mod_ctx, None, None, wts, None)
    new_state_s5 = jnp.stack(st_s5, 1).astype(x_prompt.dtype)
    new_state_lru = jnp.stack(st_lru, 1).astype(x_prompt.dtype)

    pos = _grid_pos_embed(x_sample.shape[1] // GRID_W, d).astype(x_sample.dtype)
    y_sample, _, _ = _trunk(x_sample, mod_dec, state_s5, state_lru, wts, pos)
    return (y_prompt, y_sample, new_state_s5, new_state_lru)
```

```python
import functools

import jax
import jax.numpy as jnp
from jax import lax
from jax.experimental import pallas as pl
from jax.experimental.pallas import tpu as pltpu

F32 = jnp.float32
BF16 = jnp.bfloat16

GRID_W = 64
S5_GROUP = 16
S5_STATE = 64
LRU_BLOCK = 64
LRU_C = 8.0
CONV_W = 4
CONV_LEFT = 2
LN_EPS = 1e-5
LOG2_E = 1.4426950408889634

LANES = 128
SUBLANES = 8
VMEM_LIMIT_BYTES = 56 * 1024 * 1024

S5_CHUNK = 16
S5_CW = S5_CHUNK * S5_GROUP
S5_SW = 4 * S5_STATE
S5_GROUP_BLOCK = 8
PERM_PITCH = S5_CHUNK + 4
TOKEN_TILE = 512
FFN_SUBTILES = 2
S5_PROJ_SUBTILES = 2
LRU_SEGMENT = 256


def _params(*sem):
    return pltpu.CompilerParams(dimension_semantics=sem, vmem_limit_bytes=VMEM_LIMIT_BYTES)


def _const_spec(shape, lead=()):
    idx = tuple(lead) + (0,) * len(shape)
    return pl.BlockSpec((None,) * len(lead) + tuple(shape), lambda *_: idx, pipeline_mode=pl.Buffered(1))


def _layer_norm(z, g, b):
    mu = jnp.mean(z, axis=-1, keepdims=True)
    zc = z - mu
    var = jnp.mean(zc * zc, axis=-1, keepdims=True)
    return zc * lax.rsqrt(var + LN_EPS) * g + b


def _sigmoid(x):
    return 0.5 * jnp.tanh(0.5 * x) + 0.5


def _gelu(x):
    c = 0.7978845608028654
    h = 0.5 * x
    return h + h * jnp.tanh(x * ((c * 0.044715) * (x * x) + c))


def _modulate(x, mod_ref, sub):
    sh = mod_ref[0, 3 * sub:3 * sub + 1, :]
    sc = mod_ref[0, 3 * sub + 1:3 * sub + 2, :]
    return x * (1.0 + sc) + sh


def _gate(mod_ref, sub):
    return mod_ref[0, 3 * sub + 2:3 * sub + 3, :]


def _mod_spec(n_cond, d, tm, seq_len):
    if n_cond == 1:
        return pl.BlockSpec((1, 9, d), lambda i: (0, 0, 0))
    return pl.BlockSpec((1, 9, d), lambda i: ((i * tm) // seq_len, 0, 0))


def _mod_kernel(c_ref, w_ref, b_ref, o_ref):
    s = jax.nn.silu(c_ref[...]).astype(BF16)
    o_ref[0] = jnp.dot(s, w_ref[0].astype(BF16), preferred_element_type=F32) + b_ref[0]


def _modulation(cond8, ada_w, ada_b):
    depth, d, n9 = ada_w.shape
    tn = n9 // 4
    return pl.pallas_call(
        _mod_kernel,
        grid=(depth, n9 // tn),
        in_specs=[pl.BlockSpec((SUBLANES, d), lambda l, n: (0, 0)),
                  pl.BlockSpec((1, d, tn), lambda l, n: (l, 0, n)),
                  pl.BlockSpec((1, 1, tn), lambda l, n: (l, 0, n))],
        out_specs=pl.BlockSpec((1, SUBLANES, tn), lambda l, n: (l, 0, n)),
        out_shape=jax.ShapeDtypeStruct((depth, SUBLANES, n9), F32),
        compiler_params=_params("arbitrary", "arbitrary"),
        name="adaln_modulation",
    )(cond8, ada_w, ada_b.reshape(depth, 1, n9))


def _store_rows_permuted(res, o_ref, perm_ref, *, to_phase_major):
    tm, d = res.shape
    nch = tm // S5_CHUNK
    for v in range(d // LANES):
        lanes = slice(v * LANES, (v + 1) * LANES)
        if to_phase_major:
            for c in range(nch):
                perm_ref[v, c * PERM_PITCH:c * PERM_PITCH + S5_CHUNK, :] = res[c * S5_CHUNK:(c + 1) * S5_CHUNK, lanes]
            for t in range(S5_CHUNK):
                o_ref[t * nch:(t + 1) * nch, lanes] = perm_ref[v, pl.ds(t, nch, stride=PERM_PITCH), :]
        else:
            for t in range(S5_CHUNK):
                perm_ref[v, pl.ds(t, nch, stride=PERM_PITCH), :] = res[t * nch:(t + 1) * nch, lanes]
            for c in range(nch):
                o_ref[c * S5_CHUNK:(c + 1) * S5_CHUNK, lanes] = perm_ref[v, c * PERM_PITCH:c * PERM_PITCH + S5_CHUNK, :]


def _ffn_kernel(*refs, sub, alpha, d_ff, ck, has_pos, row_order):
    if has_pos:
        x_ref, pos_ref, mod_ref, wup_ref, wdn_ref, g_ref, b_ref, o_ref, act_ref, perm_ref = refs
    else:
        x_ref, mod_ref, wup_ref, wdn_ref, g_ref, b_ref, o_ref, act_ref, perm_ref = refs
    slabs = perm_ref.shape[0] * TOKEN_TILE // x_ref.shape[0]
    for j in range(x_ref.shape[0] // TOKEN_TILE):
        rows = slice(j * TOKEN_TILE, (j + 1) * TOKEN_TILE)
        x = x_ref[rows, :] + pos_ref[rows, :] if has_pos else x_ref[rows, :]
        h = _modulate(x, mod_ref, sub).astype(BF16)
        for c0 in range(0, d_ff, ck):
            a = jnp.dot(h, wup_ref[:, c0:c0 + ck], preferred_element_type=F32)
            g = jnp.dot(h, wup_ref[:, d_ff + c0:d_ff + c0 + ck], preferred_element_type=F32)
            act_ref[rows, c0:c0 + ck] = (jax.nn.silu(g) * a).astype(BF16)
        f = jnp.dot(act_ref[rows, :], wdn_ref[...], preferred_element_type=F32)
        z = alpha * x + (0.5 * _gate(mod_ref, sub)) * f
        res = _layer_norm(z, g_ref[...], b_ref[...])
        if row_order is None:
            o_ref[rows, :] = res
        else:
            _store_rows_permuted(res, o_ref.at[rows], perm_ref.at[j * slabs:(j + 1) * slabs],
                                 to_phase_major=row_order == "to_phase_major")


def _ffn(x, mod, w_up, w_dn, ln_g, ln_b, *, layer, sub, alpha, seq_len, pos=None, row_order=None):
    n, d = x.shape
    d_ff = w_dn.shape[2]
    which = (layer, sub // 2)
    tm = min(FFN_SUBTILES * TOKEN_TILE, n)
    ck = 256 if d_ff % 256 == 0 else d_ff
    row = pl.BlockSpec((tm, d), lambda i: (i, 0))
    in_specs = [row]
    args = [x]
    if pos is not None:
        per_seq = seq_len // tm
        in_specs.append(pl.BlockSpec((tm, d), lambda i: (i % per_seq, 0)))
        args.append(pos)
    in_specs += [_mod_spec(mod.shape[0], d, tm, seq_len), _const_spec(w_up.shape[2:], which),
                 _const_spec(w_dn.shape[2:], which), _const_spec((1, d)), _const_spec((1, d))]
    args += [mod, w_up, w_dn, ln_g.reshape(1, d), ln_b.reshape(1, d)]
    perm_rows = TOKEN_TILE // S5_CHUNK * PERM_PITCH
    perm_shape = (tm // TOKEN_TILE * (d // LANES), perm_rows, LANES) if row_order else (1, SUBLANES, LANES)
    return pl.pallas_call(
        functools.partial(_ffn_kernel, sub=sub, alpha=alpha, d_ff=d_ff, ck=ck, has_pos=pos is not None,
                          row_order=row_order),
        grid=(n // tm,),
        in_specs=in_specs,
        out_specs=row,
        out_shape=jax.ShapeDtypeStruct((n, d), F32),
        scratch_shapes=[pltpu.VMEM((tm, d_ff), BF16), pltpu.VMEM(perm_shape, F32)],
        compiler_params=_params("arbitrary"),
        name="ffn_sublayer",
    )(*args)


def _inproj_kernel(x_ref, mod_ref, w_ref, o_ref, *, sub):
    h = _modulate(x_ref[...], mod_ref, sub).astype(BF16)
    o_ref[...] = jnp.dot(h, w_ref[...], preferred_element_type=F32)


def _inproj(x, mod, w, *, sub, seq_len):
    n, d = x.shape
    dn = w.shape[1]
    tm = min(TOKEN_TILE, n)
    return pl.pallas_call(
        functools.partial(_inproj_kernel, sub=sub),
        grid=(n // tm,),
        in_specs=[pl.BlockSpec((tm, d), lambda i: (i, 0)), _mod_spec(mod.shape[0], d, tm, seq_len),
                  _const_spec(w.shape)],
        out_specs=pl.BlockSpec((tm, dn), lambda i: (i, 0)),
        out_shape=jax.ShapeDtypeStruct((n, dn), F32),
        compiler_params=_params("arbitrary"),
        name="mixer_in_proj",
    )(x, mod, w)


def _s5_discretize(are, aim, ldt):
    dt = jnp.exp(ldt)
    mag = jnp.exp(dt * are)
    abr = mag * jnp.cos(dt * aim)
    abi = mag * jnp.sin(dt * aim)
    den = are * are + aim * aim
    nr, ni = abr - 1.0, abi
    qr = (nr * are + ni * aim) / den
    qi = (ni * are - nr * aim) / den
    return abr, abi, qr, qi


def _split_bf16(x):
    hi = x.astype(BF16)
    return hi, (x - hi.astype(F32)).astype(BF16)


def _s5_ops_kernel(arow_ref, btr_ref, bti_ref, ctr_ref, cti_ref, m_ref, mst_ref, min_ref, a16_ref):
    t_chunk, grp, w = S5_CHUNK, S5_GROUP, S5_CW
    arow = arow_ref[0]
    btr, bti = btr_ref[0], bti_ref[0]
    ctr, cti = ctr_ref[0], cti_ref[0]
    colblk = lax.broadcasted_iota(jnp.int32, (w, w), 1) // grp
    pow_row = lax.broadcasted_iota(jnp.int32, (t_chunk, w), 0)
    pow_blk = lax.broadcasted_iota(jnp.int32, (t_chunk, w), 1) // grp
    tn_dims = (((0,), (0,)), ((), ()))

    abr2, abi2, qr2, qi2 = _s5_discretize(arow[0:2], arow[2:4], arow[4:6])
    lag_tabs, decay, carry_in = [], [], []
    for d in range(2):
        abr, abi, qr, qi = abr2[d:d + 1], abi2[d:d + 1], qr2[d:d + 1], qi2[d:d + 1]
        lr = [qr * btr - qi * bti]
        li = [qr * bti + qi * btr]
        pr, pi = [abr], [abi]
        for _ in range(t_chunk - 1):
            lr.append(lr[-1] * abr - li[-1] * abi)
            li.append(lr[-2] * abi + li[-1] * abr)
            pr.append(pr[-1] * abr - pi[-1] * abi)
            pi.append(pr[-2] * abi + pi[-1] * abr)
        decay.append((pr[-1], pi[-1]))
        if d == 0:
            lr, li = lr[::-1], li[::-1]
        lag_tabs.append((jnp.concatenate(lr, axis=0), jnp.concatenate(li, axis=0)))

        place = (pow_blk == (pow_row if d == 0 else t_chunk - 1 - pow_row)).astype(BF16)
        e_t = []
        for tbl in (jnp.concatenate(pr, axis=0), jnp.concatenate(pi, axis=0)):
            hi, lo = _split_bf16(tbl)
            e_t.append(lax.dot_general(hi, place, tn_dims, preferred_element_type=F32)
                       + lax.dot_general(lo, place, tn_dims, preferred_element_type=F32))
        er_t, ei_t = e_t
        carry_in.append((ctr * er_t - cti * ei_t, ctr * ei_t + cti * er_t))

    lhs = jnp.concatenate([jnp.concatenate([l_re, -l_im], axis=1) for l_re, l_im in lag_tabs], axis=0)
    l_hi, l_lo = _split_bf16(lhs)
    c_hi, c_lo = _split_bf16(jnp.concatenate([ctr, cti], axis=0))
    tab = (jnp.dot(l_hi, c_hi, preferred_element_type=F32) + jnp.dot(l_lo, c_hi, preferred_element_type=F32)
           + jnp.dot(l_hi, c_lo, preferred_element_type=F32))
    tab_f, tab_b = tab[:w], tab[w:]
    acc = jnp.zeros((w, w), F32)
    for tp in range(t_chunk):
        cut_f = (t_chunk - 1 - tp) * grp
        cut_b = tp * grp
        sh_f = tab_f if cut_f == 0 else jnp.concatenate([tab_f[cut_f:], jnp.zeros((cut_f, w), F32)], 0)
        sh_b = tab_b if cut_b == 0 else jnp.concatenate([jnp.zeros((cut_b, w), F32), tab_b[:w - cut_b]], 0)
        acc = jnp.where(colblk == tp, sh_f + sh_b, acc)
    m_ref[0] = acc.astype(BF16)

    (fr, fi), (br, bi) = lag_tabs
    mst_ref[0] = jnp.concatenate([fr, br, fi, bi], axis=1).astype(BF16)
    (wfr, wfi), (wbr, wbi) = carry_in
    min_ref[0] = jnp.concatenate([wfr, wbr, -wfi, -wbi], axis=0).astype(BF16)
    (efr, efi), (ebr, ebi) = decay
    a16_ref[0] = jnp.concatenate([efr, ebr, efi, ebi], axis=1)


def _s5_operators(a_re, a_im, log_dt, b_re, b_im, c_re, c_im):
    _, g, p = a_re.shape
    ldt = jnp.broadcast_to(log_dt[:, :, None], (2, g, p))
    arow = jnp.concatenate([a_re, a_im, ldt, jnp.zeros((2, g, p), F32)], axis=0).transpose(1, 0, 2)
    bt_re, bt_im = b_re.transpose(0, 2, 1), b_im.transpose(0, 2, 1)
    ct_re = jnp.tile(c_re.transpose(0, 2, 1), (1, 1, S5_CHUNK))
    ct_im = jnp.tile(c_im.transpose(0, 2, 1), (1, 1, S5_CHUNK))
    per_g = lambda *s: pl.BlockSpec((1,) + s, lambda i: (i, 0, 0))
    w = S5_CW
    return pl.pallas_call(
        _s5_ops_kernel,
        grid=(g,),
        in_specs=[per_g(SUBLANES, p), per_g(S5_GROUP, p), per_g(S5_GROUP, p), per_g(p, w), per_g(p, w)],
        out_specs=[per_g(w, w), per_g(w, S5_SW), per_g(S5_SW, w), per_g(1, S5_SW)],
        out_shape=[jax.ShapeDtypeStruct((g, w, w), BF16), jax.ShapeDtypeStruct((g, w, S5_SW), BF16),
                   jax.ShapeDtypeStruct((g, S5_SW, w), BF16), jax.ShapeDtypeStruct((g, 1, S5_SW), F32)],
        compiler_params=_params("arbitrary"),
        name="s5_operators",
    )(arow, bt_re, bt_im, ct_re, ct_im)


def _lane_block_transpose(vs):
    blk = lax.broadcasted_iota(jnp.int32, vs[0].shape, 1) // S5_GROUP
    for dist in (4, 2, 1):
        keep = (blk & dist) == 0
        out = list(vs)
        for r in range(len(vs)):
            if r & dist:
                continue
            lo, hi = vs[r], vs[r + dist]
            out[r] = jnp.where(keep, lo, pltpu.roll(hi, dist * S5_GROUP, axis=1))
            out[r + dist] = jnp.where(keep, pltpu.roll(lo, LANES - dist * S5_GROUP, axis=1), hi)
        vs = out
    return vs


def _s5_in_kernel(x_ref, mod_ref, w_ref, o_ref):
    nch = TOKEN_TILE // S5_CHUNK
    d = w_ref.shape[0]
    per = LANES // S5_GROUP
    for j in range(x_ref.shape[0] // TOKEN_TILE):
        h = _modulate(x_ref[j * TOKEN_TILE:(j + 1) * TOKEN_TILE, :], mod_ref, 1).astype(BF16)
        u = jnp.dot(h, w_ref[...], preferred_element_type=F32)
        for v in range(d // LANES):
            for th in range(S5_CHUNK // per):
                src = [u[(per * th + r) * nch:(per * th + r + 1) * nch, v * LANES:(v + 1) * LANES]
                       for r in range(per)]
                dst = _lane_block_transpose(src)
                for q in range(per):
                    o_ref[per * v + q, j * nch:(j + 1) * nch, th * LANES:(th + 1) * LANES] = dst[q]


def _s5_in(x, mod, w, *, seq_len):
    n, d = x.shape
    g = d // S5_GROUP
    tm = TOKEN_TILE
    nch = tm // S5_CHUNK
    nc = n // S5_CHUNK
    return pl.pallas_call(
        _s5_in_kernel,
        grid=(n // tm,),
        in_specs=[pl.BlockSpec((tm, d), lambda i: (i, 0)),
                  _mod_spec(mod.shape[0], d, tm, seq_len), _const_spec(w.shape)],
        out_specs=pl.BlockSpec((g, nch, S5_CW), lambda i: (0, i, 0)),
        out_shape=jax.ShapeDtypeStruct((g, nc, S5_CW), F32),
        compiler_params=_params("arbitrary"),
        name="s5_in_proj",
    )(x, mod, w)


def _s5_mix_kernel(u_ref, m_ref, mst_ref, min_ref, a16_ref, dsk_ref, s0_ref, y_ref, fin_ref,
                   loc_ref, sa_ref, sb_ref, *, n_seq, ncs):
    gb = u_ref.shape[0]
    half = S5_SW // 2
    pitch = ncs + SUBLANES
    for g in range(gb):
        loc = jnp.dot(u_ref[g].astype(BF16), mst_ref[g], preferred_element_type=F32)
        for s in range(n_seq):
            loc_ref[0, g, s * pitch:s * pitch + ncs, :] = loc[s * ncs:(s + 1) * ncs, :half]
            loc_ref[1, g, s * pitch:s * pitch + ncs, :] = loc[s * ncs:(s + 1) * ncs, half:]
    is_fwd = lax.broadcasted_iota(jnp.int32, (n_seq, half), 1) < S5_STATE
    decay = [(a16_ref[g][:, :half], a16_ref[g][:, half:]) for g in range(gb)]

    def step(k, carry):
        rows_f = pl.ds(k, n_seq, stride=pitch)
        rows_b = pl.ds(ncs - 1 - k, n_seq, stride=pitch)
        out = []
        for g in range(gb):
            sr, si = carry[2 * g], carry[2 * g + 1]
            ar, ai = decay[g]
            sa_ref[0, g, rows_f, :] = sr
            sa_ref[1, g, rows_f, :] = si
            sb_ref[0, g, rows_b, :] = sr
            sb_ref[1, g, rows_b, :] = si
            lr = jnp.where(is_fwd, loc_ref[0, g, rows_f, :], loc_ref[0, g, rows_b, :])
            li = jnp.where(is_fwd, loc_ref[1, g, rows_f, :], loc_ref[1, g, rows_b, :])
            out += [ar * sr - ai * si + lr, ar * si + ai * sr + li]
        return tuple(out)

    init = []
    for g in range(gb):
        init += [s0_ref[g][:, :half], s0_ref[g][:, half:]]
    fin = lax.fori_loop(0, ncs, step, tuple(init))
    sel = lax.broadcasted_iota(jnp.int32, (ncs, half), 1) < S5_STATE
    for g in range(gb):
        fin_ref[g] = jnp.concatenate([fin[2 * g], fin[2 * g + 1]], axis=1)
        parts = []
        for ri in range(2):
            parts.append(jnp.concatenate(
                [jnp.where(sel, sa_ref[ri, g, s * pitch:s * pitch + ncs, :], sb_ref[ri, g, s * pitch:s * pitch + ncs, :])
                 for s in range(n_seq)], axis=0))
        s_in = jnp.concatenate(parts, axis=1).astype(BF16)
        u = u_ref[g]
        y_ref[g] = (jnp.dot(u.astype(BF16), m_ref[g], preferred_element_type=F32)
                    + jnp.dot(s_in, min_ref[g], preferred_element_type=F32)
                    + dsk_ref[g] * u)


def _s5_mix(u_chunks, ops, d_skip, s0, *, n_seq, ncs):
    g, nc, w = u_chunks.shape
    m, mst, m_in, a16 = ops
    gb = S5_GROUP_BLOCK
    dsk = jnp.tile(d_skip.reshape(g, 1, S5_GROUP), (1, 1, S5_CHUNK))
    per_g = lambda *s: pl.BlockSpec((gb,) + s, lambda i: (i, 0, 0))
    rows = n_seq * (ncs + SUBLANES)
    return pl.pallas_call(
        functools.partial(_s5_mix_kernel, n_seq=n_seq, ncs=ncs),
        grid=(g // gb,),
        in_specs=[per_g(nc, w), per_g(w, w), per_g(w, S5_SW), per_g(S5_SW, w), per_g(1, S5_SW), per_g(1, w),
                  per_g(n_seq, S5_SW)],
        out_specs=[per_g(nc, w), per_g(n_seq, S5_SW)],
        out_shape=[jax.ShapeDtypeStruct((g, nc, w), F32), jax.ShapeDtypeStruct((g, n_seq, S5_SW), F32)],
        scratch_shapes=[pltpu.VMEM((2, gb, rows, S5_SW // 2), F32)] * 3,
        compiler_params=_params("arbitrary"),
        name="s5_chunk_scan",
    )(u_chunks, m, mst, m_in, a16, dsk, s0)


def _s5_out_kernel(y_ref, x_ref, mod_ref, w_ref, g_ref, b_ref, o_ref, *, alpha):
    nch = TOKEN_TILE // S5_CHUNK
    d = w_ref.shape[0]
    per = LANES // S5_GROUP
    for j in range(x_ref.shape[0] // TOKEN_TILE):
        rows = slice(j * TOKEN_TILE, (j + 1) * TOKEN_TILE)
        phase = [[None] * (d // LANES) for _ in range(S5_CHUNK)]
        for v in range(d // LANES):
            for th in range(S5_CHUNK // per):
                src = [y_ref[per * v + q, j * nch:(j + 1) * nch, th * LANES:(th + 1) * LANES] for q in range(per)]
                dst = _lane_block_transpose(src)
                for r in range(per):
                    phase[per * th + r][v] = dst[r]
        y = jnp.concatenate([jnp.concatenate(p, axis=1) for p in phase], axis=0)
        z = jnp.dot(_gelu(y).astype(BF16), w_ref[...], preferred_element_type=F32)
        out = z[:, :d] * _sigmoid(z[:, d:])
        o_ref[rows, :] = _layer_norm(alpha * x_ref[rows, :] + _gate(mod_ref, 1) * out, g_ref[...], b_ref[...])


def _s5_out(y_chunks, x, mod, w_glu, ln_g, ln_b, *, alpha, seq_len):
    n, d = x.shape
    g = d // S5_GROUP
    tm = S5_PROJ_SUBTILES * TOKEN_TILE
    nch = tm // S5_CHUNK
    row = pl.BlockSpec((tm, d), lambda i: (i, 0))
    return pl.pallas_call(
        functools.partial(_s5_out_kernel, alpha=alpha),
        grid=(n // tm,),
        in_specs=[pl.BlockSpec((g, nch, S5_CW), lambda i: (0, i, 0)), row,
                  _mod_spec(mod.shape[0], d, tm, seq_len), _const_spec(w_glu.shape),
                  _const_spec((1, d)), _const_spec((1, d))],
        out_specs=row,
        out_shape=jax.ShapeDtypeStruct((n, d), F32),
        compiler_params=_params("arbitrary"),
        name="s5_out_proj",
    )(y_chunks, x, mod, w_glu, ln_g.reshape(1, d), ln_b.reshape(1, d))


def _s5_mixer(x, mod, w_in, ops, d_skip, w_glu, ln_g, ln_b, init, *, alpha, n_seq, seq_len):
    n, d = x.shape
    g = d // S5_GROUP
    ncs = seq_len // S5_CHUNK
    u_chunks = _s5_in(x, mod, w_in, seq_len=seq_len)
    if init is None:
        s0 = jnp.zeros((g, n_seq, S5_SW), F32)
    else:
        s0 = init.transpose(3, 0, 2, 1, 4).reshape(g, n_seq, S5_SW)
    y_chunks, fin = _s5_mix(u_chunks, ops, d_skip, s0, n_seq=n_seq, ncs=ncs)
    x_new = _s5_out(y_chunks, x, mod, w_glu, ln_g, ln_b, alpha=alpha, seq_len=seq_len)
    state = fin.reshape(g, n_seq, 2, 2, S5_STATE).transpose(1, 3, 2, 0, 4)
    return x_new, state


def _softplus(x):
    return jnp.maximum(x, 0.0) + jnp.log1p(jnp.exp(-jnp.abs(x)))


def _lru_scan_kernel(xr_ref, gate_ref, cw_ref, cb_ref, w4_ref, b4_ref, lam_ref, h0_ref, y_ref, fin_ref,
                     af_ref, bf_ref, ab_ref, bb_ref, end_ref, xpad_ref, *, n_seq, seq_len, seg):
    n = n_seq * seq_len
    n_seg = seq_len // seg
    ns = n_seq * n_seg
    pitch = seg + SUBLANES
    c = LANES
    chained = n_seg > 1
    half_rate = (0.5 * LRU_C) * _softplus(-lam_ref[0])
    cw, cb = cw_ref[0], cb_ref[0]
    w4_half = (0.5 * w4_ref[0].astype(F32)).astype(BF16)
    b4_half = 0.5 * b4_ref[0]

    zero_tile = jnp.zeros((SUBLANES, c), F32)
    xpad_ref[0:SUBLANES, :] = zero_tile
    xpad_ref[SUBLANES + n:2 * SUBLANES + n, :] = zero_tile

    def pad_copy(s, _):
        r0 = pl.multiple_of(s * seg, seg)
        xpad_ref[pl.ds(SUBLANES + r0, seg), :] = xr_ref[pl.ds(r0, seg), :]
        return 0

    lax.fori_loop(0, ns, pad_copy, 0)
    row8 = lax.broadcasted_iota(jnp.int32, (SUBLANES, c), 0)

    def gates(s, _):
        r0 = pl.multiple_of(s * seg, seg)
        p0 = pl.multiple_of(s * pitch, SUBLANES)
        seq_start = s % n_seg == 0
        seq_end = s % n_seg == n_seg - 1
        xc = cb
        for k in range(CONV_W):
            off = k - CONV_LEFT
            tap = xpad_ref[pl.ds(SUBLANES + r0 + off, seg), :]
            if off < 0:
                head = jnp.where(seq_start & (row8 < -off), 0.0, tap[:SUBLANES])
                tap = jnp.concatenate([head, tap[SUBLANES:]], axis=0)
            elif off > 0:
                tail = jnp.where(seq_end & (row8 >= SUBLANES - off), 0.0, tap[seg - SUBLANES:])
                tap = jnp.concatenate([tap[:seg - SUBLANES], tail], axis=0)
            xc = xc + tap * cw[k:k + 1]
        half_pre = jnp.dot(xc.astype(BF16), w4_half, preferred_element_type=F32) + b4_half
        xc_half = 0.5 * xc
        for d, (a_ref, b_ref) in enumerate(((af_ref, bf_ref), (ab_ref, bb_ref))):
            tanh_a = jnp.tanh(half_pre[:, 2 * d * c:(2 * d + 1) * c])
            tanh_x = jnp.tanh(half_pre[:, (2 * d + 1) * c:(2 * d + 2) * c])
            neg_log_a = half_rate[d:d + 1] * tanh_a + half_rate[d:d + 1]
            a = jnp.exp2(neg_log_a * -LOG2_E)
            a_ref[pl.ds(p0, seg), :] = a
            v = jnp.tanh(neg_log_a) * (a * a + 1.0)
            root = jnp.where(v > 0.0, v * lax.rsqrt(v), 0.0)
            b_ref[pl.ds(p0, seg), :] = root * (xc_half * tanh_x + xc_half)
        return 0

    lax.fori_loop(0, ns, gates, 0, unroll=2)

    def step(k, carry):
        hf, hb, pf, pb = carry
        rows_f = pl.ds(k, ns, stride=pitch)
        rows_b = pl.ds(seg - 1 - k, ns, stride=pitch)
        af, ab = af_ref[rows_f, :], ab_ref[rows_b, :]
        hf = af * hf + bf_ref[rows_f, :]
        hb = ab * hb + bb_ref[rows_b, :]
        bf_ref[rows_f, :] = hf
        bb_ref[rows_b, :] = hb
        if chained:
            pf, pb = pf * af, pb * ab
            af_ref[rows_f, :] = pf
            ab_ref[rows_b, :] = pb
        return hf, hb, pf, pb

    one = jnp.ones((ns, c), F32)
    if chained:
        init = (jnp.zeros((ns, c), F32), jnp.zeros((ns, c), F32), one, one)
    else:
        init = (h0_ref[0], h0_ref[1], one, one)
    hf, hb, pf, pb = lax.fori_loop(0, seg, step, init, unroll=2)

    if chained:
        end_ref[0], end_ref[1], end_ref[2], end_ref[3] = hf, hb, pf, pb
        hin_f, hin_b = h0_ref[0], h0_ref[1]
        for j in range(n_seg):
            jf, jb = pl.ds(j, n_seq, stride=n_seg), pl.ds(n_seg - 1 - j, n_seq, stride=n_seg)
            end_ref[4, jf, :] = hin_f
            end_ref[5, jb, :] = hin_b
            hin_f = end_ref[2, jf, :] * hin_f + end_ref[0, jf, :]
            hin_b = end_ref[3, jb, :] * hin_b + end_ref[1, jb, :]
        fin_ref[0] = hin_f
        fin_ref[1] = hin_b
    else:
        fin_ref[0] = hf
        fin_ref[1] = hb

    def combine(s, _):
        rows = pl.ds(pl.multiple_of(s * seg, seg), seg)
        prow = pl.ds(pl.multiple_of(s * pitch, SUBLANES), seg)
        h = bf_ref[prow, :] + bb_ref[prow, :]
        if chained:
            h = h + af_ref[prow, :] * end_ref[4, pl.ds(s, 1), :] + ab_ref[prow, :] * end_ref[5, pl.ds(s, 1), :]
        y_ref[rows, :] = (h * _gelu(gate_ref[rows, :])).astype(BF16)
        return 0

    lax.fori_loop(0, ns, combine, 0)


def _lru_scan(xg, conv_w, conv_b, w4, b4, lam, h0, *, n_seq, seq_len):
    n, d2 = xg.shape
    d = d2 // 2
    nt = d // LANES
    seg = min(LRU_SEGMENT, seq_len)
    ns = n // seg
    stripe = lambda off: pl.BlockSpec((n, LANES), lambda j: (0, j + off))
    per_t = lambda *s: pl.BlockSpec((1,) + s, lambda j: (j, 0, 0))
    return pl.pallas_call(
        functools.partial(_lru_scan_kernel, n_seq=n_seq, seq_len=seq_len, seg=seg),
        grid=(nt,),
        in_specs=[stripe(0), stripe(nt), per_t(CONV_W, LANES), per_t(1, LANES), per_t(LANES, 4 * LANES),
                  per_t(1, 4 * LANES), per_t(2, LANES), pl.BlockSpec((2, n_seq, LANES), lambda j: (0, 0, j))],
        out_specs=[pl.BlockSpec((n, LANES), lambda j: (0, j)),
                   pl.BlockSpec((2, n_seq, LANES), lambda j: (0, 0, j))],
        out_shape=[jax.ShapeDtypeStruct((n, d), BF16), jax.ShapeDtypeStruct((2, n_seq, d), F32)],
        scratch_shapes=([pltpu.VMEM((ns * (seg + SUBLANES), LANES), F32)] * 4
                        + [pltpu.VMEM((6, ns, LANES), F32), pltpu.VMEM((n + 2 * SUBLANES, LANES), F32)]),
        compiler_params=_params("arbitrary"),
        name="rglru_scan",
    )(xg, xg, conv_w, conv_b, w4, b4, lam, h0)


def _lru_out_kernel(y_ref, x_ref, mod_ref, w_ref, g_ref, b_ref, o_ref, *, alpha):
    out = jnp.dot(y_ref[...], w_ref[...], preferred_element_type=F32)
    o_ref[...] = _layer_norm(alpha * x_ref[...] + _gate(mod_ref, 1) * out, g_ref[...], b_ref[...])


def _lru_out(y, x, mod, w_out, ln_g, ln_b, *, alpha, seq_len):
    n, d = x.shape
    tm = min(TOKEN_TILE, n)
    row = pl.BlockSpec((tm, d), lambda i: (i, 0))
    return pl.pallas_call(
        functools.partial(_lru_out_kernel, alpha=alpha),
        grid=(n // tm,),
        in_specs=[row, row, _mod_spec(mod.shape[0], d, tm, seq_len), _const_spec(w_out.shape),
                  _const_spec((1, d)), _const_spec((1, d))],
        out_specs=row,
        out_shape=jax.ShapeDtypeStruct((n, d), F32),
        compiler_params=_params("arbitrary"),
        name="rglru_out_proj",
    )(y, x, mod, w_out, ln_g.reshape(1, d), ln_b.reshape(1, d))


def _lru_stripe_weights(conv_w, conv_b, w_a, b_a, w_x, b_x, lam):
    d = conv_w.shape[1]
    nt = d // LANES
    per = LANES // LRU_BLOCK

    def blockdiag(w):
        wb = w.reshape(nt, per, LRU_BLOCK, LRU_BLOCK)
        eye = jnp.eye(per, dtype=w.dtype)
        return jnp.einsum('tpij,pq->tpiqj', wb, eye).reshape(nt, LANES, LANES)

    w4 = jnp.concatenate([blockdiag(w_a[0]), blockdiag(w_x[0]), blockdiag(w_a[1]), blockdiag(w_x[1])], axis=2)
    b4 = jnp.concatenate([v.reshape(nt, 1, LANES) for v in (b_a[0], b_x[0], b_a[1], b_x[1])], axis=2)
    cw = conv_w.reshape(CONV_W, nt, LANES).transpose(1, 0, 2)
    cb = conv_b.reshape(nt, 1, LANES)
    lam_t = lam.reshape(2, nt, LANES).transpose(1, 0, 2)
    return cw, cb, w4.astype(BF16), b4, lam_t


def _lru_mixer(x, mod, w_in, stripe_w, w_out, ln_g, ln_b, init, *, alpha, n_seq, seq_len):
    n, d = x.shape
    xg = _inproj(x, mod, w_in, sub=1, seq_len=seq_len)
    h0 = jnp.zeros((2, n_seq, d), F32) if init is None else init.transpose(1, 0, 2)
    y, fin = _lru_scan(xg, *stripe_w, h0, n_seq=n_seq, seq_len=seq_len)
    x_new = _lru_out(y, x, mod, w_out, ln_g, ln_b, alpha=alpha, seq_len=seq_len)
    return x_new, fin.transpose(1, 0, 2)


def _grid_pos_embed(rows, d):
    t = jnp.arange(rows * GRID_W)
    row = (t // GRID_W).astype(F32)
    col = (t % GRID_W).astype(F32)
    quarter = d // 4
    omega = 1.0 / (10000.0 ** (jnp.arange(quarter, dtype=F32) / quarter))

    def emb(p):
        ang = p[:, None] * omega[None, :]
        return jnp.concatenate([jnp.sin(ang), jnp.cos(ang)], -1)

    return jnp.concatenate([emb(row), emb(col)], -1)


def _trunk(x3, mod, init_s5, init_lru, wts, pos):
    n_seq, seq_len, d = x3.shape
    depth = mod.shape[0]
    alpha = (2.0 * depth) ** 0.25
    x = x3.reshape(n_seq * seq_len, d)
    st_s5, st_lru = [], []
    for i in range(depth):
        j = i // 2
        m = mod[i]
        x = _ffn(x, m, wts["up"], wts["dn"], wts["ln_g"][i, 0], wts["ln_b"][i, 0],
                 layer=i, sub=0, alpha=alpha, seq_len=seq_len, pos=pos if i == 0 else None,
                 row_order="to_phase_major" if i % 2 == 0 else None)
        if i % 2 == 0:
            init = None if init_s5 is None else init_s5[:, j]
            x, st = _s5_mixer(x, m, wts["s5_w_in"][j], wts["s5_ops"][j], wts["s5_d"][j], wts["s5_w_glu"][j],
                              wts["ln_g"][i, 1], wts["ln_b"][i, 1], init,
                              alpha=alpha, n_seq=n_seq, seq_len=seq_len)
            st_s5.append(st)
        else:
            init = None if init_lru is None else init_lru[:, j]
            x, st = _lru_mixer(x, m, wts["lru_w_in"][j], wts["lru_stripe"][j], wts["lru_w_out"][j],
                               wts["ln_g"][i, 1], wts["ln_b"][i, 1], init,
                               alpha=alpha, n_seq=n_seq, seq_len=seq_len)
            st_lru.append(st)
        x = _ffn(x, m, wts["up"], wts["dn"], wts["ln_g"][i, 2], wts["ln_b"][i, 2],
                 layer=i, sub=2, alpha=alpha, seq_len=seq_len, row_order="to_natural" if i % 2 == 0 else None)
    return x.reshape(n_seq, seq_len, d), st_s5, st_lru


def kernel(x_prompt, x_sample, c, state_s5, state_lru, c_ctx, ada_w, ada_b, ln_g, ln_b, ffn_w_up, ffn_w_down, s5_w_in, s5_a_re, s5_a_im, s5_log_dt, s5_b_re, s5_b_im, s5_c_re, s5_c_im, s5_d, s5_w_glu, lru_w_in, lru_conv_w, lru_conv_b, lru_w_a, lru_b_a, lru_w_x, lru_b_x, lru_lambda, lru_w_out):
    depth, d, _ = ada_w.shape
    n_dec = c.shape[0]
    assert 1 + n_dec <= SUBLANES
    cond8 = jnp.concatenate([c_ctx[None, :], c, jnp.zeros((SUBLANES - 1 - n_dec, d), F32)], axis=0)
    mod = _modulation(cond8, ada_w, ada_b).reshape(depth, SUBLANES, 9, d)
    mod_ctx, mod_dec = mod[:, 0:1], mod[:, 1:1 + n_dec]

    wts = {
        "up": ffn_w_up.astype(BF16), "dn": ffn_w_down.astype(BF16),
        "ln_g": ln_g, "ln_b": ln_b,
        "s5_w_in": s5_w_in.astype(BF16), "s5_d": s5_d, "s5_w_glu": s5_w_glu.astype(BF16),
        "s5_ops": [_s5_operators(s5_a_re[j], s5_a_im[j], s5_log_dt[j], s5_b_re[j], s5_b_im[j],
                                 s5_c_re[j], s5_c_im[j]) for j in range(s5_w_in.shape[0])],
        "lru_w_in": lru_w_in.astype(BF16), "lru_w_out": lru_w_out.astype(BF16),
        "lru_stripe": [_lru_stripe_weights(lru_conv_w[j], lru_conv_b[j], lru_w_a[j], lru_b_a[j], lru_w_x[j],
                                           lru_b_x[j], lru_lambda[j]) for j in range(lru_w_in.shape[0])],
    }

    y_prompt, st_s5, st_lru = _trunk(x_prompt, mod_ctx, None, None, wts, None)
    new_state_s5 = jnp.stack(st_s5, 1).astype(x_prompt.dtype)
    new_state_lru = jnp.stack(st_lru, 1).astype(x_prompt.dtype)

    pos = _grid_pos_embed(x_sample.shape[1] // GRID_W, d).astype(x_sample.dtype)
    y_sample, _, _ = _trunk(x_sample, mod_dec, state_s5, state_lru, wts, pos)
    return (y_prompt, y_sample, new_state_s5, new_state_lru)
```

```python
import functools

import jax
import jax.numpy as jnp
from jax import lax
from jax.experimental import pallas as pl
from jax.experimental.pallas import tpu as pltpu

F32 = jnp.float32
BF16 = jnp.bfloat16

GRID_W = 64
S5_GROUP = 16
S5_STATE = 64
LRU_BLOCK = 64
LRU_C = 8.0
CONV_W = 4
CONV_LEFT = 2
LN_EPS = 1e-5
LOG2_E = 1.4426950408889634

LANES = 128
SUBLANES = 8
VMEM_LIMIT_BYTES = 56 * 1024 * 1024

S5_CHUNK = 16
S5_CW = S5_CHUNK * S5_GROUP
S5_SW = 4 * S5_STATE
S5_GROUP_BLOCK = 8
PERM_PITCH = S5_CHUNK + 4
S5_OPS_GROUP_BLOCK = 2
SCAN_PAD = 4
LRU_OUT_SUBTILES = 2
TOKEN_TILE = 512
FFN_SUBTILES = 2
S5_PROJ_SUBTILES = 2
LRU_SEGMENT = 256


def _params(*sem):
    return pltpu.CompilerParams(dimension_semantics=sem, vmem_limit_bytes=VMEM_LIMIT_BYTES)


def _const_spec(shape, lead=()):
    idx = tuple(lead) + (0,) * len(shape)
    return pl.BlockSpec((None,) * len(lead) + tuple(shape), lambda *_: idx, pipeline_mode=pl.Buffered(1))


def _layer_norm(z, g, b):
    mu = jnp.mean(z, axis=-1, keepdims=True)
    zc = z - mu
    var = jnp.mean(zc * zc, axis=-1, keepdims=True)
    return zc * lax.rsqrt(var + LN_EPS) * g + b


def _sigmoid(x):
    return 0.5 * jnp.tanh(0.5 * x) + 0.5


def _gelu(x):
    c = 0.7978845608028654
    h = 0.5 * x
    return h + h * jnp.tanh(x * ((c * 0.044715) * (x * x) + c))


def _modulate(x, mod_ref, sub):
    sh = mod_ref[0, 3 * sub:3 * sub + 1, :]
    sc = mod_ref[0, 3 * sub + 1:3 * sub + 2, :]
    return x * (1.0 + sc) + sh


def _gate(mod_ref, sub):
    return mod_ref[0, 3 * sub + 2:3 * sub + 3, :]


def _mod_spec(n_cond, d, tm, seq_len):
    if n_cond == 1:
        return pl.BlockSpec((1, 9, d), lambda i: (0, 0, 0))
    return pl.BlockSpec((1, 9, d), lambda i: ((i * tm) // seq_len, 0, 0))


def _mod_kernel(c_ref, w_ref, b_ref, o_ref):
    s = jax.nn.silu(c_ref[...]).astype(BF16)
    o_ref[0] = jnp.dot(s, w_ref[0].astype(BF16), preferred_element_type=F32) + b_ref[0]


def _modulation(cond8, ada_w, ada_b):
    depth, d, n9 = ada_w.shape
    tn = n9 // 4
    return pl.pallas_call(
        _mod_kernel,
        grid=(depth, n9 // tn),
        in_specs=[pl.BlockSpec((SUBLANES, d), lambda l, n: (0, 0)),
                  pl.BlockSpec((1, d, tn), lambda l, n: (l, 0, n)),
                  pl.BlockSpec((1, 1, tn), lambda l, n: (l, 0, n))],
        out_specs=pl.BlockSpec((1, SUBLANES, tn), lambda l, n: (l, 0, n)),
        out_shape=jax.ShapeDtypeStruct((depth, SUBLANES, n9), F32),
        compiler_params=_params("arbitrary", "arbitrary"),
        name="adaln_modulation",
    )(cond8, ada_w, ada_b.reshape(depth, 1, n9))


def _store_rows_permuted(res, o_ref, perm_ref, *, to_phase_major):
    tm, d = res.shape
    nch = tm // S5_CHUNK
    for v in range(d // LANES):
        lanes = slice(v * LANES, (v + 1) * LANES)
        if to_phase_major:
            for c in range(nch):
                perm_ref[v, c * PERM_PITCH:c * PERM_PITCH + S5_CHUNK, :] = res[c * S5_CHUNK:(c + 1) * S5_CHUNK, lanes]
            for t in range(S5_CHUNK):
                o_ref[t * nch:(t + 1) * nch, lanes] = perm_ref[v, pl.ds(t, nch, stride=PERM_PITCH), :]
        else:
            for t in range(S5_CHUNK):
                perm_ref[v, pl.ds(t, nch, stride=PERM_PITCH), :] = res[t * nch:(t + 1) * nch, lanes]
            for c in range(nch):
                o_ref[c * S5_CHUNK:(c + 1) * S5_CHUNK, lanes] = perm_ref[v, c * PERM_PITCH:c * PERM_PITCH + S5_CHUNK, :]


def _ffn_kernel(*refs, sub, alpha, d_ff, ck, has_pos, row_order):
    if has_pos:
        x_ref, pos_ref, mod_ref, wup_ref, wdn_ref, g_ref, b_ref, o_ref, act_ref, perm_ref = refs
    else:
        x_ref, mod_ref, wup_ref, wdn_ref, g_ref, b_ref, o_ref, act_ref, perm_ref = refs
    slabs = perm_ref.shape[0] * TOKEN_TILE // x_ref.shape[0]
    for j in range(x_ref.shape[0] // TOKEN_TILE):
        rows = slice(j * TOKEN_TILE, (j + 1) * TOKEN_TILE)
        x = x_ref[rows, :] + pos_ref[rows, :] if has_pos else x_ref[rows, :]
        h = _modulate(x, mod_ref, sub).astype(BF16)
        for c0 in range(0, d_ff, ck):
            a = jnp.dot(h, wup_ref[:, c0:c0 + ck], preferred_element_type=F32)
            g = jnp.dot(h, wup_ref[:, d_ff + c0:d_ff + c0 + ck], preferred_element_type=F32)
            act_ref[rows, c0:c0 + ck] = (jax.nn.silu(g) * a).astype(BF16)
        f = jnp.dot(act_ref[rows, :], wdn_ref[...], preferred_element_type=F32)
        z = alpha * x + (0.5 * _gate(mod_ref, sub)) * f
        res = _layer_norm(z, g_ref[...], b_ref[...])
        if row_order is None:
            o_ref[rows, :] = res
        else:
            _store_rows_permuted(res, o_ref.at[rows], perm_ref.at[j * slabs:(j + 1) * slabs],
                                 to_phase_major=row_order == "to_phase_major")


def _ffn(x, mod, w_up, w_dn, ln_g, ln_b, *, layer, sub, alpha, seq_len, pos=None, row_order=None):
    n, d = x.shape
    d_ff = w_dn.shape[2]
    which = (layer, sub // 2)
    tm = min(FFN_SUBTILES * TOKEN_TILE, n)
    ck = 256 if d_ff % 256 == 0 else d_ff
    row = pl.BlockSpec((tm, d), lambda i: (i, 0))
    in_specs = [row]
    args = [x]
    if pos is not None:
        per_seq = seq_len // tm
        in_specs.append(pl.BlockSpec((tm, d), lambda i: (i % per_seq, 0)))
        args.append(pos)
    in_specs += [_mod_spec(mod.shape[0], d, tm, seq_len), _const_spec(w_up.shape[2:], which),
                 _const_spec(w_dn.shape[2:], which), _const_spec((1, d)), _const_spec((1, d))]
    args += [mod, w_up, w_dn, ln_g.reshape(1, d), ln_b.reshape(1, d)]
    perm_rows = TOKEN_TILE // S5_CHUNK * PERM_PITCH
    perm_shape = (tm // TOKEN_TILE * (d // LANES), perm_rows, LANES) if row_order else (1, SUBLANES, LANES)
    return pl.pallas_call(
        functools.partial(_ffn_kernel, sub=sub, alpha=alpha, d_ff=d_ff, ck=ck, has_pos=pos is not None,
                          row_order=row_order),
        grid=(n // tm,),
        in_specs=in_specs,
        out_specs=row,
        out_shape=jax.ShapeDtypeStruct((n, d), F32),
        scratch_shapes=[pltpu.VMEM((tm, d_ff), BF16), pltpu.VMEM(perm_shape, F32)],
        compiler_params=_params("arbitrary"),
        name="ffn_sublayer",
    )(*args)


def _inproj_kernel(x_ref, mod_ref, w_ref, o_ref, *, sub):
    h = _modulate(x_ref[...], mod_ref, sub).astype(BF16)
    o_ref[...] = jnp.dot(h, w_ref[...], preferred_element_type=F32)


def _inproj(x, mod, w, *, sub, seq_len):
    n, d = x.shape
    dn = w.shape[1]
    tm = min(TOKEN_TILE, n)
    return pl.pallas_call(
        functools.partial(_inproj_kernel, sub=sub),
        grid=(n // tm,),
        in_specs=[pl.BlockSpec((tm, d), lambda i: (i, 0)), _mod_spec(mod.shape[0], d, tm, seq_len),
                  _const_spec(w.shape)],
        out_specs=pl.BlockSpec((tm, dn), lambda i: (i, 0)),
        out_shape=jax.ShapeDtypeStruct((n, dn), F32),
        compiler_params=_params("arbitrary"),
        name="mixer_in_proj",
    )(x, mod, w)


def _s5_discretize(are, aim, ldt):
    dt = jnp.exp(ldt)
    mag = jnp.exp(dt * are)
    abr = mag * jnp.cos(dt * aim)
    abi = mag * jnp.sin(dt * aim)
    den = are * are + aim * aim
    nr, ni = abr - 1.0, abi
    qr = (nr * are + ni * aim) / den
    qi = (ni * are - nr * aim) / den
    return abr, abi, qr, qi


def _split_bf16(x):
    hi = x.astype(BF16)
    return hi, (x - hi.astype(F32)).astype(BF16)


def _s5_ops_kernel(arow_ref, btr_ref, bti_ref, ctr_ref, cti_ref, m_ref, mst_ref, min_ref, a16_ref):
    for g in range(arow_ref.shape[0]):
        m_ref[g], mst_ref[g], min_ref[g], a16_ref[g] = _s5_group_operators(
            arow_ref[g], btr_ref[g], bti_ref[g], ctr_ref[g], cti_ref[g])


def _s5_group_operators(arow, btr, bti, ctr, cti):
    t_chunk, grp, w = S5_CHUNK, S5_GROUP, S5_CW
    colblk = lax.broadcasted_iota(jnp.int32, (w, w), 1) // grp
    pow_row = lax.broadcasted_iota(jnp.int32, (t_chunk, w), 0)
    pow_blk = lax.broadcasted_iota(jnp.int32, (t_chunk, w), 1) // grp
    tn_dims = (((0,), (0,)), ((), ()))

    abr2, abi2, qr2, qi2 = _s5_discretize(arow[0:2], arow[2:4], arow[4:6])
    lag_tabs, decay, carry_in = [], [], []
    for d in range(2):
        abr, abi, qr, qi = abr2[d:d + 1], abi2[d:d + 1], qr2[d:d + 1], qi2[d:d + 1]
        lr = [qr * btr - qi * bti]
        li = [qr * bti + qi * btr]
        pr, pi = [abr], [abi]
        for _ in range(t_chunk - 1):
            lr.append(lr[-1] * abr - li[-1] * abi)
            li.append(lr[-2] * abi + li[-1] * abr)
            pr.append(pr[-1] * abr - pi[-1] * abi)
            pi.append(pr[-2] * abi + pi[-1] * abr)
        decay.append((pr[-1], pi[-1]))
        if d == 0:
            lr, li = lr[::-1], li[::-1]
        lag_tabs.append((jnp.concatenate(lr, axis=0), jnp.concatenate(li, axis=0)))

        place = (pow_blk == (pow_row if d == 0 else t_chunk - 1 - pow_row)).astype(BF16)
        e_t = []
        for tbl in (jnp.concatenate(pr, axis=0), jnp.concatenate(pi, axis=0)):
            hi, lo = _split_bf16(tbl)
            e_t.append(lax.dot_general(hi, place, tn_dims, preferred_element_type=F32)
                       + lax.dot_general(lo, place, tn_dims, preferred_element_type=F32))
        er_t, ei_t = e_t
        carry_in.append((ctr * er_t - cti * ei_t, ctr * ei_t + cti * er_t))

    lhs = jnp.concatenate([jnp.concatenate([l_re, -l_im], axis=1) for l_re, l_im in lag_tabs], axis=0)
    l_hi, l_lo = _split_bf16(lhs)
    c_hi, c_lo = _split_bf16(jnp.concatenate([ctr, cti], axis=0))
    tab = (jnp.dot(l_hi, c_hi, preferred_element_type=F32) + jnp.dot(l_lo, c_hi, preferred_element_type=F32)
           + jnp.dot(l_hi, c_lo, preferred_element_type=F32))
    tab_f, tab_b = tab[:w], tab[w:]
    acc = jnp.zeros((w, w), F32)
    for tp in range(t_chunk):
        cut_f = (t_chunk - 1 - tp) * grp
        cut_b = tp * grp
        sh_f = tab_f if cut_f == 0 else jnp.concatenate([tab_f[cut_f:], jnp.zeros((cut_f, w), F32)], 0)
        sh_b = tab_b if cut_b == 0 else jnp.concatenate([jnp.zeros((cut_b, w), F32), tab_b[:w - cut_b]], 0)
        acc = jnp.where(colblk == tp, sh_f + sh_b, acc)
    (fr, fi), (br, bi) = lag_tabs
    (wfr, wfi), (wbr, wbi) = carry_in
    (efr, efi), (ebr, ebi) = decay
    return (acc.astype(BF16),
            jnp.concatenate([fr, br, fi, bi], axis=1).astype(BF16),
            jnp.concatenate([wfr, wbr, -wfi, -wbi], axis=0).astype(BF16),
            jnp.concatenate([efr, ebr, efi, ebi], axis=1))


def _s5_operators(a_re, a_im, log_dt, b_re, b_im, c_re, c_im):
    _, g, p = a_re.shape
    ldt = jnp.broadcast_to(log_dt[:, :, None], (2, g, p))
    arow = jnp.concatenate([a_re, a_im, ldt, jnp.zeros((2, g, p), F32)], axis=0).transpose(1, 0, 2)
    bt_re, bt_im = b_re.transpose(0, 2, 1), b_im.transpose(0, 2, 1)
    ct_re = jnp.tile(c_re.transpose(0, 2, 1), (1, 1, S5_CHUNK))
    ct_im = jnp.tile(c_im.transpose(0, 2, 1), (1, 1, S5_CHUNK))
    gb = S5_OPS_GROUP_BLOCK
    per_g = lambda *s: pl.BlockSpec((gb,) + s, lambda i: (i, 0, 0))
    w = S5_CW
    return pl.pallas_call(
        _s5_ops_kernel,
        grid=(g // gb,),
        in_specs=[per_g(SUBLANES, p), per_g(S5_GROUP, p), per_g(S5_GROUP, p), per_g(p, w), per_g(p, w)],
        out_specs=[per_g(w, w), per_g(w, S5_SW), per_g(S5_SW, w), per_g(1, S5_SW)],
        out_shape=[jax.ShapeDtypeStruct((g, w, w), BF16), jax.ShapeDtypeStruct((g, w, S5_SW), BF16),
                   jax.ShapeDtypeStruct((g, S5_SW, w), BF16), jax.ShapeDtypeStruct((g, 1, S5_SW), F32)],
        compiler_params=_params("arbitrary"),
        name="s5_operators",
    )(arow, bt_re, bt_im, ct_re, ct_im)


def _lane_block_transpose(vs):
    blk = lax.broadcasted_iota(jnp.int32, vs[0].shape, 1) // S5_GROUP
    for dist in (4, 2, 1):
        keep = (blk & dist) == 0
        out = list(vs)
        for r in range(len(vs)):
            if r & dist:
                continue
            lo, hi = vs[r], vs[r + dist]
            out[r] = jnp.where(keep, lo, pltpu.roll(hi, dist * S5_GROUP, axis=1))
            out[r + dist] = jnp.where(keep, pltpu.roll(lo, LANES - dist * S5_GROUP, axis=1), hi)
        vs = out
    return vs


def _s5_in_kernel(x_ref, mod_ref, w_ref, o_ref):
    nch = TOKEN_TILE // S5_CHUNK
    d = w_ref.shape[0]
    per = LANES // S5_GROUP
    for j in range(x_ref.shape[0] // TOKEN_TILE):
        h = _modulate(x_ref[j * TOKEN_TILE:(j + 1) * TOKEN_TILE, :], mod_ref, 1).astype(BF16)
        u = jnp.dot(h, w_ref[...], preferred_element_type=F32)
        for v in range(d // LANES):
            for th in range(S5_CHUNK // per):
                src = [u[(per * th + r) * nch:(per * th + r + 1) * nch, v * LANES:(v + 1) * LANES]
                       for r in range(per)]
                dst = _lane_block_transpose(src)
                for q in range(per):
                    o_ref[per * v + q, j * nch:(j + 1) * nch, th * LANES:(th + 1) * LANES] = dst[q]


def _s5_in(x, mod, w, *, seq_len):
    n, d = x.shape
    g = d // S5_GROUP
    tm = TOKEN_TILE
    nch = tm // S5_CHUNK
    nc = n // S5_CHUNK
    return pl.pallas_call(
        _s5_in_kernel,
        grid=(n // tm,),
        in_specs=[pl.BlockSpec((tm, d), lambda i: (i, 0)),
                  _mod_spec(mod.shape[0], d, tm, seq_len), _const_spec(w.shape)],
        out_specs=pl.BlockSpec((g, nch, S5_CW), lambda i: (0, i, 0)),
        out_shape=jax.ShapeDtypeStruct((g, nc, S5_CW), F32),
        compiler_params=_params("arbitrary"),
        name="s5_in_proj",
    )(x, mod, w)


def _s5_mix_kernel(u_ref, m_ref, mst_ref, min_ref, a16_ref, dsk_ref, s0_ref, y_ref, fin_ref,
                   loc_ref, sa_ref, sb_ref, *, n_seq, ncs):
    gb = u_ref.shape[0]
    half = S5_SW // 2
    pitch = ncs + SCAN_PAD
    for g in range(gb):
        loc = jnp.dot(u_ref[g].astype(BF16), mst_ref[g], preferred_element_type=F32)
        for s in range(n_seq):
            loc_ref[0, g, s * pitch:s * pitch + ncs, :] = loc[s * ncs:(s + 1) * ncs, :half]
            loc_ref[1, g, s * pitch:s * pitch + ncs, :] = loc[s * ncs:(s + 1) * ncs, half:]
    is_fwd = lax.broadcasted_iota(jnp.int32, (n_seq, half), 1) < S5_STATE
    decay = [(a16_ref[g][:, :half], a16_ref[g][:, half:]) for g in range(gb)]

    def step(k, carry):
        rows_f = pl.ds(k, n_seq, stride=pitch)
        rows_b = pl.ds(ncs - 1 - k, n_seq, stride=pitch)
        out = []
        for g in range(gb):
            sr, si = carry[2 * g], carry[2 * g + 1]
            ar, ai = decay[g]
            sa_ref[0, g, rows_f, :] = sr
            sa_ref[1, g, rows_f, :] = si
            sb_ref[0, g, rows_b, :] = sr
            sb_ref[1, g, rows_b, :] = si
            lr = jnp.where(is_fwd, loc_ref[0, g, rows_f, :], loc_ref[0, g, rows_b, :])
            li = jnp.where(is_fwd, loc_ref[1, g, rows_f, :], loc_ref[1, g, rows_b, :])
            out += [ar * sr - ai * si + lr, ar * si + ai * sr + li]
        return tuple(out)

    init = []
    for g in range(gb):
        init += [s0_ref[g][:, :half], s0_ref[g][:, half:]]
    fin = lax.fori_loop(0, ncs, step, tuple(init))
    sel = lax.broadcasted_iota(jnp.int32, (ncs, half), 1) < S5_STATE
    for g in range(gb):
        fin_ref[g] = jnp.concatenate([fin[2 * g], fin[2 * g + 1]], axis=1)
        parts = []
        for ri in range(2):
            parts.append(jnp.concatenate(
                [jnp.where(sel, sa_ref[ri, g, s * pitch:s * pitch + ncs, :], sb_ref[ri, g, s * pitch:s * pitch + ncs, :])
                 for s in range(n_seq)], axis=0))
        s_in = jnp.concatenate(parts, axis=1).astype(BF16)
        u = u_ref[g]
        y_ref[g] = (jnp.dot(u.astype(BF16), m_ref[g], preferred_element_type=F32)
                    + jnp.dot(s_in, min_ref[g], preferred_element_type=F32)
                    + dsk_ref[g] * u)


def _s5_mix(u_chunks, ops, d_skip, s0, *, n_seq, ncs):
    g, nc, w = u_chunks.shape
    m, mst, m_in, a16 = ops
    gb = S5_GROUP_BLOCK
    dsk = jnp.tile(d_skip.reshape(g, 1, S5_GROUP), (1, 1, S5_CHUNK))
    per_g = lambda *s: pl.BlockSpec((gb,) + s, lambda i: (i, 0, 0))
    rows = n_seq * (ncs + SCAN_PAD)
    return pl.pallas_call(
        functools.partial(_s5_mix_kernel, n_seq=n_seq, ncs=ncs),
        grid=(g // gb,),
        in_specs=[per_g(nc, w), per_g(w, w), per_g(w, S5_SW), per_g(S5_SW, w), per_g(1, S5_SW), per_g(1, w),
                  per_g(n_seq, S5_SW)],
        out_specs=[per_g(nc, w), per_g(n_seq, S5_SW)],
        out_shape=[jax.ShapeDtypeStruct((g, nc, w), F32), jax.ShapeDtypeStruct((g, n_seq, S5_SW), F32)],
        scratch_shapes=[pltpu.VMEM((2, gb, rows, S5_SW // 2), F32)] * 3,
        compiler_params=_params("arbitrary"),
        name="s5_chunk_scan",
    )(u_chunks, m, mst, m_in, a16, dsk, s0)


def _s5_out_kernel(y_ref, x_ref, mod_ref, w_ref, g_ref, b_ref, o_ref, *, alpha):
    nch = TOKEN_TILE // S5_CHUNK
    d = w_ref.shape[0]
    per = LANES // S5_GROUP
    for j in range(x_ref.shape[0] // TOKEN_TILE):
        rows = slice(j * TOKEN_TILE, (j + 1) * TOKEN_TILE)
        phase = [[None] * (d // LANES) for _ in range(S5_CHUNK)]
        for v in range(d // LANES):
            for th in range(S5_CHUNK // per):
                src = [y_ref[per * v + q, j * nch:(j + 1) * nch, th * LANES:(th + 1) * LANES] for q in range(per)]
                dst = _lane_block_transpose(src)
                for r in range(per):
                    phase[per * th + r][v] = dst[r]
        y = jnp.concatenate([jnp.concatenate(p, axis=1) for p in phase], axis=0)
        z = jnp.dot(_gelu(y).astype(BF16), w_ref[...], preferred_element_type=F32)
        out = z[:, :d] * _sigmoid(z[:, d:])
        o_ref[rows, :] = _layer_norm(alpha * x_ref[rows, :] + _gate(mod_ref, 1) * out, g_ref[...], b_ref[...])


def _s5_out(y_chunks, x, mod, w_glu, ln_g, ln_b, *, alpha, seq_len):
    n, d = x.shape
    g = d // S5_GROUP
    tm = S5_PROJ_SUBTILES * TOKEN_TILE
    nch = tm // S5_CHUNK
    row = pl.BlockSpec((tm, d), lambda i: (i, 0))
    return pl.pallas_call(
        functools.partial(_s5_out_kernel, alpha=alpha),
        grid=(n // tm,),
        in_specs=[pl.BlockSpec((g, nch, S5_CW), lambda i: (0, i, 0)), row,
                  _mod_spec(mod.shape[0], d, tm, seq_len), _const_spec(w_glu.shape),
                  _const_spec((1, d)), _const_spec((1, d))],
        out_specs=row,
        out_shape=jax.ShapeDtypeStruct((n, d), F32),
        compiler_params=_params("arbitrary"),
        name="s5_out_proj",
    )(y_chunks, x, mod, w_glu, ln_g.reshape(1, d), ln_b.reshape(1, d))


def _s5_mixer(x, mod, w_in, ops, d_skip, w_glu, ln_g, ln_b, init, *, alpha, n_seq, seq_len):
    n, d = x.shape
    g = d // S5_GROUP
    ncs = seq_len // S5_CHUNK
    u_chunks = _s5_in(x, mod, w_in, seq_len=seq_len)
    if init is None:
        s0 = jnp.zeros((g, n_seq, S5_SW), F32)
    else:
        s0 = init.transpose(3, 0, 2, 1, 4).reshape(g, n_seq, S5_SW)
    y_chunks, fin = _s5_mix(u_chunks, ops, d_skip, s0, n_seq=n_seq, ncs=ncs)
    x_new = _s5_out(y_chunks, x, mod, w_glu, ln_g, ln_b, alpha=alpha, seq_len=seq_len)
    state = fin.reshape(g, n_seq, 2, 2, S5_STATE).transpose(1, 3, 2, 0, 4)
    return x_new, state


def _softplus(x):
    return jnp.maximum(x, 0.0) + jnp.log1p(jnp.exp(-jnp.abs(x)))


def _lru_scan_kernel(xr_ref, gate_ref, cw_ref, cb_ref, w4_ref, b4_ref, lam_ref, h0_ref, y_ref, fin_ref,
                     af_ref, bf_ref, ab_ref, bb_ref, end_ref, xpad_ref, *, n_seq, seq_len, seg):
    n = n_seq * seq_len
    n_seg = seq_len // seg
    ns = n_seq * n_seg
    pitch = seg + SCAN_PAD
    c = LANES
    chained = n_seg > 1
    half_rate = (0.5 * LRU_C) * _softplus(-lam_ref[0])
    cw, cb = cw_ref[0], cb_ref[0]
    w4_half = (0.5 * w4_ref[0].astype(F32)).astype(BF16)
    b4_half = 0.5 * b4_ref[0]

    zero_tile = jnp.zeros((SUBLANES, c), F32)
    xpad_ref[0:SUBLANES, :] = zero_tile
    xpad_ref[SUBLANES + n:2 * SUBLANES + n, :] = zero_tile

    def pad_copy(s, _):
        r0 = pl.multiple_of(s * seg, seg)
        xpad_ref[pl.ds(SUBLANES + r0, seg), :] = xr_ref[pl.ds(r0, seg), :]
        return 0

    lax.fori_loop(0, ns, pad_copy, 0)
    row8 = lax.broadcasted_iota(jnp.int32, (SUBLANES, c), 0)

    def gates(s, _):
        r0 = pl.multiple_of(s * seg, seg)
        p0 = pl.multiple_of(s * pitch, SCAN_PAD)
        seq_start = s % n_seg == 0
        seq_end = s % n_seg == n_seg - 1
        xc = cb
        for k in range(CONV_W):
            off = k - CONV_LEFT
            tap = xpad_ref[pl.ds(SUBLANES + r0 + off, seg), :]
            if off < 0:
                head = jnp.where(seq_start & (row8 < -off), 0.0, tap[:SUBLANES])
                tap = jnp.concatenate([head, tap[SUBLANES:]], axis=0)
            elif off > 0:
                tail = jnp.where(seq_end & (row8 >= SUBLANES - off), 0.0, tap[seg - SUBLANES:])
                tap = jnp.concatenate([tap[:seg - SUBLANES], tail], axis=0)
            xc = xc + tap * cw[k:k + 1]
        half_pre = jnp.dot(xc.astype(BF16), w4_half, preferred_element_type=F32) + b4_half
        xc_half = 0.5 * xc
        for d, (a_ref, b_ref) in enumerate(((af_ref, bf_ref), (ab_ref, bb_ref))):
            tanh_a = jnp.tanh(half_pre[:, 2 * d * c:(2 * d + 1) * c])
            tanh_x = jnp.tanh(half_pre[:, (2 * d + 1) * c:(2 * d + 2) * c])
            neg_log_a = half_rate[d:d + 1] * tanh_a + half_rate[d:d + 1]
            a = jnp.exp2(neg_log_a * -LOG2_E)
            a_ref[pl.ds(p0, seg), :] = a
            v = jnp.tanh(neg_log_a) * (a * a + 1.0)
            root = jnp.where(v > 0.0, v * lax.rsqrt(v), 0.0)
            b_ref[pl.ds(p0, seg), :] = root * (xc_half * tanh_x + xc_half)
        return 0

    lax.fori_loop(0, ns, gates, 0, unroll=2)

    def step(k, carry):
        hf, hb, pf, pb = carry
        rows_f = pl.ds(k, ns, stride=pitch)
        rows_b = pl.ds(seg - 1 - k, ns, stride=pitch)
        af, ab = af_ref[rows_f, :], ab_ref[rows_b, :]
        hf = af * hf + bf_ref[rows_f, :]
        hb = ab * hb + bb_ref[rows_b, :]
        bf_ref[rows_f, :] = hf
        bb_ref[rows_b, :] = hb
        if chained:
            pf, pb = pf * af, pb * ab
            af_ref[rows_f, :] = pf
            ab_ref[rows_b, :] = pb
        return hf, hb, pf, pb

    one = jnp.ones((ns, c), F32)
    if chained:
        init = (jnp.zeros((ns, c), F32), jnp.zeros((ns, c), F32), one, one)
    else:
        init = (h0_ref[0], h0_ref[1], one, one)
    hf, hb, pf, pb = lax.fori_loop(0, seg, step, init, unroll=2)

    if chained:
        end_ref[0], end_ref[1], end_ref[2], end_ref[3] = hf, hb, pf, pb
        hin_f, hin_b = h0_ref[0], h0_ref[1]
        for j in range(n_seg):
            jf, jb = pl.ds(j, n_seq, stride=n_seg), pl.ds(n_seg - 1 - j, n_seq, stride=n_seg)
            end_ref[4, jf, :] = hin_f
            end_ref[5, jb, :] = hin_b
            hin_f = end_ref[2, jf, :] * hin_f + end_ref[0, jf, :]
            hin_b = end_ref[3, jb, :] * hin_b + end_ref[1, jb, :]
        fin_ref[0] = hin_f
        fin_ref[1] = hin_b
    else:
        fin_ref[0] = hf
        fin_ref[1] = hb

    def combine(s, _):
        rows = pl.ds(pl.multiple_of(s * seg, seg), seg)
        prow = pl.ds(pl.multiple_of(s * pitch, SCAN_PAD), seg)
        h = bf_ref[prow, :] + bb_ref[prow, :]
        if chained:
            h = h + af_ref[prow, :] * end_ref[4, pl.ds(s, 1), :] + ab_ref[prow, :] * end_ref[5, pl.ds(s, 1), :]
        y_ref[rows, :] = (h * _gelu(gate_ref[rows, :])).astype(BF16)
        return 0

    lax.fori_loop(0, ns, combine, 0)


def _lru_scan(xg, conv_w, conv_b, w4, b4, lam, h0, *, n_seq, seq_len):
    n, d2 = xg.shape
    d = d2 // 2
    nt = d // LANES
    seg = min(LRU_SEGMENT, seq_len)
    ns = n // seg
    stripe = lambda off: pl.BlockSpec((n, LANES), lambda j: (0, j + off))
    per_t = lambda *s: pl.BlockSpec((1,) + s, lambda j: (j, 0, 0))
    return pl.pallas_call(
        functools.partial(_lru_scan_kernel, n_seq=n_seq, seq_len=seq_len, seg=seg),
        grid=(nt,),
        in_specs=[stripe(0), stripe(nt), per_t(CONV_W, LANES), per_t(1, LANES), per_t(LANES, 4 * LANES),
                  per_t(1, 4 * LANES), per_t(2, LANES), pl.BlockSpec((2, n_seq, LANES), lambda j: (0, 0, j))],
        out_specs=[pl.BlockSpec((n, LANES), lambda j: (0, j)),
                   pl.BlockSpec((2, n_seq, LANES), lambda j: (0, 0, j))],
        out_shape=[jax.ShapeDtypeStruct((n, d), BF16), jax.ShapeDtypeStruct((2, n_seq, d), F32)],
        scratch_shapes=([pltpu.VMEM((ns * (seg + SCAN_PAD), LANES), F32)] * 4
                        + [pltpu.VMEM((6, ns, LANES), F32), pltpu.VMEM((n + 2 * SUBLANES, LANES), F32)]),
        compiler_params=_params("arbitrary"),
        name="rglru_scan",
    )(xg, xg, conv_w, conv_b, w4, b4, lam, h0)


def _lru_out_kernel(y_ref, x_ref, mod_ref, w_ref, g_ref, b_ref, o_ref, *, alpha):
    for j in range(x_ref.shape[0] // TOKEN_TILE):
        rows = slice(j * TOKEN_TILE, (j + 1) * TOKEN_TILE)
        out = jnp.dot(y_ref[rows, :], w_ref[...], preferred_element_type=F32)
        o_ref[rows, :] = _layer_norm(alpha * x_ref[rows, :] + _gate(mod_ref, 1) * out, g_ref[...], b_ref[...])


def _lru_out(y, x, mod, w_out, ln_g, ln_b, *, alpha, seq_len):
    n, d = x.shape
    tm = LRU_OUT_SUBTILES * TOKEN_TILE
    row = pl.BlockSpec((tm, d), lambda i: (i, 0))
    return pl.pallas_call(
        functools.partial(_lru_out_kernel, alpha=alpha),
        grid=(n // tm,),
        in_specs=[row, row, _mod_spec(mod.shape[0], d, tm, seq_len), _const_spec(w_out.shape),
                  _const_spec((1, d)), _const_spec((1, d))],
        out_specs=row,
        out_shape=jax.ShapeDtypeStruct((n, d), F32),
        compiler_params=_params("arbitrary"),
        name="rglru_out_proj",
    )(y, x, mod, w_out, ln_g.reshape(1, d), ln_b.reshape(1, d))


def _lru_stripe_weights(conv_w, conv_b, w_a, b_a, w_x, b_x, lam):
    d = conv_w.shape[1]
    nt = d // LANES
    per = LANES // LRU_BLOCK

    def blockdiag(w):
        wb = w.reshape(nt, per, LRU_BLOCK, LRU_BLOCK)
        eye = jnp.eye(per, dtype=w.dtype)
        return jnp.einsum('tpij,pq->tpiqj', wb, eye).reshape(nt, LANES, LANES)

    w4 = jnp.concatenate([blockdiag(w_a[0]), blockdiag(w_x[0]), blockdiag(w_a[1]), blockdiag(w_x[1])], axis=2)
    b4 = jnp.concatenate([v.reshape(nt, 1, LANES) for v in (b_a[0], b_x[0], b_a[1], b_x[1])], axis=2)
    cw = conv_w.reshape(CONV_W, nt, LANES).transpose(1, 0, 2)
    cb = conv_b.reshape(nt, 1, LANES)
    lam_t = lam.reshape(2, nt, LANES).transpose(1, 0, 2)
    return cw, cb, w4.astype(BF16), b4, lam_t


def _lru_mixer(x, mod, w_in, stripe_w, w_out, ln_g, ln_b, init, *, alpha, n_seq, seq_len):
    n, d = x.shape
    xg = _inproj(x, mod, w_in, sub=1, seq_len=seq_len)
    h0 = jnp.zeros((2, n_seq, d), F32) if init is None else init.transpose(1, 0, 2)
    y, fin = _lru_scan(xg, *stripe_w, h0, n_seq=n_seq, seq_len=seq_len)
    x_new = _lru_out(y, x, mod, w_out, ln_g, ln_b, alpha=alpha, seq_len=seq_len)
    return x_new, fin.transpose(1, 0, 2)


def _grid_pos_embed(rows, d):
    t = jnp.arange(rows * GRID_W)
    row = (t // GRID_W).astype(F32)
    col = (t % GRID_W).astype(F32)
    quarter = d // 4
    omega = 1.0 / (10000.0 ** (jnp.arange(quarter, dtype=F32) / quarter))

    def emb(p):
        ang = p[:, None] * omega[None, :]
        return jnp.concatenate([jnp.sin(ang), jnp.cos(ang)], -1)

    return jnp.concatenate([emb(row), emb(col)], -1)


def _trunk(x3, mod, init_s5, init_lru, wts, pos):
    n_seq, seq_len, d = x3.shape
    depth = mod.shape[0]
    alpha = (2.0 * depth) ** 0.25
    x = x3.reshape(n_seq * seq_len, d)
    st_s5, st_lru = [], []
    for i in range(depth):
        j = i // 2
        m = mod[i]
        x = _ffn(x, m, wts["up"], wts["dn"], wts["ln_g"][i, 0], wts["ln_b"][i, 0],
                 layer=i, sub=0, alpha=alpha, seq_len=seq_len, pos=pos if i == 0 else None,
                 row_order="to_phase_major" if i % 2 == 0 else None)
        if i % 2 == 0:
            init = None if init_s5 is None else init_s5[:, j]
            x, st = _s5_mixer(x, m, wts["s5_w_in"][j], wts["s5_ops"][j], wts["s5_d"][j], wts["s5_w_glu"][j],
                              wts["ln_g"][i, 1], wts["ln_b"][i, 1], init,
                              alpha=alpha, n_seq=n_seq, seq_len=seq_len)
            st_s5.append(st)
        else:
            init = None if init_lru is None else init_lru[:, j]
            x, st = _lru_mixer(x, m, wts["lru_w_in"][j], wts["lru_stripe"][j], wts["lru_w_out"][j],
                               wts["ln_g"][i, 1], wts["ln_b"][i, 1], init,
                               alpha=alpha, n_seq=n_seq, seq_len=seq_len)
            st_lru.append(st)
        x = _ffn(x, m, wts["up"], wts["dn"], wts["ln_g"][i, 2], wts["ln_b"][i, 2],
                 layer=i, sub=2, alpha=alpha, seq_len=seq_len, row_order="to_natural" if i % 2 == 0 else None)
    return x.reshape(n_seq, seq_len, d), st_s5, st_lru


def kernel(x_prompt, x_sample, c, state_s5, state_lru, c_ctx, ada_w, ada_b, ln_g, ln_b, ffn_w_up, ffn_w_down, s5_w_in, s5_a_re, s5_a_im, s5_log_dt, s5_b_re, s5_b_im, s5_c_re, s5_c_im, s5_d, s5_w_glu, lru_w_in, lru_conv_w, lru_conv_b, lru_w_a, lru_b_a, lru_w_x, lru_b_x, lru_lambda, lru_w_out):
    depth, d, _ = ada_w.shape
    n_dec = c.shape[0]
    assert 1 + n_dec <= SUBLANES
    cond8 = jnp.concatenate([c_ctx[None, :], c, jnp.zeros((SUBLANES - 1 - n_dec, d), F32)], axis=0)
    mod = _modulation(cond8, ada_w, ada_b).reshape(depth, SUBLANES, 9, d)
    mod_ctx, mod_dec = mod[:, 0:1], mod[:, 1:1 + n_dec]

    wts = {
        "up": ffn_w_up.astype(BF16), "dn": ffn_w_down.astype(BF16),
        "ln_g": ln_g, "ln_b": ln_b,
        "s5_w_in": s5_w_in.astype(BF16), "s5_d": s5_d, "s5_w_glu": s5_w_glu.astype(BF16),
        "s5_ops": [_s5_operators(s5_a_re[j], s5_a_im[j], s5_log_dt[j], s5_b_re[j], s5_b_im[j],
                                 s5_c_re[j], s5_c_im[j]) for j in range(s5_w_in.shape[0])],
        "lru_w_in": lru_w_in.astype(BF16), "lru_w_out": lru_w_out.astype(BF16),
        "lru_stripe": [_lru_stripe_weights(lru_conv_w[j], lru_conv_b[j], lru_w_a[j], lru_b_a[j], lru_w_x[j],
                                           lru_b_x[j], lru_lambda[j]) for j in range(lru_w_in.shape[0])],
    }

    y_prompt, st_s5, st_lru = _trunk(x_prompt, mod_ctx, None, None, wts, None)
    new_state_s5 = jnp.stack(st_s5, 1).astype(x_prompt.dtype)
    new_state_lru = jnp.stack(st_lru, 1).astype(x_prompt.dtype)

    pos = _grid_pos_embed(x_sample.shape[1] // GRID_W, d).astype(x_sample.dtype)
    y_sample, _, _ = _trunk(x_sample, mod_dec, state_s5, state_lru, wts, pos)
    return (y_prompt, y_sample, new_state_s5, new_state_lru)
```

```python
import functools

import jax
import jax.numpy as jnp
from jax import lax
from jax.experimental import pallas as pl
from jax.experimental.pallas import tpu as pltpu

F32 = jnp.float32
BF16 = jnp.bfloat16

GRID_W = 64
S5_GROUP = 16
S5_STATE = 64
LRU_BLOCK = 64
LRU_C = 8.0
CONV_W = 4
CONV_LEFT = 2
LN_EPS = 1e-5
LOG2_E = 1.4426950408889634

LANES = 128
SUBLANES = 8
VMEM_LIMIT_BYTES = 56 * 1024 * 1024

S5_CHUNK = 16
S5_CW = S5_CHUNK * S5_GROUP
S5_SW = 4 * S5_STATE
S5_GROUP_BLOCK = 8
PERM_PITCH = S5_CHUNK + 4
S5_OPS_GROUP_BLOCK = 2
SCAN_PAD = 4
LRU_OUT_SUBTILES = 2
TOKEN_TILE = 512
FFN_SUBTILES = 2
FFN_CHUNK = 256
S5_PROJ_SUBTILES = 2
LRU_SEGMENT = 256


def _params(*sem):
    return pltpu.CompilerParams(dimension_semantics=sem, vmem_limit_bytes=VMEM_LIMIT_BYTES)


def _const_spec(shape, lead=()):
    idx = tuple(lead) + (0,) * len(shape)
    return pl.BlockSpec((None,) * len(lead) + tuple(shape), lambda *_: idx, pipeline_mode=pl.Buffered(1))


def _layer_norm(z, g, b):
    mu = jnp.mean(z, axis=-1, keepdims=True)
    zc = z - mu
    var = jnp.mean(zc * zc, axis=-1, keepdims=True)
    return zc * lax.rsqrt(var + LN_EPS) * g + b


def _sigmoid(x):
    return 0.5 * jnp.tanh(0.5 * x) + 0.5


def _gelu(x):
    c = 0.7978845608028654
    h = 0.5 * x
    return h + h * jnp.tanh(x * ((c * 0.044715) * (x * x) + c))


def _modulate(x, mod_ref, sub):
    sh = mod_ref[0, 3 * sub:3 * sub + 1, :]
    sc = mod_ref[0, 3 * sub + 1:3 * sub + 2, :]
    return x * (1.0 + sc) + sh


def _gate(mod_ref, sub):
    return mod_ref[0, 3 * sub + 2:3 * sub + 3, :]


def _mod_spec(n_cond, d, tm, seq_len):
    if n_cond == 1:
        return pl.BlockSpec((1, 9, d), lambda i: (0, 0, 0))
    return pl.BlockSpec((1, 9, d), lambda i: ((i * tm) // seq_len, 0, 0))


def _mod_kernel(c_ref, w_ref, b_ref, o_ref):
    s = jax.nn.silu(c_ref[...]).astype(BF16)
    o_ref[0] = jnp.dot(s, w_ref[0].astype(BF16), preferred_element_type=F32) + b_ref[0]


def _modulation(cond8, ada_w, ada_b):
    depth, d, n9 = ada_w.shape
    tn = n9 // 4
    return pl.pallas_call(
        _mod_kernel,
        grid=(depth, n9 // tn),
        in_specs=[pl.BlockSpec((SUBLANES, d), lambda l, n: (0, 0)),
                  pl.BlockSpec((1, d, tn), lambda l, n: (l, 0, n)),
                  pl.BlockSpec((1, 1, tn), lambda l, n: (l, 0, n))],
        out_specs=pl.BlockSpec((1, SUBLANES, tn), lambda l, n: (l, 0, n)),
        out_shape=jax.ShapeDtypeStruct((depth, SUBLANES, n9), F32),
        compiler_params=_params("arbitrary", "arbitrary"),
        name="adaln_modulation",
    )(cond8, ada_w, ada_b.reshape(depth, 1, n9))


def _store_rows_permuted(res, o_ref, perm_ref, *, to_phase_major):
    tm, d = res.shape
    nch = tm // S5_CHUNK
    for v in range(d // LANES):
        lanes = slice(v * LANES, (v + 1) * LANES)
        if to_phase_major:
            for c in range(nch):
                perm_ref[v, c * PERM_PITCH:c * PERM_PITCH + S5_CHUNK, :] = res[c * S5_CHUNK:(c + 1) * S5_CHUNK, lanes]
            for t in range(S5_CHUNK):
                o_ref[t * nch:(t + 1) * nch, lanes] = perm_ref[v, pl.ds(t, nch, stride=PERM_PITCH), :]
        else:
            for t in range(S5_CHUNK):
                perm_ref[v, pl.ds(t, nch, stride=PERM_PITCH), :] = res[t * nch:(t + 1) * nch, lanes]
            for c in range(nch):
                o_ref[c * S5_CHUNK:(c + 1) * S5_CHUNK, lanes] = perm_ref[v, c * PERM_PITCH:c * PERM_PITCH + S5_CHUNK, :]


def _ffn_kernel(*refs, sub, alpha, d_ff, ck, has_pos, row_order, which):
    if has_pos:
        x_ref, pos_ref = refs[:2]
        refs = refs[2:]
    else:
        x_ref, pos_ref = refs[0], None
        refs = refs[1:]
    (mod_ref, wup_hbm, wdn_hbm, g_ref, b_ref, o_ref,
     act_ref, perm_ref, wup_ref, wdn_ref, up_stage, dn_stage, sems) = refs
    layer, ffn = which
    n_ck = d_ff // ck
    slabs = perm_ref.shape[0] * TOKEN_TILE // x_ref.shape[0]

    def up_copy(k):
        col = (k % 2) * d_ff + (k // 2) * ck
        return pltpu.make_async_copy(wup_hbm.at[layer, ffn, :, pl.ds(col, ck)], up_stage.at[k % 2], sems.at[0, k % 2])

    def dn_copy(k):
        return pltpu.make_async_copy(wdn_hbm.at[layer, ffn, pl.ds(k * ck, ck), :], dn_stage.at[k % 2], sems.at[1, k % 2])

    def receive_chunk(c):
        for half in range(2):
            k = 2 * c + half
            up_copy(k).wait()
            wup_ref[:, half * d_ff + c * ck:half * d_ff + (c + 1) * ck] = up_stage[half].astype(BF16)
            if k + 2 < 2 * n_ck:
                up_copy(k + 2).start()
        dn_copy(c).wait()
        wdn_ref[c * ck:(c + 1) * ck, :] = dn_stage[c % 2].astype(BF16)
        if c + 2 < n_ck:
            dn_copy(c + 2).start()

    def subtile(j, stream_weights):
        rows = slice(j * TOKEN_TILE, (j + 1) * TOKEN_TILE)
        x = x_ref[rows, :] + pos_ref[rows, :] if has_pos else x_ref[rows, :]
        h = _modulate(x, mod_ref, sub).astype(BF16)
        for c in range(n_ck):
            c0 = c * ck
            if stream_weights:
                receive_chunk(c)
            a = jnp.dot(h, wup_ref[:, c0:c0 + ck], preferred_element_type=F32)
            g = jnp.dot(h, wup_ref[:, d_ff + c0:d_ff + c0 + ck], preferred_element_type=F32)
            act_ref[rows, c0:c0 + ck] = (jax.nn.silu(g) * a).astype(BF16)
        f = jnp.dot(act_ref[rows, :], wdn_ref[...], preferred_element_type=F32)
        z = alpha * x + (0.5 * _gate(mod_ref, sub)) * f
        res = _layer_norm(z, g_ref[...], b_ref[...])
        if row_order is None:
            o_ref[rows, :] = res
        else:
            _store_rows_permuted(res, o_ref.at[rows], perm_ref.at[j * slabs:(j + 1) * slabs],
                                 to_phase_major=row_order == "to_phase_major")

    n_sub = x_ref.shape[0] // TOKEN_TILE
    first = pl.program_id(0) == 0

    @pl.when(first)
    def _():
        for k in range(2):
            up_copy(k).start()
            dn_copy(k).start()
        for j in range(n_sub):
            subtile(j, stream_weights=j == 0)

    @pl.when(jnp.logical_not(first))
    def _():
        for j in range(n_sub):
            subtile(j, stream_weights=False)


def _ffn(x, mod, w_up, w_dn, ln_g, ln_b, *, layer, sub, alpha, seq_len, pos=None, row_order=None):
    n, d = x.shape
    d_ff = w_dn.shape[2]
    tm = TOKEN_TILE if pos is not None else FFN_SUBTILES * TOKEN_TILE
    ck = FFN_CHUNK
    assert d_ff % ck == 0 and d_ff // ck >= 2
    row = pl.BlockSpec((tm, d), lambda i: (i, 0))
    in_specs = [row]
    args = [x]
    if pos is not None:
        per_seq = seq_len // tm
        in_specs.append(pl.BlockSpec((tm, d), lambda i: (i % per_seq, 0)))
        args.append(pos)
    hbm = pl.BlockSpec(memory_space=pl.ANY)
    in_specs += [_mod_spec(mod.shape[0], d, tm, seq_len), hbm, hbm, _const_spec((1, d)), _const_spec((1, d))]
    args += [mod, w_up, w_dn, ln_g.reshape(1, d), ln_b.reshape(1, d)]
    perm_rows = TOKEN_TILE // S5_CHUNK * PERM_PITCH
    perm_shape = (tm // TOKEN_TILE * (d // LANES), perm_rows, LANES) if row_order else (1, SUBLANES, LANES)
    return pl.pallas_call(
        functools.partial(_ffn_kernel, sub=sub, alpha=alpha, d_ff=d_ff, ck=ck, has_pos=pos is not None,
                          row_order=row_order, which=(layer, sub // 2)),
        grid=(n // tm,),
        in_specs=in_specs,
        out_specs=row,
        out_shape=jax.ShapeDtypeStruct((n, d), F32),
        scratch_shapes=[pltpu.VMEM((tm, d_ff), BF16), pltpu.VMEM(perm_shape, F32),
                        pltpu.VMEM((d, 2 * d_ff), BF16), pltpu.VMEM((d_ff, d), BF16),
                        pltpu.VMEM((2, d, ck), F32), pltpu.VMEM((2, ck, d), F32),
                        pltpu.SemaphoreType.DMA((2, 2))],
        compiler_params=_params("arbitrary"),
        name="ffn_sublayer",
    )(*args)


def _inproj_kernel(x_ref, mod_ref, w_ref, o_ref, *, sub):
    h = _modulate(x_ref[...], mod_ref, sub).astype(BF16)
    o_ref[...] = jnp.dot(h, w_ref[...], preferred_element_type=F32)


def _inproj(x, mod, w, *, sub, seq_len):
    n, d = x.shape
    dn = w.shape[1]
    tm = min(TOKEN_TILE, n)
    return pl.pallas_call(
        functools.partial(_inproj_kernel, sub=sub),
        grid=(n // tm,),
        in_specs=[pl.BlockSpec((tm, d), lambda i: (i, 0)), _mod_spec(mod.shape[0], d, tm, seq_len),
                  _const_spec(w.shape)],
        out_specs=pl.BlockSpec((tm, dn), lambda i: (i, 0)),
        out_shape=jax.ShapeDtypeStruct((n, dn), F32),
        compiler_params=_params("arbitrary"),
        name="mixer_in_proj",
    )(x, mod, w)


def _s5_discretize(are, aim, ldt):
    dt = jnp.exp(ldt)
    mag = jnp.exp(dt * are)
    abr = mag * jnp.cos(dt * aim)
    abi = mag * jnp.sin(dt * aim)
    den = are * are + aim * aim
    nr, ni = abr - 1.0, abi
    qr = (nr * are + ni * aim) / den
    qi = (ni * are - nr * aim) / den
    return abr, abi, qr, qi


def _split_bf16(x):
    hi = x.astype(BF16)
    return hi, (x - hi.astype(F32)).astype(BF16)


def _s5_ops_kernel(arow_ref, btr_ref, bti_ref, ctr_ref, cti_ref, m_ref, mst_ref, min_ref, a16_ref):
    for g in range(arow_ref.shape[0]):
        m_ref[g], mst_ref[g], min_ref[g], a16_ref[g] = _s5_group_operators(
            arow_ref[g], btr_ref[g], bti_ref[g], ctr_ref[g], cti_ref[g])


def _s5_group_operators(arow, btr, bti, ctr, cti):
    t_chunk, grp, w = S5_CHUNK, S5_GROUP, S5_CW
    colblk = lax.broadcasted_iota(jnp.int32, (w, w), 1) // grp
    pow_row = lax.broadcasted_iota(jnp.int32, (t_chunk, w), 0)
    pow_blk = lax.broadcasted_iota(jnp.int32, (t_chunk, w), 1) // grp
    tn_dims = (((0,), (0,)), ((), ()))

    abr2, abi2, qr2, qi2 = _s5_discretize(arow[0:2], arow[2:4], arow[4:6])
    lag_tabs, decay, carry_in = [], [], []
    for d in range(2):
        abr, abi, qr, qi = abr2[d:d + 1], abi2[d:d + 1], qr2[d:d + 1], qi2[d:d + 1]
        lr = [qr * btr - qi * bti]
        li = [qr * bti + qi * btr]
        pr, pi = [abr], [abi]
        for _ in range(t_chunk - 1):
            lr.append(lr[-1] * abr - li[-1] * abi)
            li.append(lr[-2] * abi + li[-1] * abr)
            pr.append(pr[-1] * abr - pi[-1] * abi)
            pi.append(pr[-2] * abi + pi[-1] * abr)
        decay.append((pr[-1], pi[-1]))
        if d == 0:
            lr, li = lr[::-1], li[::-1]
        lag_tabs.append((jnp.concatenate(lr, axis=0), jnp.concatenate(li, axis=0)))

        place = (pow_blk == (pow_row if d == 0 else t_chunk - 1 - pow_row)).astype(BF16)
        e_t = []
        for tbl in (jnp.concatenate(pr, axis=0), jnp.concatenate(pi, axis=0)):
            hi, lo = _split_bf16(tbl)
            e_t.append(lax.dot_general(hi, place, tn_dims, preferred_element_type=F32)
                       + lax.dot_general(lo, place, tn_dims, preferred_element_type=F32))
        er_t, ei_t = e_t
        carry_in.append((ctr * er_t - cti * ei_t, ctr * ei_t + cti * er_t))

    lhs = jnp.concatenate([jnp.concatenate([l_re, -l_im], axis=1) for l_re, l_im in lag_tabs], axis=0)
    l_hi, l_lo = _split_bf16(lhs)
    c_hi, c_lo = _split_bf16(jnp.concatenate([ctr, cti], axis=0))
    tab = (jnp.dot(l_hi, c_hi, preferred_element_type=F32) + jnp.dot(l_lo, c_hi, preferred_element_type=F32)
           + jnp.dot(l_hi, c_lo, preferred_element_type=F32))
    tab_f, tab_b = tab[:w], tab[w:]
    acc = jnp.zeros((w, w), F32)
    for tp in range(t_chunk):
        cut_f = (t_chunk - 1 - tp) * grp
        cut_b = tp * grp
        sh_f = tab_f if cut_f == 0 else jnp.concatenate([tab_f[cut_f:], jnp.zeros((cut_f, w), F32)], 0)
        sh_b = tab_b if cut_b == 0 else jnp.concatenate([jnp.zeros((cut_b, w), F32), tab_b[:w - cut_b]], 0)
        acc = jnp.where(colblk == tp, sh_f + sh_b, acc)
    (fr, fi), (br, bi) = lag_tabs
    (wfr, wfi), (wbr, wbi) = carry_in
    (efr, efi), (ebr, ebi) = decay
    return (acc.astype(BF16),
            jnp.concatenate([fr, br, fi, bi], axis=1).astype(BF16),
            jnp.concatenate([wfr, wbr, -wfi, -wbi], axis=0).astype(BF16),
            jnp.concatenate([efr, ebr, efi, ebi], axis=1))


def _s5_operators(a_re, a_im, log_dt, b_re, b_im, c_re, c_im):
    _, g, p = a_re.shape
    ldt = jnp.broadcast_to(log_dt[:, :, None], (2, g, p))
    arow = jnp.concatenate([a_re, a_im, ldt, jnp.zeros((2, g, p), F32)], axis=0).transpose(1, 0, 2)
    bt_re, bt_im = b_re.transpose(0, 2, 1), b_im.transpose(0, 2, 1)
    ct_re = jnp.tile(c_re.transpose(0, 2, 1), (1, 1, S5_CHUNK))
    ct_im = jnp.tile(c_im.transpose(0, 2, 1), (1, 1, S5_CHUNK))
    gb = S5_OPS_GROUP_BLOCK
    per_g = lambda *s: pl.BlockSpec((gb,) + s, lambda i: (i, 0, 0))
    w = S5_CW
    return pl.pallas_call(
        _s5_ops_kernel,
        grid=(g // gb,),
        in_specs=[per_g(SUBLANES, p), per_g(S5_GROUP, p), per_g(S5_GROUP, p), per_g(p, w), per_g(p, w)],
        out_specs=[per_g(w, w), per_g(w, S5_SW), per_g(S5_SW, w), per_g(1, S5_SW)],
        out_shape=[jax.ShapeDtypeStruct((g, w, w), BF16), jax.ShapeDtypeStruct((g, w, S5_SW), BF16),
                   jax.ShapeDtypeStruct((g, S5_SW, w), BF16), jax.ShapeDtypeStruct((g, 1, S5_SW), F32)],
        compiler_params=_params("arbitrary"),
        name="s5_operators",
    )(arow, bt_re, bt_im, ct_re, ct_im)


def _lane_block_transpose(vs):
    blk = lax.broadcasted_iota(jnp.int32, vs[0].shape, 1) // S5_GROUP
    for dist in (4, 2, 1):
        keep = (blk & dist) == 0
        out = list(vs)
        for r in range(len(vs)):
            if r & dist:
                continue
            lo, hi = vs[r], vs[r + dist]
            out[r] = jnp.where(keep, lo, pltpu.roll(hi, dist * S5_GROUP, axis=1))
            out[r + dist] = jnp.where(keep, pltpu.roll(lo, LANES - dist * S5_GROUP, axis=1), hi)
        vs = out
    return vs


def _s5_in_kernel(x_ref, mod_ref, w_ref, o_ref):
    nch = TOKEN_TILE // S5_CHUNK
    d = w_ref.shape[0]
    per = LANES // S5_GROUP
    for j in range(x_ref.shape[0] // TOKEN_TILE):
        h = _modulate(x_ref[j * TOKEN_TILE:(j + 1) * TOKEN_TILE, :], mod_ref, 1).astype(BF16)
        u = jnp.dot(h, w_ref[...], preferred_element_type=F32)
        for v in range(d // LANES):
            for th in range(S5_CHUNK // per):
                src = [u[(per * th + r) * nch:(per * th + r + 1) * nch, v * LANES:(v + 1) * LANES]
                       for r in range(per)]
                dst = _lane_block_transpose(src)
                for q in range(per):
                    o_ref[per * v + q, j * nch:(j + 1) * nch, th * LANES:(th + 1) * LANES] = dst[q]


def _s5_in(x, mod, w, *, seq_len):
    n, d = x.shape
    g = d // S5_GROUP
    tm = TOKEN_TILE
    nch = tm // S5_CHUNK
    nc = n // S5_CHUNK
    return pl.pallas_call(
        _s5_in_kernel,
        grid=(n // tm,),
        in_specs=[pl.BlockSpec((tm, d), lambda i: (i, 0)),
                  _mod_spec(mod.shape[0], d, tm, seq_len), _const_spec(w.shape)],
        out_specs=pl.BlockSpec((g, nch, S5_CW), lambda i: (0, i, 0)),
        out_shape=jax.ShapeDtypeStruct((g, nc, S5_CW), F32),
        compiler_params=_params("arbitrary"),
        name="s5_in_proj",
    )(x, mod, w)


def _s5_mix_kernel(u_ref, m_ref, mst_ref, min_ref, a16_ref, dsk_ref, s0_ref, y_ref, fin_ref,
                   loc_ref, sa_ref, sb_ref, *, n_seq, ncs):
    gb = u_ref.shape[0]
    half = S5_SW // 2
    pitch = ncs + SCAN_PAD
    for g in range(gb):
        loc = jnp.dot(u_ref[g].astype(BF16), mst_ref[g], preferred_element_type=F32)
        for s in range(n_seq):
            loc_ref[0, g, s * pitch:s * pitch + ncs, :] = loc[s * ncs:(s + 1) * ncs, :half]
            loc_ref[1, g, s * pitch:s * pitch + ncs, :] = loc[s * ncs:(s + 1) * ncs, half:]
    is_fwd = lax.broadcasted_iota(jnp.int32, (n_seq, half), 1) < S5_STATE
    decay = [(a16_ref[g][:, :half], a16_ref[g][:, half:]) for g in range(gb)]

    def step(k, carry):
        rows_f = pl.ds(k, n_seq, stride=pitch)
        rows_b = pl.ds(ncs - 1 - k, n_seq, stride=pitch)
        out = []
        for g in range(gb):
            sr, si = carry[2 * g], carry[2 * g + 1]
            ar, ai = decay[g]
            sa_ref[0, g, rows_f, :] = sr
            sa_ref[1, g, rows_f, :] = si
            sb_ref[0, g, rows_b, :] = sr
            sb_ref[1, g, rows_b, :] = si
            lr = jnp.where(is_fwd, loc_ref[0, g, rows_f, :], loc_ref[0, g, rows_b, :])
            li = jnp.where(is_fwd, loc_ref[1, g, rows_f, :], loc_ref[1, g, rows_b, :])
            out += [ar * sr - ai * si + lr, ar * si + ai * sr + li]
        return tuple(out)

    init = []
    for g in range(gb):
        init += [s0_ref[g][:, :half], s0_ref[g][:, half:]]
    fin = lax.fori_loop(0, ncs, step, tuple(init))
    sel = lax.broadcasted_iota(jnp.int32, (ncs, half), 1) < S5_STATE
    for g in range(gb):
        fin_ref[g] = jnp.concatenate([fin[2 * g], fin[2 * g + 1]], axis=1)
        parts = []
        for ri in range(2):
            parts.append(jnp.concatenate(
                [jnp.where(sel, sa_ref[ri, g, s * pitch:s * pitch + ncs, :], sb_ref[ri, g, s * pitch:s * pitch + ncs, :])
                 for s in range(n_seq)], axis=0))
        s_in = jnp.concatenate(parts, axis=1).astype(BF16)
        u = u_ref[g]
        y_ref[g] = (jnp.dot(u.astype(BF16), m_ref[g], preferred_element_type=F32)
                    + jnp.dot(s_in, min_ref[g], preferred_element_type=F32)
                    + dsk_ref[g] * u)


def _s5_mix(u_chunks, ops, d_skip, s0, *, n_seq, ncs):
    g, nc, w = u_chunks.shape
    m, mst, m_in, a16 = ops
    gb = S5_GROUP_BLOCK
    dsk = jnp.tile(d_skip.reshape(g, 1, S5_GROUP), (1, 1, S5_CHUNK))
    per_g = lambda *s: pl.BlockSpec((gb,) + s, lambda i: (i, 0, 0))
    rows = n_seq * (ncs + SCAN_PAD)
    return pl.pallas_call(
        functools.partial(_s5_mix_kernel, n_seq=n_seq, ncs=ncs),
        grid=(g // gb,),
        in_specs=[per_g(nc, w), per_g(w, w), per_g(w, S5_SW), per_g(S5_SW, w), per_g(1, S5_SW), per_g(1, w),
                  per_g(n_seq, S5_SW)],
        out_specs=[per_g(nc, w), per_g(n_seq, S5_SW)],
        out_shape=[jax.ShapeDtypeStruct((g, nc, w), F32), jax.ShapeDtypeStruct((g, n_seq, S5_SW), F32)],
        scratch_shapes=[pltpu.VMEM((2, gb, rows, S5_SW // 2), F32)] * 3,
        compiler_params=_params("arbitrary"),
        name="s5_chunk_scan",
    )(u_chunks, m, mst, m_in, a16, dsk, s0)


def _s5_out_kernel(y_ref, x_ref, mod_ref, w_ref, g_ref, b_ref, o_ref, *, alpha):
    nch = TOKEN_TILE // S5_CHUNK
    d = w_ref.shape[0]
    per = LANES // S5_GROUP
    for j in range(x_ref.shape[0] // TOKEN_TILE):
        rows = slice(j * TOKEN_TILE, (j + 1) * TOKEN_TILE)
        phase = [[None] * (d // LANES) for _ in range(S5_CHUNK)]
        for v in range(d // LANES):
            for th in range(S5_CHUNK // per):
                src = [y_ref[per * v + q, j * nch:(j + 1) * nch, th * LANES:(th + 1) * LANES] for q in range(per)]
                dst = _lane_block_transpose(src)
                for r in range(per):
                    phase[per * th + r][v] = dst[r]
        y = jnp.concatenate([jnp.concatenate(p, axis=1) for p in phase], axis=0)
        z = jnp.dot(_gelu(y).astype(BF16), w_ref[...], preferred_element_type=F32)
        out = z[:, :d] * _sigmoid(z[:, d:])
        o_ref[rows, :] = _layer_norm(alpha * x_ref[rows, :] + _gate(mod_ref, 1) * out, g_ref[...], b_ref[...])


def _s5_out(y_chunks, x, mod, w_glu, ln_g, ln_b, *, alpha, seq_len):
    n, d = x.shape
    g = d // S5_GROUP
    tm = S5_PROJ_SUBTILES * TOKEN_TILE
    nch = tm // S5_CHUNK
    row = pl.BlockSpec((tm, d), lambda i: (i, 0))
    return pl.pallas_call(
        functools.partial(_s5_out_kernel, alpha=alpha),
        grid=(n // tm,),
        in_specs=[pl.BlockSpec((g, nch, S5_CW), lambda i: (0, i, 0)), row,
                  _mod_spec(mod.shape[0], d, tm, seq_len), _const_spec(w_glu.shape),
                  _const_spec((1, d)), _const_spec((1, d))],
        out_specs=row,
        out_shape=jax.ShapeDtypeStruct((n, d), F32),
        compiler_params=_params("arbitrary"),
        name="s5_out_proj",
    )(y_chunks, x, mod, w_glu, ln_g.reshape(1, d), ln_b.reshape(1, d))


def _s5_mixer(x, mod, w_in, ops, d_skip, w_glu, ln_g, ln_b, init, *, alpha, n_seq, seq_len):
    n, d = x.shape
    g = d // S5_GROUP
    ncs = seq_len // S5_CHUNK
    u_chunks = _s5_in(x, mod, w_in, seq_len=seq_len)
    if init is None:
        s0 = jnp.zeros((g, n_seq, S5_SW), F32)
    else:
        s0 = init.transpose(3, 0, 2, 1, 4).reshape(g, n_seq, S5_SW)
    y_chunks, fin = _s5_mix(u_chunks, ops, d_skip, s0, n_seq=n_seq, ncs=ncs)
    x_new = _s5_out(y_chunks, x, mod, w_glu, ln_g, ln_b, alpha=alpha, seq_len=seq_len)
    state = fin.reshape(g, n_seq, 2, 2, S5_STATE).transpose(1, 3, 2, 0, 4)
    return x_new, state


def _softplus(x):
    return jnp.maximum(x, 0.0) + jnp.log1p(jnp.exp(-jnp.abs(x)))


def _lru_scan_kernel(xr_ref, gate_ref, cw_ref, cb_ref, w4_ref, b4_ref, lam_ref, h0_ref, y_ref, fin_ref,
                     af_ref, bf_ref, ab_ref, bb_ref, end_ref, xpad_ref, *, n_seq, seq_len, seg):
    n = n_seq * seq_len
    n_seg = seq_len // seg
    ns = n_seq * n_seg
    pitch = seg + SCAN_PAD
    c = LANES
    chained = n_seg > 1
    half_rate = (0.5 * LRU_C) * _softplus(-lam_ref[0])
    cw, cb = cw_ref[0], cb_ref[0]
    w4_half = (0.5 * w4_ref[0].astype(F32)).astype(BF16)
    b4_half = 0.5 * b4_ref[0]

    zero_tile = jnp.zeros((SUBLANES, c), F32)
    xpad_ref[0:SUBLANES, :] = zero_tile
    xpad_ref[SUBLANES + n:2 * SUBLANES + n, :] = zero_tile

    def pad_copy(s, _):
        r0 = pl.multiple_of(s * seg, seg)
        xpad_ref[pl.ds(SUBLANES + r0, seg), :] = xr_ref[pl.ds(r0, seg), :]
        return 0

    lax.fori_loop(0, ns, pad_copy, 0)
    row8 = lax.broadcasted_iota(jnp.int32, (SUBLANES, c), 0)

    def gates(s, _):
        r0 = pl.multiple_of(s * seg, seg)
        p0 = pl.multiple_of(s * pitch, SCAN_PAD)
        seq_start = s % n_seg == 0
        seq_end = s % n_seg == n_seg - 1
        xc = cb
        for k in range(CONV_W):
            off = k - CONV_LEFT
            tap = xpad_ref[pl.ds(SUBLANES + r0 + off, seg), :]
            if off < 0:
                head = jnp.where(seq_start & (row8 < -off), 0.0, tap[:SUBLANES])
                tap = jnp.concatenate([head, tap[SUBLANES:]], axis=0)
            elif off > 0:
                tail = jnp.where(seq_end & (row8 >= SUBLANES - off), 0.0, tap[seg - SUBLANES:])
                tap = jnp.concatenate([tap[:seg - SUBLANES], tail], axis=0)
            xc = xc + tap * cw[k:k + 1]
        half_pre = jnp.dot(xc.astype(BF16), w4_half, preferred_element_type=F32) + b4_half
        xc_half = 0.5 * xc
        for d, (a_ref, b_ref) in enumerate(((af_ref, bf_ref), (ab_ref, bb_ref))):
            tanh_a = jnp.tanh(half_pre[:, 2 * d * c:(2 * d + 1) * c])
            tanh_x = jnp.tanh(half_pre[:, (2 * d + 1) * c:(2 * d + 2) * c])
            neg_log_a = half_rate[d:d + 1] * tanh_a + half_rate[d:d + 1]
            a = jnp.exp2(neg_log_a * -LOG2_E)
            a_ref[pl.ds(p0, seg), :] = a
            v = jnp.tanh(neg_log_a) * (a * a + 1.0)
            root = jnp.where(v > 0.0, v * lax.rsqrt(v), 0.0)
            b_ref[pl.ds(p0, seg), :] = root * (xc_half * tanh_x + xc_half)
        return 0

    lax.fori_loop(0, ns, gates, 0, unroll=2)

    def step(k, carry):
        hf, hb, pf, pb = carry
        rows_f = pl.ds(k, ns, stride=pitch)
        rows_b = pl.ds(seg - 1 - k, ns, stride=pitch)
        af, ab = af_ref[rows_f, :], ab_ref[rows_b, :]
        hf = af * hf + bf_ref[rows_f, :]
        hb = ab * hb + bb_ref[rows_b, :]
        bf_ref[rows_f, :] = hf
        bb_ref[rows_b, :] = hb
        if chained:
            pf, pb = pf * af, pb * ab
            af_ref[rows_f, :] = pf
            ab_ref[rows_b, :] = pb
        return hf, hb, pf, pb

    one = jnp.ones((ns, c), F32)
    if chained:
        init = (jnp.zeros((ns, c), F32), jnp.zeros((ns, c), F32), one, one)
    else:
        init = (h0_ref[0], h0_ref[1], one, one)
    hf, hb, pf, pb = lax.fori_loop(0, seg, step, init, unroll=2)

    if chained:
        end_ref[0], end_ref[1], end_ref[2], end_ref[3] = hf, hb, pf, pb
        hin_f, hin_b = h0_ref[0], h0_ref[1]
        for j in range(n_seg):
            jf, jb = pl.ds(j, n_seq, stride=n_seg), pl.ds(n_seg - 1 - j, n_seq, stride=n_seg)
            end_ref[4, jf, :] = hin_f
            end_ref[5, jb, :] = hin_b
            hin_f = end_ref[2, jf, :] * hin_f + end_ref[0, jf, :]
            hin_b = end_ref[3, jb, :] * hin_b + end_ref[1, jb, :]
        fin_ref[0] = hin_f
        fin_ref[1] = hin_b
    else:
        fin_ref[0] = hf
        fin_ref[1] = hb

    def combine(s, _):
        rows = pl.ds(pl.multiple_of(s * seg, seg), seg)
        prow = pl.ds(pl.multiple_of(s * pitch, SCAN_PAD), seg)
        h = bf_ref[prow, :] + bb_ref[prow, :]
        if chained:
            h = h + af_ref[prow, :] * end_ref[4, pl.ds(s, 1), :] + ab_ref[prow, :] * end_ref[5, pl.ds(s, 1), :]
        y_ref[rows, :] = (h * _gelu(gate_ref[rows, :])).astype(BF16)
        return 0

    lax.fori_loop(0, ns, combine, 0)


def _lru_scan(xg, conv_w, conv_b, w4, b4, lam, h0, *, n_seq, seq_len):
    n, d2 = xg.shape
    d = d2 // 2
    nt = d // LANES
    seg = min(LRU_SEGMENT, seq_len)
    ns = n // seg
    stripe = lambda off: pl.BlockSpec((n, LANES), lambda j: (0, j + off))
    per_t = lambda *s: pl.BlockSpec((1,) + s, lambda j: (j, 0, 0))
    return pl.pallas_call(
        functools.partial(_lru_scan_kernel, n_seq=n_seq, seq_len=seq_len, seg=seg),
        grid=(nt,),
        in_specs=[stripe(0), stripe(nt), per_t(CONV_W, LANES), per_t(1, LANES), per_t(LANES, 4 * LANES),
                  per_t(1, 4 * LANES), per_t(2, LANES), pl.BlockSpec((2, n_seq, LANES), lambda j: (0, 0, j))],
        out_specs=[pl.BlockSpec((n, LANES), lambda j: (0, j)),
                   pl.BlockSpec((2, n_seq, LANES), lambda j: (0, 0, j))],
        out_shape=[jax.ShapeDtypeStruct((n, d), BF16), jax.ShapeDtypeStruct((2, n_seq, d), F32)],
        scratch_shapes=([pltpu.VMEM((ns * (seg + SCAN_PAD), LANES), F32)] * 4
                        + [pltpu.VMEM((6, ns, LANES), F32), pltpu.VMEM((n + 2 * SUBLANES, LANES), F32)]),
        compiler_params=_params("arbitrary"),
        name="rglru_scan",
    )(xg, xg, conv_w, conv_b, w4, b4, lam, h0)


def _lru_out_kernel(y_ref, x_ref, mod_ref, w_ref, g_ref, b_ref, o_ref, *, alpha):
    for j in range(x_ref.shape[0] // TOKEN_TILE):
        rows = slice(j * TOKEN_TILE, (j + 1) * TOKEN_TILE)
        out = jnp.dot(y_ref[rows, :], w_ref[...], preferred_element_type=F32)
        o_ref[rows, :] = _layer_norm(alpha * x_ref[rows, :] + _gate(mod_ref, 1) * out, g_ref[...], b_ref[...])


def _lru_out(y, x, mod, w_out, ln_g, ln_b, *, alpha, seq_len):
    n, d = x.shape
    tm = LRU_OUT_SUBTILES * TOKEN_TILE
    row = pl.BlockSpec((tm, d), lambda i: (i, 0))
    return pl.pallas_call(
        functools.partial(_lru_out_kernel, alpha=alpha),
        grid=(n // tm,),
        in_specs=[row, row, _mod_spec(mod.shape[0], d, tm, seq_len), _const_spec(w_out.shape),
                  _const_spec((1, d)), _const_spec((1, d))],
        out_specs=row,
        out_shape=jax.ShapeDtypeStruct((n, d), F32),
        compiler_params=_params("arbitrary"),
        name="rglru_out_proj",
    )(y, x, mod, w_out, ln_g.reshape(1, d), ln_b.reshape(1, d))


def _lru_stripe_weights(conv_w, conv_b, w_a, b_a, w_x, b_x, lam):
    d = conv_w.shape[1]
    nt = d // LANES
    per = LANES // LRU_BLOCK

    def blockdiag(w):
        wb = w.reshape(nt, per, LRU_BLOCK, LRU_BLOCK)
        eye = jnp.eye(per, dtype=w.dtype)
        return jnp.einsum('tpij,pq->tpiqj', wb, eye).reshape(nt, LANES, LANES)

    w4 = jnp.concatenate([blockdiag(w_a[0]), blockdiag(w_x[0]), blockdiag(w_a[1]), blockdiag(w_x[1])], axis=2)
    b4 = jnp.concatenate([v.reshape(nt, 1, LANES) for v in (b_a[0], b_x[0], b_a[1], b_x[1])], axis=2)
    cw = conv_w.reshape(CONV_W, nt, LANES).transpose(1, 0, 2)
    cb = conv_b.reshape(nt, 1, LANES)
    lam_t = lam.reshape(2, nt, LANES).transpose(1, 0, 2)
    return cw, cb, w4.astype(BF16), b4, lam_t


def _lru_mixer(x, mod, w_in, stripe_w, w_out, ln_g, ln_b, init, *, alpha, n_seq, seq_len):
    n, d = x.shape
    xg = _inproj(x, mod, w_in, sub=1, seq_len=seq_len)
    h0 = jnp.zeros((2, n_seq, d), F32) if init is None else init.transpose(1, 0, 2)
    y, fin = _lru_scan(xg, *stripe_w, h0, n_seq=n_seq, seq_len=seq_len)
    x_new = _lru_out(y, x, mod, w_out, ln_g, ln_b, alpha=alpha, seq_len=seq_len)
    return x_new, fin.transpose(1, 0, 2)


def _grid_pos_embed(rows, d):
    t = jnp.arange(rows * GRID_W)
    row = (t // GRID_W).astype(F32)
    col = (t % GRID_W).astype(F32)
    quarter = d // 4
    omega = 1.0 / (10000.0 ** (jnp.arange(quarter, dtype=F32) / quarter))

    def emb(p):
        ang = p[:, None] * omega[None, :]
        return jnp.concatenate([jnp.sin(ang), jnp.cos(ang)], -1)

    return jnp.concatenate([emb(row), emb(col)], -1)


def _trunk(x3, mod, init_s5, init_lru, wts, pos):
    n_seq, seq_len, d = x3.shape
    depth = mod.shape[0]
    alpha = (2.0 * depth) ** 0.25
    x = x3.reshape(n_seq * seq_len, d)
    st_s5, st_lru = [], []
    for i in range(depth):
        j = i // 2
        m = mod[i]
        x = _ffn(x, m, wts["up"], wts["dn"], wts["ln_g"][i, 0], wts["ln_b"][i, 0],
                 layer=i, sub=0, alpha=alpha, seq_len=seq_len, pos=pos if i == 0 else None,
                 row_order="to_phase_major" if i % 2 == 0 else None)
        if i % 2 == 0:
            init = None if init_s5 is None else init_s5[:, j]
            x, st = _s5_mixer(x, m, wts["s5_w_in"][j], wts["s5_ops"][j], wts["s5_d"][j], wts["s5_w_glu"][j],
                              wts["ln_g"][i, 1], wts["ln_b"][i, 1], init,
                              alpha=alpha, n_seq=n_seq, seq_len=seq_len)
            st_s5.append(st)
        else:
            init = None if init_lru is None else init_lru[:, j]
            x, st = _lru_mixer(x, m, wts["lru_w_in"][j], wts["lru_stripe"][j], wts["lru_w_out"][j],
                               wts["ln_g"][i, 1], wts["ln_b"][i, 1], init,
                               alpha=alpha, n_seq=n_seq, seq_len=seq_len)
            st_lru.append(st)
        x = _ffn(x, m, wts["up"], wts["dn"], wts["ln_g"][i, 2], wts["ln_b"][i, 2],
                 layer=i, sub=2, alpha=alpha, seq_len=seq_len, row_order="to_natural" if i % 2 == 0 else None)
    return x.reshape(n_seq, seq_len, d), st_s5, st_lru


def kernel(x_prompt, x_sample, c, state_s5, state_lru, c_ctx, ada_w, ada_b, ln_g, ln_b, ffn_w_up, ffn_w_down, s5_w_in, s5_a_re, s5_a_im, s5_log_dt, s5_b_re, s5_b_im, s5_c_re, s5_c_im, s5_d, s5_w_glu, lru_w_in, lru_conv_w, lru_conv_b, lru_w_a, lru_b_a, lru_w_x, lru_b_x, lru_lambda, lru_w_out):
    depth, d, _ = ada_w.shape
    n_dec = c.shape[0]
    assert 1 + n_dec <= SUBLANES
    cond8 = jnp.concatenate([c_ctx[None, :], c, jnp.zeros((SUBLANES - 1 - n_dec, d), F32)], axis=0)
    mod = _modulation(cond8, ada_w, ada_b).reshape(depth, SUBLANES, 9, d)
    mod_ctx, mod_dec = mod[:, 0:1], mod[:, 1:1 + n_dec]

    wts = {
        "up": ffn_w_up, "dn": ffn_w_down,
        "ln_g": ln_g, "ln_b": ln_b,
        "s5_w_in": s5_w_in.astype(BF16), "s5_d": s5_d, "s5_w_glu": s5_w_glu.astype(BF16),
        "s5_ops": [_s5_operators(s5_a_re[j], s5_a_im[j], s5_log_dt[j], s5_b_re[j], s5_b_im[j],
                                 s5_c_re[j], s5_c_im[j]) for j in range(s5_w_in.shape[0])],
        "lru_w_in": lru_w_in.astype(BF16), "lru_w_out": lru_w_out.astype(BF16),
        "lru_stripe": [_lru_stripe_weights(lru_conv_w[j], lru_conv_b[j], lru_w_a[j], lru_b_a[j], lru_w_x[j],
                                           lru_b_x[j], lru_lambda[j]) for j in range(lru_w_in.shape[0])],
    }

    y_prompt, st_s5, st_lru = _trunk(x_prompt, mod_ctx, None, None, wts, None)
    new_state_s5 = jnp.stack(st_s5, 1).astype(x_prompt.dtype)
    new_state_lru = jnp.stack(st_lru, 1).astype(x_prompt.dtype)

    pos = _grid_pos_embed(x_sample.shape[1] // GRID_W, d).astype(x_sample.dtype)
    y_sample, _, _ = _trunk(x_sample, mod_dec, state_s5, state_lru, wts, pos)
    return (y_prompt, y_sample, new_state_s5, new_state_lru)
```

```python
import functools

import jax
import jax.numpy as jnp
from jax import lax
from jax.experimental import pallas as pl
from jax.experimental.pallas import tpu as pltpu

F32 = jnp.float32
BF16 = jnp.bfloat16

GRID_W = 64
S5_GROUP = 16
S5_STATE = 64
LRU_BLOCK = 64
LRU_C = 8.0
CONV_W = 4
CONV_LEFT = 2
LN_EPS = 1e-5
LOG2_E = 1.4426950408889634

LANES = 128
SUBLANES = 8
VMEM_LIMIT_BYTES = 56 * 1024 * 1024

S5_CHUNK = 16
S5_CW = S5_CHUNK * S5_GROUP
S5_SW = 4 * S5_STATE
S5_GROUP_BLOCK = 8
PERM_PITCH = S5_CHUNK + 4
S5_OPS_GROUP_BLOCK = 4
SCAN_PAD = 4
LRU_OUT_SUBTILES = 2
TOKEN_TILE = 512
FFN_SUBTILES = 2
FFN_CHUNK = 256
S5_PROJ_SUBTILES = 2
LRU_SEGMENT = 256


def _params(*sem):
    return pltpu.CompilerParams(dimension_semantics=sem, vmem_limit_bytes=VMEM_LIMIT_BYTES)


def _const_spec(shape, lead=()):
    idx = tuple(lead) + (0,) * len(shape)
    return pl.BlockSpec((None,) * len(lead) + tuple(shape), lambda *_: idx, pipeline_mode=pl.Buffered(1))


def _layer_norm(z, g, b):
    mu = jnp.mean(z, axis=-1, keepdims=True)
    zc = z - mu
    var = jnp.mean(zc * zc, axis=-1, keepdims=True)
    return zc * lax.rsqrt(var + LN_EPS) * g + b


def _sigmoid(x):
    return 0.5 * jnp.tanh(0.5 * x) + 0.5


def _gelu(x):
    c = 0.7978845608028654
    h = 0.5 * x
    return h + h * jnp.tanh(x * ((c * 0.044715) * (x * x) + c))


def _modulate(x, mod_ref, sub):
    sh = mod_ref[0, 3 * sub:3 * sub + 1, :]
    sc = mod_ref[0, 3 * sub + 1:3 * sub + 2, :]
    return x * (1.0 + sc) + sh


def _gate(mod_ref, sub):
    return mod_ref[0, 3 * sub + 2:3 * sub + 3, :]


def _mod_spec(n_cond, d, tm, seq_len):
    if n_cond == 1:
        return pl.BlockSpec((1, 9, d), lambda i: (0, 0, 0))
    return pl.BlockSpec((1, 9, d), lambda i: ((i * tm) // seq_len, 0, 0))


def _mod_kernel(c_ref, w_ref, b_ref, o_ref):
    s = jax.nn.silu(c_ref[...]).astype(BF16)
    o_ref[0] = jnp.dot(s, w_ref[0].astype(BF16), preferred_element_type=F32) + b_ref[0]


def _modulation(cond8, ada_w, ada_b):
    depth, d, n9 = ada_w.shape
    tn = n9 // 4
    return pl.pallas_call(
        _mod_kernel,
        grid=(depth, n9 // tn),
        in_specs=[pl.BlockSpec((SUBLANES, d), lambda l, n: (0, 0)),
                  pl.BlockSpec((1, d, tn), lambda l, n: (l, 0, n)),
                  pl.BlockSpec((1, 1, tn), lambda l, n: (l, 0, n))],
        out_specs=pl.BlockSpec((1, SUBLANES, tn), lambda l, n: (l, 0, n)),
        out_shape=jax.ShapeDtypeStruct((depth, SUBLANES, n9), F32),
        compiler_params=_params("arbitrary", "arbitrary"),
        name="adaln_modulation",
    )(cond8, ada_w, ada_b.reshape(depth, 1, n9))


def _store_rows_permuted(res, o_ref, perm_ref, *, to_phase_major):
    tm, d = res.shape
    nch = tm // S5_CHUNK
    for v in range(d // LANES):
        lanes = slice(v * LANES, (v + 1) * LANES)
        if to_phase_major:
            for c in range(nch):
                perm_ref[v, c * PERM_PITCH:c * PERM_PITCH + S5_CHUNK, :] = res[c * S5_CHUNK:(c + 1) * S5_CHUNK, lanes]
            for t in range(S5_CHUNK):
                o_ref[t * nch:(t + 1) * nch, lanes] = perm_ref[v, pl.ds(t, nch, stride=PERM_PITCH), :]
        else:
            for t in range(S5_CHUNK):
                perm_ref[v, pl.ds(t, nch, stride=PERM_PITCH), :] = res[t * nch:(t + 1) * nch, lanes]
            for c in range(nch):
                o_ref[c * S5_CHUNK:(c + 1) * S5_CHUNK, lanes] = perm_ref[v, c * PERM_PITCH:c * PERM_PITCH + S5_CHUNK, :]


def _ffn_kernel(*refs, sub, alpha, d_ff, ck, has_pos, row_order):
    if has_pos:
        x_ref, pos_ref, mod_ref, wup_ref, wdn_ref, g_ref, b_ref, o_ref, act_ref, perm_ref = refs
    else:
        x_ref, mod_ref, wup_ref, wdn_ref, g_ref, b_ref, o_ref, act_ref, perm_ref = refs
    slabs = perm_ref.shape[0] * TOKEN_TILE // x_ref.shape[0]
    for j in range(x_ref.shape[0] // TOKEN_TILE):
        rows = slice(j * TOKEN_TILE, (j + 1) * TOKEN_TILE)
        x = x_ref[rows, :] + pos_ref[rows, :] if has_pos else x_ref[rows, :]
        h = _modulate(x, mod_ref, sub).astype(BF16)
        for c0 in range(0, d_ff, ck):
            c1 = min(c0 + ck, d_ff)
            a = jnp.dot(h, wup_ref[:, c0:c1], preferred_element_type=F32)
            g = jnp.dot(h, wup_ref[:, d_ff + c0:d_ff + c1], preferred_element_type=F32)
            act_ref[rows, c0:c1] = (jax.nn.silu(g) * a).astype(BF16)
        f = jnp.dot(act_ref[rows, :], wdn_ref[...], preferred_element_type=F32)
        z = alpha * x + (0.5 * _gate(mod_ref, sub)) * f
        res = _layer_norm(z, g_ref[...], b_ref[...])
        if row_order is None:
            o_ref[rows, :] = res
        else:
            _store_rows_permuted(res, o_ref.at[rows], perm_ref.at[j * slabs:(j + 1) * slabs],
                                 to_phase_major=row_order == "to_phase_major")


def _ffn(x, mod, w_up, w_dn, ln_g, ln_b, *, layer, sub, alpha, seq_len, pos=None, row_order=None):
    n, d = x.shape
    d_ff = w_dn.shape[2]
    which = (layer, sub // 2)
    tm = min(FFN_SUBTILES * TOKEN_TILE, n)
    ck = FFN_CHUNK
    row = pl.BlockSpec((tm, d), lambda i: (i, 0))
    in_specs = [row]
    args = [x]
    if pos is not None:
        per_seq = seq_len // tm
        in_specs.append(pl.BlockSpec((tm, d), lambda i: (i % per_seq, 0)))
        args.append(pos)
    in_specs += [_mod_spec(mod.shape[0], d, tm, seq_len), _const_spec(w_up.shape[2:], which),
                 _const_spec(w_dn.shape[2:], which), _const_spec((1, d)), _const_spec((1, d))]
    args += [mod, w_up, w_dn, ln_g.reshape(1, d), ln_b.reshape(1, d)]
    perm_rows = TOKEN_TILE // S5_CHUNK * PERM_PITCH
    perm_shape = (tm // TOKEN_TILE * (d // LANES), perm_rows, LANES) if row_order else (1, SUBLANES, LANES)
    return pl.pallas_call(
        functools.partial(_ffn_kernel, sub=sub, alpha=alpha, d_ff=d_ff, ck=ck, has_pos=pos is not None,
                          row_order=row_order),
        grid=(n // tm,),
        in_specs=in_specs,
        out_specs=row,
        out_shape=jax.ShapeDtypeStruct((n, d), F32),
        scratch_shapes=[pltpu.VMEM((tm, d_ff), BF16), pltpu.VMEM(perm_shape, F32)],
        compiler_params=_params("arbitrary"),
        name="ffn_sublayer",
    )(*args)


def _inproj_kernel(x_ref, mod_ref, w_ref, o_ref, *, sub):
    h = _modulate(x_ref[...], mod_ref, sub).astype(BF16)
    o_ref[...] = jnp.dot(h, w_ref[...], preferred_element_type=F32)


def _inproj(x, mod, w, *, sub, seq_len):
    n, d = x.shape
    dn = w.shape[1]
    tm = min(TOKEN_TILE, n)
    return pl.pallas_call(
        functools.partial(_inproj_kernel, sub=sub),
        grid=(n // tm,),
        in_specs=[pl.BlockSpec((tm, d), lambda i: (i, 0)), _mod_spec(mod.shape[0], d, tm, seq_len),
                  _const_spec(w.shape)],
        out_specs=pl.BlockSpec((tm, dn), lambda i: (i, 0)),
        out_shape=jax.ShapeDtypeStruct((n, dn), F32),
        compiler_params=_params("arbitrary"),
        name="mixer_in_proj",
    )(x, mod, w)


def _s5_discretize(are, aim, ldt):
    dt = jnp.exp(ldt)
    mag = jnp.exp(dt * are)
    abr = mag * jnp.cos(dt * aim)
    abi = mag * jnp.sin(dt * aim)
    den = are * are + aim * aim
    nr, ni = abr - 1.0, abi
    qr = (nr * are + ni * aim) / den
    qi = (ni * are - nr * aim) / den
    return abr, abi, qr, qi


def _split_bf16(x):
    hi = x.astype(BF16)
    return hi, (x - hi.astype(F32)).astype(BF16)


def _s5_ops_kernel(arow_ref, btr_ref, bti_ref, ctr_ref, cti_ref, m_ref, mst_ref, min_ref, a16_ref):
    for g in range(arow_ref.shape[0]):
        m_ref[g], mst_ref[g], min_ref[g], a16_ref[g] = _s5_group_operators(
            arow_ref[g], btr_ref[g], bti_ref[g], ctr_ref[g], cti_ref[g])


def _s5_group_operators(arow, btr, bti, ctr, cti):
    t_chunk, grp, w = S5_CHUNK, S5_GROUP, S5_CW
    colblk = lax.broadcasted_iota(jnp.int32, (w, w), 1) // grp
    pow_row = lax.broadcasted_iota(jnp.int32, (t_chunk, w), 0)
    pow_blk = lax.broadcasted_iota(jnp.int32, (t_chunk, w), 1) // grp
    tn_dims = (((0,), (0,)), ((), ()))

    abr2, abi2, qr2, qi2 = _s5_discretize(arow[0:2], arow[2:4], arow[4:6])
    lag_tabs, decay, carry_in = [], [], []
    for d in range(2):
        abr, abi, qr, qi = abr2[d:d + 1], abi2[d:d + 1], qr2[d:d + 1], qi2[d:d + 1]
        lr = [qr * btr - qi * bti]
        li = [qr * bti + qi * btr]
        pr, pi = [abr], [abi]
        for _ in range(t_chunk - 1):
            lr.append(lr[-1] * abr - li[-1] * abi)
            li.append(lr[-2] * abi + li[-1] * abr)
            pr.append(pr[-1] * abr - pi[-1] * abi)
            pi.append(pr[-2] * abi + pi[-1] * abr)
        decay.append((pr[-1], pi[-1]))
        if d == 0:
            lr, li = lr[::-1], li[::-1]
        lag_tabs.append((jnp.concatenate(lr, axis=0), jnp.concatenate(li, axis=0)))

        place = (pow_blk == (pow_row if d == 0 else t_chunk - 1 - pow_row)).astype(BF16)
        e_t = []
        for tbl in (jnp.concatenate(pr, axis=0), jnp.concatenate(pi, axis=0)):
            hi, lo = _split_bf16(tbl)
            e_t.append(lax.dot_general(hi, place, tn_dims, preferred_element_type=F32)
                       + lax.dot_general(lo, place, tn_dims, preferred_element_type=F32))
        er_t, ei_t = e_t
        carry_in.append((ctr * er_t - cti * ei_t, ctr * ei_t + cti * er_t))

    lhs = jnp.concatenate([jnp.concatenate([l_re, -l_im], axis=1) for l_re, l_im in lag_tabs], axis=0)
    l_hi, l_lo = _split_bf16(lhs)
    c_hi, c_lo = _split_bf16(jnp.concatenate([ctr, cti], axis=0))
    tab = (jnp.dot(l_hi, c_hi, preferred_element_type=F32) + jnp.dot(l_lo, c_hi, preferred_element_type=F32)
           + jnp.dot(l_hi, c_lo, preferred_element_type=F32))
    tab_f, tab_b = tab[:w], tab[w:]
    acc = jnp.zeros((w, w), F32)
    for tp in range(t_chunk):
        cut_f = (t_chunk - 1 - tp) * grp
        cut_b = tp * grp
        sh_f = tab_f if cut_f == 0 else jnp.concatenate([tab_f[cut_f:], jnp.zeros((cut_f, w), F32)], 0)
        sh_b = tab_b if cut_b == 0 else jnp.concatenate([jnp.zeros((cut_b, w), F32), tab_b[:w - cut_b]], 0)
        acc = jnp.where(colblk == tp, sh_f + sh_b, acc)
    (fr, fi), (br, bi) = lag_tabs
    (wfr, wfi), (wbr, wbi) = carry_in
    (efr, efi), (ebr, ebi) = decay
    return (acc.astype(BF16),
            jnp.concatenate([fr, br, fi, bi], axis=1).astype(BF16),
            jnp.concatenate([wfr, wbr, -wfi, -wbi], axis=0).astype(BF16),
            jnp.concatenate([efr, ebr, efi, ebi], axis=1))


def _s5_operators(a_re, a_im, log_dt, b_re, b_im, c_re, c_im):
    _, g, p = a_re.shape
    ldt = jnp.broadcast_to(log_dt[:, :, None], (2, g, p))
    arow = jnp.concatenate([a_re, a_im, ldt, jnp.zeros((2, g, p), F32)], axis=0).transpose(1, 0, 2)
    bt_re, bt_im = b_re.transpose(0, 2, 1), b_im.transpose(0, 2, 1)
    ct_re = jnp.tile(c_re.transpose(0, 2, 1), (1, 1, S5_CHUNK))
    ct_im = jnp.tile(c_im.transpose(0, 2, 1), (1, 1, S5_CHUNK))
    gb = S5_OPS_GROUP_BLOCK
    per_g = lambda *s: pl.BlockSpec((gb,) + s, lambda i: (i, 0, 0))
    w = S5_CW
    return pl.pallas_call(
        _s5_ops_kernel,
        grid=(g // gb,),
        in_specs=[per_g(SUBLANES, p), per_g(S5_GROUP, p), per_g(S5_GROUP, p), per_g(p, w), per_g(p, w)],
        out_specs=[per_g(w, w), per_g(w, S5_SW), per_g(S5_SW, w), per_g(1, S5_SW)],
        out_shape=[jax.ShapeDtypeStruct((g, w, w), BF16), jax.ShapeDtypeStruct((g, w, S5_SW), BF16),
                   jax.ShapeDtypeStruct((g, S5_SW, w), BF16), jax.ShapeDtypeStruct((g, 1, S5_SW), F32)],
        compiler_params=_params("arbitrary"),
        name="s5_operators",
    )(arow, bt_re, bt_im, ct_re, ct_im)


def _lane_block_transpose(vs):
    blk = lax.broadcasted_iota(jnp.int32, vs[0].shape, 1) // S5_GROUP
    for dist in (4, 2, 1):
        keep = (blk & dist) == 0
        out = list(vs)
        for r in range(len(vs)):
            if r & dist:
                continue
            lo, hi = vs[r], vs[r + dist]
            out[r] = jnp.where(keep, lo, pltpu.roll(hi, dist * S5_GROUP, axis=1))
            out[r + dist] = jnp.where(keep, pltpu.roll(lo, LANES - dist * S5_GROUP, axis=1), hi)
        vs = out
    return vs


def _s5_in_kernel(x_ref, mod_ref, w_ref, o_ref):
    nch = TOKEN_TILE // S5_CHUNK
    d = w_ref.shape[0]
    per = LANES // S5_GROUP
    for j in range(x_ref.shape[0] // TOKEN_TILE):
        h = _modulate(x_ref[j * TOKEN_TILE:(j + 1) * TOKEN_TILE, :], mod_ref, 1).astype(BF16)
        u = jnp.dot(h, w_ref[...], preferred_element_type=F32)
        for v in range(d // LANES):
            for th in range(S5_CHUNK // per):
                src = [u[(per * th + r) * nch:(per * th + r + 1) * nch, v * LANES:(v + 1) * LANES]
                       for r in range(per)]
                dst = _lane_block_transpose(src)
                for q in range(per):
                    o_ref[per * v + q, j * nch:(j + 1) * nch, th * LANES:(th + 1) * LANES] = dst[q]


def _s5_in(x, mod, w, *, seq_len):
    n, d = x.shape
    g = d // S5_GROUP
    tm = TOKEN_TILE
    nch = tm // S5_CHUNK
    nc = n // S5_CHUNK
    return pl.pallas_call(
        _s5_in_kernel,
        grid=(n // tm,),
        in_specs=[pl.BlockSpec((tm, d), lambda i: (i, 0)),
                  _mod_spec(mod.shape[0], d, tm, seq_len), _const_spec(w.shape)],
        out_specs=pl.BlockSpec((g, nch, S5_CW), lambda i: (0, i, 0)),
        out_shape=jax.ShapeDtypeStruct((g, nc, S5_CW), F32),
        compiler_params=_params("arbitrary"),
        name="s5_in_proj",
    )(x, mod, w)


def _s5_mix_kernel(u_ref, m_ref, mst_ref, min_ref, a16_ref, dsk_ref, s0_ref, y_ref, fin_ref,
                   loc_ref, sa_ref, sb_ref, *, n_seq, ncs):
    gb = u_ref.shape[0]
    half = S5_SW // 2
    pitch = ncs + SCAN_PAD
    for g in range(gb):
        loc = jnp.dot(u_ref[g].astype(BF16), mst_ref[g], preferred_element_type=F32)
        for s in range(n_seq):
            loc_ref[0, g, s * pitch:s * pitch + ncs, :] = loc[s * ncs:(s + 1) * ncs, :half]
            loc_ref[1, g, s * pitch:s * pitch + ncs, :] = loc[s * ncs:(s + 1) * ncs, half:]
    is_fwd = lax.broadcasted_iota(jnp.int32, (n_seq, half), 1) < S5_STATE
    decay = [(a16_ref[g][:, :half], a16_ref[g][:, half:]) for g in range(gb)]

    def step(k, carry):
        rows_f = pl.ds(k, n_seq, stride=pitch)
        rows_b = pl.ds(ncs - 1 - k, n_seq, stride=pitch)
        out = []
        for g in range(gb):
            sr, si = carry[2 * g], carry[2 * g + 1]
            ar, ai = decay[g]
            sa_ref[0, g, rows_f, :] = sr
            sa_ref[1, g, rows_f, :] = si
            sb_ref[0, g, rows_b, :] = sr
            sb_ref[1, g, rows_b, :] = si
            lr = jnp.where(is_fwd, loc_ref[0, g, rows_f, :], loc_ref[0, g, rows_b, :])
            li = jnp.where(is_fwd, loc_ref[1, g, rows_f, :], loc_ref[1, g, rows_b, :])
            out += [ar * sr - ai * si + lr, ar * si + ai * sr + li]
        return tuple(out)

    init = []
    for g in range(gb):
        init += [s0_ref[g][:, :half], s0_ref[g][:, half:]]
    fin = lax.fori_loop(0, ncs, step, tuple(init))
    sel = lax.broadcasted_iota(jnp.int32, (ncs, half), 1) < S5_STATE
    for g in range(gb):
        fin_ref[g] = jnp.concatenate([fin[2 * g], fin[2 * g + 1]], axis=1)
        parts = []
        for ri in range(2):
            parts.append(jnp.concatenate(
                [jnp.where(sel, sa_ref[ri, g, s * pitch:s * pitch + ncs, :], sb_ref[ri, g, s * pitch:s * pitch + ncs, :])
                 for s in range(n_seq)], axis=0))
        s_in = jnp.concatenate(parts, axis=1).astype(BF16)
        u = u_ref[g]
        y_ref[g] = (jnp.dot(u.astype(BF16), m_ref[g], preferred_element_type=F32)
                    + jnp.dot(s_in, min_ref[g], preferred_element_type=F32)
                    + dsk_ref[g] * u)


def _s5_mix(u_chunks, ops, d_skip, s0, *, n_seq, ncs):
    g, nc, w = u_chunks.shape
    m, mst, m_in, a16 = ops
    gb = S5_GROUP_BLOCK
    dsk = jnp.tile(d_skip.reshape(g, 1, S5_GROUP), (1, 1, S5_CHUNK))
    per_g = lambda *s: pl.BlockSpec((gb,) + s, lambda i: (i, 0, 0))
    rows = n_seq * (ncs + SCAN_PAD)
    return pl.pallas_call(
        functools.partial(_s5_mix_kernel, n_seq=n_seq, ncs=ncs),
        grid=(g // gb,),
        in_specs=[per_g(nc, w), per_g(w, w), per_g(w, S5_SW), per_g(S5_SW, w), per_g(1, S5_SW), per_g(1, w),
                  per_g(n_seq, S5_SW)],
        out_specs=[per_g(nc, w), per_g(n_seq, S5_SW)],
        out_shape=[jax.ShapeDtypeStruct((g, nc, w), F32), jax.ShapeDtypeStruct((g, n_seq, S5_SW), F32)],
        scratch_shapes=[pltpu.VMEM((2, gb, rows, S5_SW // 2), F32)] * 3,
        compiler_params=_params("arbitrary"),
        name="s5_chunk_scan",
    )(u_chunks, m, mst, m_in, a16, dsk, s0)


def _s5_out_kernel(y_ref, x_ref, mod_ref, w_ref, g_ref, b_ref, o_ref, *, alpha):
    nch = TOKEN_TILE // S5_CHUNK
    d = w_ref.shape[0]
    per = LANES // S5_GROUP
    for j in range(x_ref.shape[0] // TOKEN_TILE):
        rows = slice(j * TOKEN_TILE, (j + 1) * TOKEN_TILE)
        phase = [[None] * (d // LANES) for _ in range(S5_CHUNK)]
        for v in range(d // LANES):
            for th in range(S5_CHUNK // per):
                src = [y_ref[per * v + q, j * nch:(j + 1) * nch, th * LANES:(th + 1) * LANES] for q in range(per)]
                dst = _lane_block_transpose(src)
                for r in range(per):
                    phase[per * th + r][v] = dst[r]
        y = jnp.concatenate([jnp.concatenate(p, axis=1) for p in phase], axis=0)
        z = jnp.dot(_gelu(y).astype(BF16), w_ref[...], preferred_element_type=F32)
        out = z[:, :d] * _sigmoid(z[:, d:])
        o_ref[rows, :] = _layer_norm(alpha * x_ref[rows, :] + _gate(mod_ref, 1) * out, g_ref[...], b_ref[...])


def _s5_out(y_chunks, x, mod, w_glu, ln_g, ln_b, *, alpha, seq_len):
    n, d = x.shape
    g = d // S5_GROUP
    tm = S5_PROJ_SUBTILES * TOKEN_TILE
    nch = tm // S5_CHUNK
    row = pl.BlockSpec((tm, d), lambda i: (i, 0))
    return pl.pallas_call(
        functools.partial(_s5_out_kernel, alpha=alpha),
        grid=(n // tm,),
        in_specs=[pl.BlockSpec((g, nch, S5_CW), lambda i: (0, i, 0)), row,
                  _mod_spec(mod.shape[0], d, tm, seq_len), _const_spec(w_glu.shape),
                  _const_spec((1, d)), _const_spec((1, d))],
        out_specs=row,
        out_shape=jax.ShapeDtypeStruct((n, d), F32),
        compiler_params=_params("arbitrary"),
        name="s5_out_proj",
    )(y_chunks, x, mod, w_glu, ln_g.reshape(1, d), ln_b.reshape(1, d))


def _s5_mixer(x, mod, w_in, ops, d_skip, w_glu, ln_g, ln_b, init, *, alpha, n_seq, seq_len):
    n, d = x.shape
    g = d // S5_GROUP
    ncs = seq_len // S5_CHUNK
    u_chunks = _s5_in(x, mod, w_in, seq_len=seq_len)
    if init is None:
        s0 = jnp.zeros((g, n_seq, S5_SW), F32)
    else:
        s0 = init.transpose(3, 0, 2, 1, 4).reshape(g, n_seq, S5_SW)
    y_chunks, fin = _s5_mix(u_chunks, ops, d_skip, s0, n_seq=n_seq, ncs=ncs)
    x_new = _s5_out(y_chunks, x, mod, w_glu, ln_g, ln_b, alpha=alpha, seq_len=seq_len)
    state = fin.reshape(g, n_seq, 2, 2, S5_STATE).transpose(1, 3, 2, 0, 4)
    return x_new, state


def _softplus(x):
    return jnp.maximum(x, 0.0) + jnp.log1p(jnp.exp(-jnp.abs(x)))


def _lru_scan_kernel(xr_ref, gate_ref, cw_ref, cb_ref, w4_ref, b4_ref, lam_ref, h0_ref, y_ref, fin_ref,
                     af_ref, bf_ref, ab_ref, bb_ref, end_ref, xpad_ref, *, n_seq, seq_len, seg):
    n = n_seq * seq_len
    n_seg = seq_len // seg
    ns = n_seq * n_seg
    pitch = seg + SCAN_PAD
    c = LANES
    chained = n_seg > 1
    half_rate = (0.5 * LRU_C) * _softplus(-lam_ref[0])
    cw, cb = cw_ref[0], cb_ref[0]
    w4_half = (0.5 * w4_ref[0].astype(F32)).astype(BF16)
    b4_half = 0.5 * b4_ref[0]

    zero_tile = jnp.zeros((SUBLANES, c), F32)
    xpad_ref[0:SUBLANES, :] = zero_tile
    xpad_ref[SUBLANES + n:2 * SUBLANES + n, :] = zero_tile

    def pad_copy(s, _):
        r0 = pl.multiple_of(s * seg, seg)
        xpad_ref[pl.ds(SUBLANES + r0, seg), :] = xr_ref[pl.ds(r0, seg), :]
        return 0

    lax.fori_loop(0, ns, pad_copy, 0)
    row8 = lax.broadcasted_iota(jnp.int32, (SUBLANES, c), 0)

    def gates(s, _):
        r0 = pl.multiple_of(s * seg, seg)
        p0 = pl.multiple_of(s * pitch, SCAN_PAD)
        seq_start = s % n_seg == 0
        seq_end = s % n_seg == n_seg - 1
        xc = cb
        for k in range(CONV_W):
            off = k - CONV_LEFT
            tap = xpad_ref[pl.ds(SUBLANES + r0 + off, seg), :]
            if off < 0:
                head = jnp.where(seq_start & (row8 < -off), 0.0, tap[:SUBLANES])
                tap = jnp.concatenate([head, tap[SUBLANES:]], axis=0)
            elif off > 0:
                tail = jnp.where(seq_end & (row8 >= SUBLANES - off), 0.0, tap[seg - SUBLANES:])
                tap = jnp.concatenate([tap[:seg - SUBLANES], tail], axis=0)
            xc = xc + tap * cw[k:k + 1]
        half_pre = jnp.dot(xc.astype(BF16), w4_half, preferred_element_type=F32) + b4_half
        xc_half = 0.5 * xc
        for d, (a_ref, b_ref) in enumerate(((af_ref, bf_ref), (ab_ref, bb_ref))):
            tanh_a = jnp.tanh(half_pre[:, 2 * d * c:(2 * d + 1) * c])
            tanh_x = jnp.tanh(half_pre[:, (2 * d + 1) * c:(2 * d + 2) * c])
            neg_log_a = half_rate[d:d + 1] * tanh_a + half_rate[d:d + 1]
            a = jnp.exp2(neg_log_a * -LOG2_E)
            a_ref[pl.ds(p0, seg), :] = a
            v = jnp.tanh(neg_log_a) * (a * a + 1.0)
            root = jnp.where(v > 0.0, v * lax.rsqrt(v), 0.0)
            b_ref[pl.ds(p0, seg), :] = root * (xc_half * tanh_x + xc_half)
        return 0

    lax.fori_loop(0, ns, gates, 0, unroll=2)

    def step(k, carry):
        hf, hb, pf, pb = carry
        rows_f = pl.ds(k, ns, stride=pitch)
        rows_b = pl.ds(seg - 1 - k, ns, stride=pitch)
        af, ab = af_ref[rows_f, :], ab_ref[rows_b, :]
        hf = af * hf + bf_ref[rows_f, :]
        hb = ab * hb + bb_ref[rows_b, :]
        bf_ref[rows_f, :] = hf
        bb_ref[rows_b, :] = hb
        if chained:
            pf, pb = pf * af, pb * ab
            af_ref[rows_f, :] = pf
            ab_ref[rows_b, :] = pb
        return hf, hb, pf, pb

    one = jnp.ones((ns, c), F32)
    if chained:
        init = (jnp.zeros((ns, c), F32), jnp.zeros((ns, c), F32), one, one)
    else:
        init = (h0_ref[0], h0_ref[1], one, one)
    hf, hb, pf, pb = lax.fori_loop(0, seg, step, init, unroll=2)

    if chained:
        end_ref[0], end_ref[1], end_ref[2], end_ref[3] = hf, hb, pf, pb
        hin_f, hin_b = h0_ref[0], h0_ref[1]
        for j in range(n_seg):
            jf, jb = pl.ds(j, n_seq, stride=n_seg), pl.ds(n_seg - 1 - j, n_seq, stride=n_seg)
            end_ref[4, jf, :] = hin_f
            end_ref[5, jb, :] = hin_b
            hin_f = end_ref[2, jf, :] * hin_f + end_ref[0, jf, :]
            hin_b = end_ref[3, jb, :] * hin_b + end_ref[1, jb, :]
        fin_ref[0] = hin_f
        fin_ref[1] = hin_b
    else:
        fin_ref[0] = hf
        fin_ref[1] = hb

    def combine(s, _):
        rows = pl.ds(pl.multiple_of(s * seg, seg), seg)
        prow = pl.ds(pl.multiple_of(s * pitch, SCAN_PAD), seg)
        h = bf_ref[prow, :] + bb_ref[prow, :]
        if chained:
            h = h + af_ref[prow, :] * end_ref[4, pl.ds(s, 1), :] + ab_ref[prow, :] * end_ref[5, pl.ds(s, 1), :]
        y_ref[rows, :] = (h * _gelu(gate_ref[rows, :])).astype(BF16)
        return 0

    lax.fori_loop(0, ns, combine, 0)


def _lru_scan(xg, conv_w, conv_b, w4, b4, lam, h0, *, n_seq, seq_len):
    n, d2 = xg.shape
    d = d2 // 2
    nt = d // LANES
    seg = min(LRU_SEGMENT, seq_len)
    ns = n // seg
    stripe = lambda off: pl.BlockSpec((n, LANES), lambda j: (0, j + off))
    per_t = lambda *s: pl.BlockSpec((1,) + s, lambda j: (j, 0, 0))
    return pl.pallas_call(
        functools.partial(_lru_scan_kernel, n_seq=n_seq, seq_len=seq_len, seg=seg),
        grid=(nt,),
        in_specs=[stripe(0), stripe(nt), per_t(CONV_W, LANES), per_t(1, LANES), per_t(LANES, 4 * LANES),
                  per_t(1, 4 * LANES), per_t(2, LANES), pl.BlockSpec((2, n_seq, LANES), lambda j: (0, 0, j))],
        out_specs=[pl.BlockSpec((n, LANES), lambda j: (0, j)),
                   pl.BlockSpec((2, n_seq, LANES), lambda j: (0, 0, j))],
        out_shape=[jax.ShapeDtypeStruct((n, d), BF16), jax.ShapeDtypeStruct((2, n_seq, d), F32)],
        scratch_shapes=([pltpu.VMEM((ns * (seg + SCAN_PAD), LANES), F32)] * 4
                        + [pltpu.VMEM((6, ns, LANES), F32), pltpu.VMEM((n + 2 * SUBLANES, LANES), F32)]),
        compiler_params=_params("arbitrary"),
        name="rglru_scan",
    )(xg, xg, conv_w, conv_b, w4, b4, lam, h0)


def _lru_out_kernel(y_ref, x_ref, mod_ref, w_ref, g_ref, b_ref, o_ref, *, alpha):
    for j in range(x_ref.shape[0] // TOKEN_TILE):
        rows = slice(j * TOKEN_TILE, (j + 1) * TOKEN_TILE)
        out = jnp.dot(y_ref[rows, :], w_ref[...], preferred_element_type=F32)
        o_ref[rows, :] = _layer_norm(alpha * x_ref[rows, :] + _gate(mod_ref, 1) * out, g_ref[...], b_ref[...])


def _lru_out(y, x, mod, w_out, ln_g, ln_b, *, alpha, seq_len):
    n, d = x.shape
    tm = LRU_OUT_SUBTILES * TOKEN_TILE
    row = pl.BlockSpec((tm, d), lambda i: (i, 0))
    return pl.pallas_call(
        functools.partial(_lru_out_kernel, alpha=alpha),
        grid=(n // tm,),
        in_specs=[row, row, _mod_spec(mod.shape[0], d, tm, seq_len), _const_spec(w_out.shape),
                  _const_spec((1, d)), _const_spec((1, d))],
        out_specs=row,
        out_shape=jax.ShapeDtypeStruct((n, d), F32),
        compiler_params=_params("arbitrary"),
        name="rglru_out_proj",
    )(y, x, mod, w_out, ln_g.reshape(1, d), ln_b.reshape(1, d))


def _lru_stripe_weights(conv_w, conv_b, w_a, b_a, w_x, b_x, lam):
    d = conv_w.shape[1]
    nt = d // LANES
    per = LANES // LRU_BLOCK

    def blockdiag(w):
        wb = w.reshape(nt, per, LRU_BLOCK, LRU_BLOCK)
        eye = jnp.eye(per, dtype=w.dtype)
        return jnp.einsum('tpij,pq->tpiqj', wb, eye).reshape(nt, LANES, LANES)

    w4 = jnp.concatenate([blockdiag(w_a[0]), blockdiag(w_x[0]), blockdiag(w_a[1]), blockdiag(w_x[1])], axis=2)
    b4 = jnp.concatenate([v.reshape(nt, 1, LANES) for v in (b_a[0], b_x[0], b_a[1], b_x[1])], axis=2)
    cw = conv_w.reshape(CONV_W, nt, LANES).transpose(1, 0, 2)
    cb = conv_b.reshape(nt, 1, LANES)
    lam_t = lam.reshape(2, nt, LANES).transpose(1, 0, 2)
    return cw, cb, w4.astype(BF16), b4, lam_t


def _lru_mixer(x, mod, w_in, stripe_w, w_out, ln_g, ln_b, init, *, alpha, n_seq, seq_len):
    n, d = x.shape
    xg = _inproj(x, mod, w_in, sub=1, seq_len=seq_len)
    h0 = jnp.zeros((2, n_seq, d), F32) if init is None else init.transpose(1, 0, 2)
    y, fin = _lru_scan(xg, *stripe_w, h0, n_seq=n_seq, seq_len=seq_len)
    x_new = _lru_out(y, x, mod, w_out, ln_g, ln_b, alpha=alpha, seq_len=seq_len)
    return x_new, fin.transpose(1, 0, 2)


def _grid_pos_embed(rows, d):
    quarter = d // 4
    omega = 1.0 / (10000.0 ** (jnp.arange(quarter, dtype=F32) / quarter))

    def emb(p):
        ang = p[:, None] * omega[None, :]
        return jnp.concatenate([jnp.sin(ang), jnp.cos(ang)], -1)

    by_row = emb(jnp.arange(rows, dtype=F32))[:, None, :]
    by_col = emb(jnp.arange(GRID_W, dtype=F32))[None, :, :]
    half = (rows, GRID_W, d // 2)
    table = jnp.concatenate([jnp.broadcast_to(by_row, half), jnp.broadcast_to(by_col, half)], -1)
    return table.reshape(rows * GRID_W, d)


def _trunk(x3, mod, init_s5, init_lru, wts, pos):
    n_seq, seq_len, d = x3.shape
    depth = mod.shape[0]
    alpha = (2.0 * depth) ** 0.25
    x = x3.reshape(n_seq * seq_len, d)
    st_s5, st_lru = [], []
    for i in range(depth):
        j = i // 2
        m = mod[i]
        x = _ffn(x, m, wts["up"], wts["dn"], wts["ln_g"][i, 0], wts["ln_b"][i, 0],
                 layer=i, sub=0, alpha=alpha, seq_len=seq_len, pos=pos if i == 0 else None,
                 row_order="to_phase_major" if i % 2 == 0 else None)
        if i % 2 == 0:
            init = None if init_s5 is None else init_s5[:, j]
            x, st = _s5_mixer(x, m, wts["s5_w_in"][j], wts["s5_ops"][j], wts["s5_d"][j], wts["s5_w_glu"][j],
                              wts["ln_g"][i, 1], wts["ln_b"][i, 1], init,
                              alpha=alpha, n_seq=n_seq, seq_len=seq_len)
            st_s5.append(st)
        else:
            init = None if init_lru is None else init_lru[:, j]
            x, st = _lru_mixer(x, m, wts["lru_w_in"][j], wts["lru_stripe"][j], wts["lru_w_out"][j],
                               wts["ln_g"][i, 1], wts["ln_b"][i, 1], init,
                               alpha=alpha, n_seq=n_seq, seq_len=seq_len)
            st_lru.append(st)
        x = _ffn(x, m, wts["up"], wts["dn"], wts["ln_g"][i, 2], wts["ln_b"][i, 2],
                 layer=i, sub=2, alpha=alpha, seq_len=seq_len, row_order="to_natural" if i % 2 == 0 else None)
    return x.reshape(n_seq, seq_len, d), st_s5, st_lru


def kernel(x_prompt, x_sample, c, state_s5, state_lru, c_ctx, ada_w, ada_b, ln_g, ln_b, ffn_w_up, ffn_w_down, s5_w_in, s5_a_re, s5_a_im, s5_log_dt, s5_b_re, s5_b_im, s5_c_re, s5_c_im, s5_d, s5_w_glu, lru_w_in, lru_conv_w, lru_conv_b, lru_w_a, lru_b_a, lru_w_x, lru_b_x, lru_lambda, lru_w_out):
    depth, d, _ = ada_w.shape
    n_dec = c.shape[0]
    assert 1 + n_dec <= SUBLANES
    cond8 = jnp.concatenate([c_ctx[None, :], c, jnp.zeros((SUBLANES - 1 - n_dec, d), F32)], axis=0)
    mod = _modulation(cond8, ada_w, ada_b).reshape(depth, SUBLANES, 9, d)
    mod_ctx, mod_dec = mod[:, 0:1], mod[:, 1:1 + n_dec]

    wts = {
        "up": ffn_w_up.astype(BF16), "dn": ffn_w_down.astype(BF16),
        "ln_g": ln_g, "ln_b": ln_b,
        "s5_w_in": s5_w_in.astype(BF16), "s5_d": s5_d, "s5_w_glu": s5_w_glu.astype(BF16),
        "s5_ops": [_s5_operators(s5_a_re[j], s5_a_im[j], s5_log_dt[j], s5_b_re[j], s5_b_im[j],
                                 s5_c_re[j], s5_c_im[j]) for j in range(s5_w_in.shape[0])],
        "lru_w_in": lru_w_in.astype(BF16), "lru_w_out": lru_w_out.astype(BF16),
        "lru_stripe": [_lru_stripe_weights(lru_conv_w[j], lru_conv_b[j], lru_w_a[j], lru_b_a[j], lru_w_x[j],
                                           lru_b_x[j], lru_lambda[j]) for j in range(lru_w_in.shape[0])],
    }

    y_prompt, st_s5, st_lru = _trunk(x_prompt, mod_ctx, None, None, wts, None)
    new_state_s5 = jnp.stack(st_s5, 1).astype(x_prompt.dtype)
    new_state_lru = jnp.stack(st_lru, 1).astype(x_prompt.dtype)

    pos = _grid_pos_embed(x_sample.shape[1] // GRID_W, d).astype(x_sample.dtype)
    y_sample, _, _ = _trunk(x_sample, mod_dec, state_s5, state_lru, wts, pos)
    return (y_prompt, y_sample, new_state_s5, new_state_lru)
```

```python
import functools

import jax
import jax.numpy as jnp
from jax import lax
from jax.experimental import pallas as pl
from jax.experimental.pallas import tpu as pltpu

F32 = jnp.float32
BF16 = jnp.bfloat16

GRID_W = 64
S5_GROUP = 16
S5_STATE = 64
LRU_BLOCK = 64
LRU_C = 8.0
CONV_W = 4
CONV_LEFT = 2
LN_EPS = 1e-5
LOG2_E = 1.4426950408889634

LANES = 128
SUBLANES = 8
VMEM_LIMIT_BYTES = 56 * 1024 * 1024

S5_CHUNK = 16
S5_CW = S5_CHUNK * S5_GROUP
S5_SW = 4 * S5_STATE
S5_GROUP_BLOCK = 8
PERM_PITCH = S5_CHUNK + 4
S5_OPS_GROUP_BLOCK = 4
SCAN_PAD = 4
LRU_OUT_SUBTILES = 2
TOKEN_TILE = 512
FFN_SUBTILES = 2
FFN_CHUNK = 256
S5_PROJ_SUBTILES = 2
LRU_SEGMENT = 256


def _params(*sem):
    return pltpu.CompilerParams(dimension_semantics=sem, vmem_limit_bytes=VMEM_LIMIT_BYTES)


def _const_spec(shape, lead=()):
    idx = tuple(lead) + (0,) * len(shape)
    return pl.BlockSpec((None,) * len(lead) + tuple(shape), lambda *_: idx, pipeline_mode=pl.Buffered(1))


def _layer_norm(z, g, b):
    mu = jnp.mean(z, axis=-1, keepdims=True)
    zc = z - mu
    var = jnp.mean(zc * zc, axis=-1, keepdims=True)
    return zc * lax.rsqrt(var + LN_EPS) * g + b


def _sigmoid(x):
    return 0.5 * jnp.tanh(0.5 * x) + 0.5


def _gelu(x):
    c = 0.7978845608028654
    h = 0.5 * x
    return h + h * jnp.tanh(x * ((c * 0.044715) * (x * x) + c))


def _modulate(x, mod_ref, sub):
    sh = mod_ref[0, 3 * sub:3 * sub + 1, :]
    sc = mod_ref[0, 3 * sub + 1:3 * sub + 2, :]
    return x * (1.0 + sc) + sh


def _gate(mod_ref, sub):
    return mod_ref[0, 3 * sub + 2:3 * sub + 3, :]


def _mod_spec(n_cond, d, tm, seq_len):
    if n_cond == 1:
        return pl.BlockSpec((1, 9, d), lambda i: (0, 0, 0))
    return pl.BlockSpec((1, 9, d), lambda i: ((i * tm) // seq_len, 0, 0))


def _mod_kernel(c_ref, w_ref, b_ref, o_ref):
    s = jax.nn.silu(c_ref[...]).astype(BF16)
    o_ref[0] = jnp.dot(s, w_ref[0].astype(BF16), preferred_element_type=F32) + b_ref[0]


def _modulation(cond8, ada_w, ada_b):
    depth, d, n9 = ada_w.shape
    tn = n9 // 4
    return pl.pallas_call(
        _mod_kernel,
        grid=(depth, n9 // tn),
        in_specs=[pl.BlockSpec((SUBLANES, d), lambda l, n: (0, 0)),
                  pl.BlockSpec((1, d, tn), lambda l, n: (l, 0, n)),
                  pl.BlockSpec((1, 1, tn), lambda l, n: (l, 0, n))],
        out_specs=pl.BlockSpec((1, SUBLANES, tn), lambda l, n: (l, 0, n)),
        out_shape=jax.ShapeDtypeStruct((depth, SUBLANES, n9), F32),
        compiler_params=_params("arbitrary", "arbitrary"),
        name="adaln_modulation",
    )(cond8, ada_w, ada_b.reshape(depth, 1, n9))


def _store_rows_permuted(res, o_ref, perm_ref, *, to_phase_major):
    tm, d = res.shape
    nch = tm // S5_CHUNK
    for v in range(d // LANES):
        lanes = slice(v * LANES, (v + 1) * LANES)
        if to_phase_major:
            for c in range(nch):
                perm_ref[v, c * PERM_PITCH:c * PERM_PITCH + S5_CHUNK, :] = res[c * S5_CHUNK:(c + 1) * S5_CHUNK, lanes]
            for t in range(S5_CHUNK):
                o_ref[t * nch:(t + 1) * nch, lanes] = perm_ref[v, pl.ds(t, nch, stride=PERM_PITCH), :]
        else:
            for t in range(S5_CHUNK):
                perm_ref[v, pl.ds(t, nch, stride=PERM_PITCH), :] = res[t * nch:(t + 1) * nch, lanes]
            for c in range(nch):
                o_ref[c * S5_CHUNK:(c + 1) * S5_CHUNK, lanes] = perm_ref[v, c * PERM_PITCH:c * PERM_PITCH + S5_CHUNK, :]


def _ffn_kernel(*refs, sub, alpha, d_ff, ck, has_pos, row_order):
    if has_pos:
        x_ref, pos_ref, mod_ref, wup_ref, wdn_ref, g_ref, b_ref, o_ref, act_ref, perm_ref = refs
    else:
        x_ref, mod_ref, wup_ref, wdn_ref, g_ref, b_ref, o_ref, act_ref, perm_ref = refs
    slabs = perm_ref.shape[0] * TOKEN_TILE // x_ref.shape[0]
    for j in range(x_ref.shape[0] // TOKEN_TILE):
        rows = slice(j * TOKEN_TILE, (j + 1) * TOKEN_TILE)
        x = x_ref[rows, :] + pos_ref[rows, :] if has_pos else x_ref[rows, :]
        h = _modulate(x, mod_ref, sub).astype(BF16)
        for c0 in range(0, d_ff, ck):
            c1 = min(c0 + ck, d_ff)
            a = jnp.dot(h, wup_ref[:, c0:c1], preferred_element_type=F32)
            g = jnp.dot(h, wup_ref[:, d_ff + c0:d_ff + c1], preferred_element_type=F32)
            act_ref[rows, c0:c1] = (jax.nn.silu(g) * a).astype(BF16)
        f = jnp.dot(act_ref[rows, :], wdn_ref[...], preferred_element_type=F32)
        z = alpha * x + (0.5 * _gate(mod_ref, sub)) * f
        res = _layer_norm(z, g_ref[...], b_ref[...])
        if row_order is None:
            o_ref[rows, :] = res
        else:
            _store_rows_permuted(res, o_ref.at[rows], perm_ref.at[j * slabs:(j + 1) * slabs],
                                 to_phase_major=row_order == "to_phase_major")


def _ffn(x, mod, w_up, w_dn, ln_g, ln_b, *, layer, sub, alpha, seq_len, pos=None, row_order=None):
    n, d = x.shape
    d_ff = w_dn.shape[2]
    which = (layer, sub // 2)
    tm = min(FFN_SUBTILES * TOKEN_TILE, n)
    ck = FFN_CHUNK
    row = pl.BlockSpec((tm, d), lambda i: (i, 0))
    in_specs = [row]
    args = [x]
    if pos is not None:
        per_seq = seq_len // tm
        in_specs.append(pl.BlockSpec((tm, d), lambda i: (i % per_seq, 0)))
        args.append(pos)
    in_specs += [_mod_spec(mod.shape[0], d, tm, seq_len), _const_spec(w_up.shape[2:], which),
                 _const_spec(w_dn.shape[2:], which), _const_spec((1, d)), _const_spec((1, d))]
    args += [mod, w_up, w_dn, ln_g.reshape(1, d), ln_b.reshape(1, d)]
    perm_rows = TOKEN_TILE // S5_CHUNK * PERM_PITCH
    perm_shape = (tm // TOKEN_TILE * (d // LANES), perm_rows, LANES) if row_order else (1, SUBLANES, LANES)
    return pl.pallas_call(
        functools.partial(_ffn_kernel, sub=sub, alpha=alpha, d_ff=d_ff, ck=ck, has_pos=pos is not None,
                          row_order=row_order),
        grid=(n // tm,),
        in_specs=in_specs,
        out_specs=row,
        out_shape=jax.ShapeDtypeStruct((n, d), F32),
        scratch_shapes=[pltpu.VMEM((tm, d_ff), BF16), pltpu.VMEM(perm_shape, F32)],
        compiler_params=_params("arbitrary"),
        name="ffn_sublayer",
    )(*args)


def _inproj_kernel(x_ref, mod_ref, w_ref, o_ref, *, sub):
    h = _modulate(x_ref[...], mod_ref, sub).astype(BF16)
    o_ref[...] = jnp.dot(h, w_ref[...], preferred_element_type=F32)


def _inproj(x, mod, w, *, sub, seq_len):
    n, d = x.shape
    dn = w.shape[1]
    tm = min(TOKEN_TILE, n)
    return pl.pallas_call(
        functools.partial(_inproj_kernel, sub=sub),
        grid=(n // tm,),
        in_specs=[pl.BlockSpec((tm, d), lambda i: (i, 0)), _mod_spec(mod.shape[0], d, tm, seq_len),
                  _const_spec(w.shape)],
        out_specs=pl.BlockSpec((tm, dn), lambda i: (i, 0)),
        out_shape=jax.ShapeDtypeStruct((n, dn), F32),
        compiler_params=_params("arbitrary"),
        name="mixer_in_proj",
    )(x, mod, w)


def _s5_discretize(are, aim, ldt):
    dt = jnp.exp(ldt)
    mag = jnp.exp(dt * are)
    abr = mag * jnp.cos(dt * aim)
    abi = mag * jnp.sin(dt * aim)
    den = are * are + aim * aim
    nr, ni = abr - 1.0, abi
    qr = (nr * are + ni * aim) / den
    qi = (ni * are - nr * aim) / den
    return abr, abi, qr, qi


def _split_bf16(x):
    hi = x.astype(BF16)
    return hi, (x - hi.astype(F32)).astype(BF16)


def _s5_ops_kernel(arow_ref, btr_ref, bti_ref, ctr_ref, cti_ref, m_ref, mst_ref, min_ref, a16_ref):
    for g in range(arow_ref.shape[0]):
        m_ref[g], mst_ref[g], min_ref[g], a16_ref[g] = _s5_group_operators(
            arow_ref[g], btr_ref[g], bti_ref[g], ctr_ref[g], cti_ref[g])


def _s5_group_operators(arow, btr, bti, ctr, cti):
    t_chunk, grp, w = S5_CHUNK, S5_GROUP, S5_CW
    colblk = lax.broadcasted_iota(jnp.int32, (w, w), 1) // grp
    pow_row = lax.broadcasted_iota(jnp.int32, (t_chunk, w), 0)
    pow_blk = lax.broadcasted_iota(jnp.int32, (t_chunk, w), 1) // grp
    tn_dims = (((0,), (0,)), ((), ()))

    abr2, abi2, qr2, qi2 = _s5_discretize(arow[0:2], arow[2:4], arow[4:6])
    lag_tabs, decay, carry_in = [], [], []
    for d in range(2):
        abr, abi, qr, qi = abr2[d:d + 1], abi2[d:d + 1], qr2[d:d + 1], qi2[d:d + 1]
        lr = [qr * btr - qi * bti]
        li = [qr * bti + qi * btr]
        pr, pi = [abr], [abi]
        for _ in range(t_chunk - 1):
            lr.append(lr[-1] * abr - li[-1] * abi)
            li.append(lr[-2] * abi + li[-1] * abr)
            pr.append(pr[-1] * abr - pi[-1] * abi)
            pi.append(pr[-2] * abi + pi[-1] * abr)
        decay.append((pr[-1], pi[-1]))
        if d == 0:
            lr, li = lr[::-1], li[::-1]
        lag_tabs.append((jnp.concatenate(lr, axis=0), jnp.concatenate(li, axis=0)))

        place = (pow_blk == (pow_row if d == 0 else t_chunk - 1 - pow_row)).astype(BF16)
        e_t = []
        for tbl in (jnp.concatenate(pr, axis=0), jnp.concatenate(pi, axis=0)):
            hi, lo = _split_bf16(tbl)
            e_t.append(lax.dot_general(hi, place, tn_dims, preferred_element_type=F32)
                       + lax.dot_general(lo, place, tn_dims, preferred_element_type=F32))
        er_t, ei_t = e_t
        carry_in.append((ctr * er_t - cti * ei_t, ctr * ei_t + cti * er_t))

    lhs = jnp.concatenate([jnp.concatenate([l_re, -l_im], axis=1) for l_re, l_im in lag_tabs], axis=0)
    l_hi, l_lo = _split_bf16(lhs)
    c_hi, c_lo = _split_bf16(jnp.concatenate([ctr, cti], axis=0))
    tab = (jnp.dot(l_hi, c_hi, preferred_element_type=F32) + jnp.dot(l_lo, c_hi, preferred_element_type=F32)
           + jnp.dot(l_hi, c_lo, preferred_element_type=F32))
    tab_f, tab_b = tab[:w], tab[w:]
    acc = jnp.zeros((w, w), F32)
    for tp in range(t_chunk):
        cut_f = (t_chunk - 1 - tp) * grp
        cut_b = tp * grp
        sh_f = tab_f if cut_f == 0 else jnp.concatenate([tab_f[cut_f:], jnp.zeros((cut_f, w), F32)], 0)
        sh_b = tab_b if cut_b == 0 else jnp.concatenate([jnp.zeros((cut_b, w), F32), tab_b[:w - cut_b]], 0)
        acc = jnp.where(colblk == tp, sh_f + sh_b, acc)
    (fr, fi), (br, bi) = lag_tabs
    (wfr, wfi), (wbr, wbi) = carry_in
    (efr, efi), (ebr, ebi) = decay
    return (acc.astype(BF16),
            jnp.concatenate([fr, br, fi, bi], axis=1).astype(BF16),
            jnp.concatenate([wfr, wbr, -wfi, -wbi], axis=0).astype(BF16),
            jnp.concatenate([efr, ebr, efi, ebi], axis=1))


def _s5_operators(a_re, a_im, log_dt, b_re, b_im, c_re, c_im):
    _, g, p = a_re.shape
    ldt = jnp.broadcast_to(log_dt[:, :, None], (2, g, p))
    arow = jnp.concatenate([a_re, a_im, ldt, jnp.zeros((2, g, p), F32)], axis=0).transpose(1, 0, 2)
    bt_re, bt_im = b_re.transpose(0, 2, 1), b_im.transpose(0, 2, 1)
    ct_re = jnp.tile(c_re.transpose(0, 2, 1), (1, 1, S5_CHUNK))
    ct_im = jnp.tile(c_im.transpose(0, 2, 1), (1, 1, S5_CHUNK))
    gb = S5_OPS_GROUP_BLOCK
    per_g = lambda *s: pl.BlockSpec((gb,) + s, lambda i: (i, 0, 0))
    w = S5_CW
    return pl.pallas_call(
        _s5_ops_kernel,
        grid=(g // gb,),
        in_specs=[per_g(SUBLANES, p), per_g(S5_GROUP, p), per_g(S5_GROUP, p), per_g(p, w), per_g(p, w)],
        out_specs=[per_g(w, w), per_g(w, S5_SW), per_g(S5_SW, w), per_g(1, S5_SW)],
        out_shape=[jax.ShapeDtypeStruct((g, w, w), BF16), jax.ShapeDtypeStruct((g, w, S5_SW), BF16),
                   jax.ShapeDtypeStruct((g, S5_SW, w), BF16), jax.ShapeDtypeStruct((g, 1, S5_SW), F32)],
        compiler_params=_params("arbitrary"),
        name="s5_operators",
    )(arow, bt_re, bt_im, ct_re, ct_im)


def _lane_block_transpose(vs):
    blk = lax.broadcasted_iota(jnp.int32, vs[0].shape, 1) // S5_GROUP
    for dist in (4, 2, 1):
        keep = (blk & dist) == 0
        out = list(vs)
        for r in range(len(vs)):
            if r & dist:
                continue
            lo, hi = vs[r], vs[r + dist]
            out[r] = jnp.where(keep, lo, pltpu.roll(hi, dist * S5_GROUP, axis=1))
            out[r + dist] = jnp.where(keep, pltpu.roll(lo, LANES - dist * S5_GROUP, axis=1), hi)
        vs = out
    return vs


def _s5_in_kernel(x_ref, mod_ref, w_ref, o_ref):
    nch = TOKEN_TILE // S5_CHUNK
    d = w_ref.shape[0]
    per = LANES // S5_GROUP
    n_sub = x_ref.shape[0] // TOKEN_TILE
    us = [jnp.dot(_modulate(x_ref[j * TOKEN_TILE:(j + 1) * TOKEN_TILE, :], mod_ref, 1).astype(BF16), w_ref[...],
                  preferred_element_type=F32) for j in range(n_sub)]
    for j, u in enumerate(us):
        for v in range(d // LANES):
            for th in range(S5_CHUNK // per):
                src = [u[(per * th + r) * nch:(per * th + r + 1) * nch, v * LANES:(v + 1) * LANES]
                       for r in range(per)]
                dst = _lane_block_transpose(src)
                for q in range(per):
                    o_ref[per * v + q, j * nch:(j + 1) * nch, th * LANES:(th + 1) * LANES] = dst[q]


def _s5_in(x, mod, w, *, seq_len):
    n, d = x.shape
    g = d // S5_GROUP
    tm = TOKEN_TILE
    nch = tm // S5_CHUNK
    nc = n // S5_CHUNK
    return pl.pallas_call(
        _s5_in_kernel,
        grid=(n // tm,),
        in_specs=[pl.BlockSpec((tm, d), lambda i: (i, 0)),
                  _mod_spec(mod.shape[0], d, tm, seq_len), _const_spec(w.shape)],
        out_specs=pl.BlockSpec((g, nch, S5_CW), lambda i: (0, i, 0)),
        out_shape=jax.ShapeDtypeStruct((g, nc, S5_CW), F32),
        compiler_params=_params("arbitrary"),
        name="s5_in_proj",
    )(x, mod, w)


def _s5_mix_kernel(u_ref, m_ref, mst_ref, min_ref, a16_ref, dsk_ref, s0_ref, y_ref, fin_ref,
                   loc_ref, sa_ref, sb_ref, *, n_seq, ncs):
    gb = u_ref.shape[0]
    half = S5_SW // 2
    pitch = ncs + SCAN_PAD
    for g in range(gb):
        loc = jnp.dot(u_ref[g].astype(BF16), mst_ref[g], preferred_element_type=F32)
        for s in range(n_seq):
            loc_ref[0, g, s * pitch:s * pitch + ncs, :] = loc[s * ncs:(s + 1) * ncs, :half]
            loc_ref[1, g, s * pitch:s * pitch + ncs, :] = loc[s * ncs:(s + 1) * ncs, half:]
    is_fwd = lax.broadcasted_iota(jnp.int32, (n_seq, half), 1) < S5_STATE
    decay = [(a16_ref[g][:, :half], a16_ref[g][:, half:]) for g in range(gb)]

    def step(k, carry):
        rows_f = pl.ds(k, n_seq, stride=pitch)
        rows_b = pl.ds(ncs - 1 - k, n_seq, stride=pitch)
        out = []
        for g in range(gb):
            sr, si = carry[2 * g], carry[2 * g + 1]
            ar, ai = decay[g]
            sa_ref[0, g, rows_f, :] = sr
            sa_ref[1, g, rows_f, :] = si
            sb_ref[0, g, rows_b, :] = sr
            sb_ref[1, g, rows_b, :] = si
            lr = jnp.where(is_fwd, loc_ref[0, g, rows_f, :], loc_ref[0, g, rows_b, :])
            li = jnp.where(is_fwd, loc_ref[1, g, rows_f, :], loc_ref[1, g, rows_b, :])
            out += [ar * sr - ai * si + lr, ar * si + ai * sr + li]
        return tuple(out)

    init = []
    for g in range(gb):
        init += [s0_ref[g][:, :half], s0_ref[g][:, half:]]
    fin = lax.fori_loop(0, ncs, step, tuple(init))
    sel = lax.broadcasted_iota(jnp.int32, (ncs, half), 1) < S5_STATE
    for g in range(gb):
        fin_ref[g] = jnp.concatenate([fin[2 * g], fin[2 * g + 1]], axis=1)
        parts = []
        for ri in range(2):
            parts.append(jnp.concatenate(
                [jnp.where(sel, sa_ref[ri, g, s * pitch:s * pitch + ncs, :], sb_ref[ri, g, s * pitch:s * pitch + ncs, :])
                 for s in range(n_seq)], axis=0))
        s_in = jnp.concatenate(parts, axis=1).astype(BF16)
        u = u_ref[g]
        y_ref[g] = (jnp.dot(u.astype(BF16), m_ref[g], preferred_element_type=F32)
                    + jnp.dot(s_in, min_ref[g], preferred_element_type=F32)
                    + dsk_ref[g] * u)


def _s5_mix(u_chunks, ops, d_skip, s0, *, n_seq, ncs):
    g, nc, w = u_chunks.shape
    m, mst, m_in, a16 = ops
    gb = S5_GROUP_BLOCK
    dsk = jnp.tile(d_skip.reshape(g, 1, S5_GROUP), (1, 1, S5_CHUNK))
    per_g = lambda *s: pl.BlockSpec((gb,) + s, lambda i: (i, 0, 0))
    rows = n_seq * (ncs + SCAN_PAD)
    return pl.pallas_call(
        functools.partial(_s5_mix_kernel, n_seq=n_seq, ncs=ncs),
        grid=(g // gb,),
        in_specs=[per_g(nc, w), per_g(w, w), per_g(w, S5_SW), per_g(S5_SW, w), per_g(1, S5_SW), per_g(1, w),
                  per_g(n_seq, S5_SW)],
        out_specs=[per_g(nc, w), per_g(n_seq, S5_SW)],
        out_shape=[jax.ShapeDtypeStruct((g, nc, w), F32), jax.ShapeDtypeStruct((g, n_seq, S5_SW), F32)],
        scratch_shapes=[pltpu.VMEM((2, gb, rows, S5_SW // 2), F32)] * 3,
        compiler_params=_params("arbitrary"),
        name="s5_chunk_scan",
    )(u_chunks, m, mst, m_in, a16, dsk, s0)


def _s5_out_kernel(y_ref, x_ref, mod_ref, w_ref, g_ref, b_ref, o_ref, *, alpha):
    nch = TOKEN_TILE // S5_CHUNK
    d = w_ref.shape[0]
    per = LANES // S5_GROUP
    n_sub = x_ref.shape[0] // TOKEN_TILE
    lhs = []
    for j in range(n_sub):
        phase = [[None] * (d // LANES) for _ in range(S5_CHUNK)]
        for v in range(d // LANES):
            for th in range(S5_CHUNK // per):
                src = [y_ref[per * v + q, j * nch:(j + 1) * nch, th * LANES:(th + 1) * LANES] for q in range(per)]
                dst = _lane_block_transpose(src)
                for r in range(per):
                    phase[per * th + r][v] = dst[r]
        y = jnp.concatenate([jnp.concatenate(p, axis=1) for p in phase], axis=0)
        lhs.append(_gelu(y).astype(BF16))
    zs = [jnp.dot(a, w_ref[...], preferred_element_type=F32) for a in lhs]
    for j, z in enumerate(zs):
        rows = slice(j * TOKEN_TILE, (j + 1) * TOKEN_TILE)
        out = z[:, :d] * _sigmoid(z[:, d:])
        o_ref[rows, :] = _layer_norm(alpha * x_ref[rows, :] + _gate(mod_ref, 1) * out, g_ref[...], b_ref[...])


def _s5_out(y_chunks, x, mod, w_glu, ln_g, ln_b, *, alpha, seq_len):
    n, d = x.shape
    g = d // S5_GROUP
    tm = S5_PROJ_SUBTILES * TOKEN_TILE
    nch = tm // S5_CHUNK
    row = pl.BlockSpec((tm, d), lambda i: (i, 0))
    return pl.pallas_call(
        functools.partial(_s5_out_kernel, alpha=alpha),
        grid=(n // tm,),
        in_specs=[pl.BlockSpec((g, nch, S5_CW), lambda i: (0, i, 0)), row,
                  _mod_spec(mod.shape[0], d, tm, seq_len), _const_spec(w_glu.shape),
                  _const_spec((1, d)), _const_spec((1, d))],
        out_specs=row,
        out_shape=jax.ShapeDtypeStruct((n, d), F32),
        compiler_params=_params("arbitrary"),
        name="s5_out_proj",
    )(y_chunks, x, mod, w_glu, ln_g.reshape(1, d), ln_b.reshape(1, d))


def _s5_mixer(x, mod, w_in, ops, d_skip, w_glu, ln_g, ln_b, init, *, alpha, n_seq, seq_len):
    n, d = x.shape
    g = d // S5_GROUP
    ncs = seq_len // S5_CHUNK
    u_chunks = _s5_in(x, mod, w_in, seq_len=seq_len)
    if init is None:
        s0 = jnp.zeros((g, n_seq, S5_SW), F32)
    else:
        s0 = init.transpose(3, 0, 2, 1, 4).reshape(g, n_seq, S5_SW)
    y_chunks, fin = _s5_mix(u_chunks, ops, d_skip, s0, n_seq=n_seq, ncs=ncs)
    x_new = _s5_out(y_chunks, x, mod, w_glu, ln_g, ln_b, alpha=alpha, seq_len=seq_len)
    state = fin.reshape(g, n_seq, 2, 2, S5_STATE).transpose(1, 3, 2, 0, 4)
    return x_new, state


def _softplus(x):
    return jnp.maximum(x, 0.0) + jnp.log1p(jnp.exp(-jnp.abs(x)))


def _lru_scan_kernel(xr_ref, gate_ref, cw_ref, cb_ref, w4_ref, b4_ref, lam_ref, h0_ref, y_ref, fin_ref,
                     af_ref, bf_ref, ab_ref, bb_ref, end_ref, xpad_ref, *, n_seq, seq_len, seg):
    n = n_seq * seq_len
    n_seg = seq_len // seg
    ns = n_seq * n_seg
    pitch = seg + SCAN_PAD
    c = LANES
    chained = n_seg > 1
    half_rate = (0.5 * LRU_C) * _softplus(-lam_ref[0])
    cw, cb = cw_ref[0], cb_ref[0]
    w4_half = (0.5 * w4_ref[0].astype(F32)).astype(BF16)
    b4_half = 0.5 * b4_ref[0]

    zero_tile = jnp.zeros((SUBLANES, c), F32)
    xpad_ref[0:SUBLANES, :] = zero_tile
    xpad_ref[SUBLANES + n:2 * SUBLANES + n, :] = zero_tile

    def pad_copy(s, _):
        r0 = pl.multiple_of(s * seg, seg)
        xpad_ref[pl.ds(SUBLANES + r0, seg), :] = xr_ref[pl.ds(r0, seg), :]
        return 0

    lax.fori_loop(0, ns, pad_copy, 0)
    row8 = lax.broadcasted_iota(jnp.int32, (SUBLANES, c), 0)

    def gates(s, _):
        r0 = pl.multiple_of(s * seg, seg)
        p0 = pl.multiple_of(s * pitch, SCAN_PAD)
        seq_start = s % n_seg == 0
        seq_end = s % n_seg == n_seg - 1
        xc = cb
        for k in range(CONV_W):
            off = k - CONV_LEFT
            tap = xpad_ref[pl.ds(SUBLANES + r0 + off, seg), :]
            if off < 0:
                head = jnp.where(seq_start & (row8 < -off), 0.0, tap[:SUBLANES])
                tap = jnp.concatenate([head, tap[SUBLANES:]], axis=0)
            elif off > 0:
                tail = jnp.where(seq_end & (row8 >= SUBLANES - off), 0.0, tap[seg - SUBLANES:])
                tap = jnp.concatenate([tap[:seg - SUBLANES], tail], axis=0)
            xc = xc + tap * cw[k:k + 1]
        half_pre = jnp.dot(xc.astype(BF16), w4_half, preferred_element_type=F32) + b4_half
        xc_half = 0.5 * xc
        for d, (a_ref, b_ref) in enumerate(((af_ref, bf_ref), (ab_ref, bb_ref))):
            tanh_a = jnp.tanh(half_pre[:, 2 * d * c:(2 * d + 1) * c])
            tanh_x = jnp.tanh(half_pre[:, (2 * d + 1) * c:(2 * d + 2) * c])
            neg_log_a = half_rate[d:d + 1] * tanh_a + half_rate[d:d + 1]
            a = jnp.exp2(neg_log_a * -LOG2_E)
            a_ref[pl.ds(p0, seg), :] = a
            v = jnp.tanh(neg_log_a) * (a * a + 1.0)
            root = jnp.where(v > 0.0, v * lax.rsqrt(v), 0.0)
            b_ref[pl.ds(p0, seg), :] = root * (xc_half * tanh_x + xc_half)
        return 0

    lax.fori_loop(0, ns, gates, 0, unroll=2)

    def step(k, carry):
        hf, hb, pf, pb = carry
        rows_f = pl.ds(k, ns, stride=pitch)
        rows_b = pl.ds(seg - 1 - k, ns, stride=pitch)
        af, ab = af_ref[rows_f, :], ab_ref[rows_b, :]
        hf = af * hf + bf_ref[rows_f, :]
        hb = ab * hb + bb_ref[rows_b, :]
        bf_ref[rows_f, :] = hf
        bb_ref[rows_b, :] = hb
        if chained:
            pf, pb = pf * af, pb * ab
            af_ref[rows_f, :] = pf
            ab_ref[rows_b, :] = pb
        return hf, hb, pf, pb

    one = jnp.ones((ns, c), F32)
    if chained:
        init = (jnp.zeros((ns, c), F32), jnp.zeros((ns, c), F32), one, one)
    else:
        init = (h0_ref[0], h0_ref[1], one, one)
    hf, hb, pf, pb = lax.fori_loop(0, seg, step, init, unroll=2)

    if chained:
        end_ref[0], end_ref[1], end_ref[2], end_ref[3] = hf, hb, pf, pb
        hin_f, hin_b = h0_ref[0], h0_ref[1]
        for j in range(n_seg):
            jf, jb = pl.ds(j, n_seq, stride=n_seg), pl.ds(n_seg - 1 - j, n_seq, stride=n_seg)
            end_ref[4, jf, :] = hin_f
            end_ref[5, jb, :] = hin_b
            hin_f = end_ref[2, jf, :] * hin_f + end_ref[0, jf, :]
            hin_b = end_ref[3, jb, :] * hin_b + end_ref[1, jb, :]
        fin_ref[0] = hin_f
        fin_ref[1] = hin_b
    else:
        fin_ref[0] = hf
        fin_ref[1] = hb

    def combine(s, _):
        rows = pl.ds(pl.multiple_of(s * seg, seg), seg)
        prow = pl.ds(pl.multiple_of(s * pitch, SCAN_PAD), seg)
        h = bf_ref[prow, :] + bb_ref[prow, :]
        if chained:
            h = h + af_ref[prow, :] * end_ref[4, pl.ds(s, 1), :] + ab_ref[prow, :] * end_ref[5, pl.ds(s, 1), :]
        y_ref[rows, :] = (h * _gelu(gate_ref[rows, :])).astype(BF16)
        return 0

    lax.fori_loop(0, ns, combine, 0)


def _lru_scan(xg, conv_w, conv_b, w4, b4, lam, h0, *, n_seq, seq_len):
    n, d2 = xg.shape
    d = d2 // 2
    nt = d // LANES
    seg = min(LRU_SEGMENT, seq_len)
    ns = n // seg
    stripe = lambda off: pl.BlockSpec((n, LANES), lambda j: (0, j + off))
    per_t = lambda *s: pl.BlockSpec((1,) + s, lambda j: (j, 0, 0))
    return pl.pallas_call(
        functools.partial(_lru_scan_kernel, n_seq=n_seq, seq_len=seq_len, seg=seg),
        grid=(nt,),
        in_specs=[stripe(0), stripe(nt), per_t(CONV_W, LANES), per_t(1, LANES), per_t(LANES, 4 * LANES),
                  per_t(1, 4 * LANES), per_t(2, LANES), pl.BlockSpec((2, n_seq, LANES), lambda j: (0, 0, j))],
        out_specs=[pl.BlockSpec((n, LANES), lambda j: (0, j)),
                   pl.BlockSpec((2, n_seq, LANES), lambda j: (0, 0, j))],
        out_shape=[jax.ShapeDtypeStruct((n, d), BF16), jax.ShapeDtypeStruct((2, n_seq, d), F32)],
        scratch_shapes=([pltpu.VMEM((ns * (seg + SCAN_PAD), LANES), F32)] * 4
                        + [pltpu.VMEM((6, ns, LANES), F32), pltpu.VMEM((n + 2 * SUBLANES, LANES), F32)]),
        compiler_params=_params("arbitrary"),
        name="rglru_scan",
    )(xg, xg, conv_w, conv_b, w4, b4, lam, h0)


def _lru_out_kernel(y_ref, x_ref, mod_ref, w_ref, g_ref, b_ref, o_ref, *, alpha):
    tiles = [slice(j * TOKEN_TILE, (j + 1) * TOKEN_TILE) for j in range(x_ref.shape[0] // TOKEN_TILE)]
    outs = [jnp.dot(y_ref[rows, :], w_ref[...], preferred_element_type=F32) for rows in tiles]
    for rows, out in zip(tiles, outs):
        o_ref[rows, :] = _layer_norm(alpha * x_ref[rows, :] + _gate(mod_ref, 1) * out, g_ref[...], b_ref[...])


def _lru_out(y, x, mod, w_out, ln_g, ln_b, *, alpha, seq_len):
    n, d = x.shape
    tm = LRU_OUT_SUBTILES * TOKEN_TILE
    row = pl.BlockSpec((tm, d), lambda i: (i, 0))
    return pl.pallas_call(
        functools.partial(_lru_out_kernel, alpha=alpha),
        grid=(n // tm,),
        in_specs=[row, row, _mod_spec(mod.shape[0], d, tm, seq_len), _const_spec(w_out.shape),
                  _const_spec((1, d)), _const_spec((1, d))],
        out_specs=row,
        out_shape=jax.ShapeDtypeStruct((n, d), F32),
        compiler_params=_params("arbitrary"),
        name="rglru_out_proj",
    )(y, x, mod, w_out, ln_g.reshape(1, d), ln_b.reshape(1, d))


def _lru_stripe_weights(conv_w, conv_b, w_a, b_a, w_x, b_x, lam):
    d = conv_w.shape[1]
    nt = d // LANES
    per = LANES // LRU_BLOCK

    def blockdiag(w):
        wb = w.reshape(nt, per, LRU_BLOCK, LRU_BLOCK)
        eye = jnp.eye(per, dtype=w.dtype)
        return jnp.einsum('tpij,pq->tpiqj', wb, eye).reshape(nt, LANES, LANES)

    w4 = jnp.concatenate([blockdiag(w_a[0]), blockdiag(w_x[0]), blockdiag(w_a[1]), blockdiag(w_x[1])], axis=2)
    b4 = jnp.concatenate([v.reshape(nt, 1, LANES) for v in (b_a[0], b_x[0], b_a[1], b_x[1])], axis=2)
    cw = conv_w.reshape(CONV_W, nt, LANES).transpose(1, 0, 2)
    cb = conv_b.reshape(nt, 1, LANES)
    lam_t = lam.reshape(2, nt, LANES).transpose(1, 0, 2)
    return cw, cb, w4.astype(BF16), b4, lam_t


def _lru_mixer(x, mod, w_in, stripe_w, w_out, ln_g, ln_b, init, *, alpha, n_seq, seq_len):
    n, d = x.shape
    xg = _inproj(x, mod, w_in, sub=1, seq_len=seq_len)
    h0 = jnp.zeros((2, n_seq, d), F32) if init is None else init.transpose(1, 0, 2)
    y, fin = _lru_scan(xg, *stripe_w, h0, n_seq=n_seq, seq_len=seq_len)
    x_new = _lru_out(y, x, mod, w_out, ln_g, ln_b, alpha=alpha, seq_len=seq_len)
    return x_new, fin.transpose(1, 0, 2)


def _grid_pos_embed(rows, d):
    quarter = d // 4
    omega = 1.0 / (10000.0 ** (jnp.arange(quarter, dtype=F32) / quarter))

    def emb(p):
        ang = p[:, None] * omega[None, :]
        return jnp.concatenate([jnp.sin(ang), jnp.cos(ang)], -1)

    by_row = emb(jnp.arange(rows, dtype=F32))[:, None, :]
    by_col = emb(jnp.arange(GRID_W, dtype=F32))[None, :, :]
    half = (rows, GRID_W, d // 2)
    table = jnp.concatenate([jnp.broadcast_to(by_row, half), jnp.broadcast_to(by_col, half)], -1)
    return table.reshape(rows * GRID_W, d)


def _trunk(x3, mod, init_s5, init_lru, wts, pos):
    n_seq, seq_len, d = x3.shape
    depth = mod.shape[0]
    alpha = (2.0 * depth) ** 0.25
    x = x3.reshape(n_seq * seq_len, d)
    st_s5, st_lru = [], []
    for i in range(depth):
        j = i // 2
        m = mod[i]
        x = _ffn(x, m, wts["up"], wts["dn"], wts["ln_g"][i, 0], wts["ln_b"][i, 0],
                 layer=i, sub=0, alpha=alpha, seq_len=seq_len, pos=pos if i == 0 else None,
                 row_order="to_phase_major" if i % 2 == 0 else None)
        if i % 2 == 0:
            init = None if init_s5 is None else init_s5[:, j]
            x, st = _s5_mixer(x, m, wts["s5_w_in"][j], wts["s5_ops"][j], wts["s5_d"][j], wts["s5_w_glu"][j],
                              wts["ln_g"][i, 1], wts["ln_b"][i, 1], init,
                              alpha=alpha, n_seq=n_seq, seq_len=seq_len)
            st_s5.append(st)
        else:
            init = None if init_lru is None else init_lru[:, j]
            x, st = _lru_mixer(x, m, wts["lru_w_in"][j], wts["lru_stripe"][j], wts["lru_w_out"][j],
                               wts["ln_g"][i, 1], wts["ln_b"][i, 1], init,
                               alpha=alpha, n_seq=n_seq, seq_len=seq_len)
            st_lru.append(st)
        x = _ffn(x, m, wts["up"], wts["dn"], wts["ln_g"][i, 2], wts["ln_b"][i, 2],
                 layer=i, sub=2, alpha=alpha, seq_len=seq_len, row_order="to_natural" if i % 2 == 0 else None)
    return x.reshape(n_seq, seq_len, d), st_s5, st_lru


def kernel(x_prompt, x_sample, c, state_s5, state_lru, c_ctx, ada_w, ada_b, ln_g, ln_b, ffn_w_up, ffn_w_down, s5_w_in, s5_a_re, s5_a_im, s5_log_dt, s5_b_re, s5_b_im, s5_c_re, s5_c_im, s5_d, s5_w_glu, lru_w_in, lru_conv_w, lru_conv_b, lru_w_a, lru_b_a, lru_w_x, lru_b_x, lru_lambda, lru_w_out):
    depth, d, _ = ada_w.shape
    n_dec = c.shape[0]
    assert 1 + n_dec <= SUBLANES
    cond8 = jnp.concatenate([c_ctx[None, :], c, jnp.zeros((SUBLANES - 1 - n_dec, d), F32)], axis=0)
    mod = _modulation(cond8, ada_w, ada_b).reshape(depth, SUBLANES, 9, d)
    mod_ctx, mod_dec = mod[:, 0:1], mod[:, 1:1 + n_dec]

    wts = {
        "up": ffn_w_up.astype(BF16), "dn": ffn_w_down.astype(BF16),
        "ln_g": ln_g, "ln_b": ln_b,
        "s5_w_in": s5_w_in.astype(BF16), "s5_d": s5_d, "s5_w_glu": s5_w_glu.astype(BF16),
        "s5_ops": [_s5_operators(s5_a_re[j], s5_a_im[j], s5_log_dt[j], s5_b_re[j], s5_b_im[j],
                                 s5_c_re[j], s5_c_im[j]) for j in range(s5_w_in.shape[0])],
        "lru_w_in": lru_w_in.astype(BF16), "lru_w_out": lru_w_out.astype(BF16),
        "lru_stripe": [_lru_stripe_weights(lru_conv_w[j], lru_conv_b[j], lru_w_a[j], lru_b_a[j], lru_w_x[j],
                                           lru_b_x[j], lru_lambda[j]) for j in range(lru_w_in.shape[0])],
    }

    y_prompt, st_s5, st_lru = _trunk(x_prompt, mod_ctx, None, None, wts, None)
    new_state_s5 = jnp.stack(st_s5, 1).astype(x_prompt.dtype)
    new_state_lru = jnp.stack(st_lru, 1).astype(x_prompt.dtype)

    pos = _grid_pos_embed(x_sample.shape[1] // GRID_W, d).astype(x_sample.dtype)
    y_sample, _, _ = _trunk(x_sample, mod_dec, state_s5, state_lru, wts, pos)
    return (y_prompt, y_sample, new_state_s5, new_state_lru)
```

```python
import functools

import jax
import jax.numpy as jnp
from jax import lax
from jax.experimental import pallas as pl
from jax.experimental.pallas import tpu as pltpu

F32 = jnp.float32
BF16 = jnp.bfloat16

GRID_W = 64
S5_GROUP = 16
S5_STATE = 64
LRU_BLOCK = 64
LRU_C = 8.0
CONV_W = 4
CONV_LEFT = 2
LN_EPS = 1e-5
LOG2_E = 1.4426950408889634

LANES = 128
SUBLANES = 8
VMEM_LIMIT_BYTES = 56 * 1024 * 1024

S5_CHUNK = 16
S5_CW = S5_CHUNK * S5_GROUP
S5_SW = 4 * S5_STATE
S5_GROUP_BLOCK = 8
PERM_PITCH = S5_CHUNK + 4
S5_OPS_GROUP_BLOCK = 4
SCAN_PAD = 4
TOKEN_TILE = 512
FFN_SUBTILES = 2
FFN_CHUNK = 256
S5_PROJ_SUBTILES = 2
LRU_SEGMENT = 256


def _params(*sem):
    return pltpu.CompilerParams(dimension_semantics=sem, vmem_limit_bytes=VMEM_LIMIT_BYTES)


def _const_spec(shape, lead=()):
    idx = tuple(lead) + (0,) * len(shape)
    return pl.BlockSpec((None,) * len(lead) + tuple(shape), lambda *_: idx, pipeline_mode=pl.Buffered(1))


def _layer_norm(z, g, b):
    mu = jnp.mean(z, axis=-1, keepdims=True)
    zc = z - mu
    var = jnp.mean(zc * zc, axis=-1, keepdims=True)
    return zc * lax.rsqrt(var + LN_EPS) * g + b


def _sigmoid(x):
    return 0.5 * jnp.tanh(0.5 * x) + 0.5


def _gelu(x):
    c = 0.7978845608028654
    h = 0.5 * x
    return h + h * jnp.tanh(x * ((c * 0.044715) * (x * x) + c))


def _modulate(x, mod_ref, sub):
    sh = mod_ref[0, 3 * sub:3 * sub + 1, :]
    sc = mod_ref[0, 3 * sub + 1:3 * sub + 2, :]
    return x * (1.0 + sc) + sh


def _gate(mod_ref, sub):
    return mod_ref[0, 3 * sub + 2:3 * sub + 3, :]


def _mod_spec(n_cond, d, tm, seq_len):
    if n_cond == 1:
        return pl.BlockSpec((1, 9, d), lambda i: (0, 0, 0))
    return pl.BlockSpec((1, 9, d), lambda i: ((i * tm) // seq_len, 0, 0))


def _mod_kernel(c_ref, w_ref, b_ref, o_ref):
    s = jax.nn.silu(c_ref[...]).astype(BF16)
    o_ref[0] = jnp.dot(s, w_ref[0].astype(BF16), preferred_element_type=F32) + b_ref[0]


def _modulation(cond8, ada_w, ada_b):
    depth, d, n9 = ada_w.shape
    tn = n9 // 4
    return pl.pallas_call(
        _mod_kernel,
        grid=(depth, n9 // tn),
        in_specs=[pl.BlockSpec((SUBLANES, d), lambda l, n: (0, 0)),
                  pl.BlockSpec((1, d, tn), lambda l, n: (l, 0, n)),
                  pl.BlockSpec((1, 1, tn), lambda l, n: (l, 0, n))],
        out_specs=pl.BlockSpec((1, SUBLANES, tn), lambda l, n: (l, 0, n)),
        out_shape=jax.ShapeDtypeStruct((depth, SUBLANES, n9), F32),
        compiler_params=_params("arbitrary", "arbitrary"),
        name="adaln_modulation",
    )(cond8, ada_w, ada_b.reshape(depth, 1, n9))


def _store_rows_permuted(res, o_ref, perm_ref, *, to_phase_major):
    tm, d = res.shape
    nch = tm // S5_CHUNK
    for v in range(d // LANES):
        lanes = slice(v * LANES, (v + 1) * LANES)
        if to_phase_major:
            for c in range(nch):
                perm_ref[v, c * PERM_PITCH:c * PERM_PITCH + S5_CHUNK, :] = res[c * S5_CHUNK:(c + 1) * S5_CHUNK, lanes]
            for t in range(S5_CHUNK):
                o_ref[t * nch:(t + 1) * nch, lanes] = perm_ref[v, pl.ds(t, nch, stride=PERM_PITCH), :]
        else:
            for t in range(S5_CHUNK):
                perm_ref[v, pl.ds(t, nch, stride=PERM_PITCH), :] = res[t * nch:(t + 1) * nch, lanes]
            for c in range(nch):
                o_ref[c * S5_CHUNK:(c + 1) * S5_CHUNK, lanes] = perm_ref[v, c * PERM_PITCH:c * PERM_PITCH + S5_CHUNK, :]


def _ffn_kernel(*refs, sub, alpha, d_ff, ck, has_pos, has_mixer_out, row_order):
    x_ref, refs = refs[0], refs[1:]
    if has_pos:
        pos_ref, refs = refs[0], refs[1:]
    if has_mixer_out:
        (y_ref, wout_ref, mg_ref, mb_ref), refs = refs[:4], refs[4:]
    mod_ref, wup_ref, wdn_ref, g_ref, b_ref, o_ref, act_ref, perm_ref = refs
    slabs = perm_ref.shape[0] * TOKEN_TILE // x_ref.shape[0]
    for j in range(x_ref.shape[0] // TOKEN_TILE):
        rows = slice(j * TOKEN_TILE, (j + 1) * TOKEN_TILE)
        x = x_ref[rows, :] + pos_ref[rows, :] if has_pos else x_ref[rows, :]
        if has_mixer_out:
            mixed = jnp.dot(y_ref[rows, :], wout_ref[...], preferred_element_type=F32)
            x = _layer_norm(alpha * x + _gate(mod_ref, 1) * mixed, mg_ref[...], mb_ref[...])
        h = _modulate(x, mod_ref, sub).astype(BF16)
        for c0 in range(0, d_ff, ck):
            c1 = min(c0 + ck, d_ff)
            a = jnp.dot(h, wup_ref[:, c0:c1], preferred_element_type=F32)
            g = jnp.dot(h, wup_ref[:, d_ff + c0:d_ff + c1], preferred_element_type=F32)
            act_ref[rows, c0:c1] = (jax.nn.silu(g) * a).astype(BF16)
        f = jnp.dot(act_ref[rows, :], wdn_ref[...], preferred_element_type=F32)
        z = alpha * x + (0.5 * _gate(mod_ref, sub)) * f
        res = _layer_norm(z, g_ref[...], b_ref[...])
        if row_order is None:
            o_ref[rows, :] = res
        else:
            _store_rows_permuted(res, o_ref.at[rows], perm_ref.at[j * slabs:(j + 1) * slabs],
                                 to_phase_major=row_order == "to_phase_major")


def _ffn(x, mod, w_up, w_dn, ln_g, ln_b, *, layer, sub, alpha, seq_len, pos=None, row_order=None, mixer_out=None):
    n, d = x.shape
    d_ff = w_dn.shape[2]
    which = (layer, sub // 2)
    tm = TOKEN_TILE if mixer_out is not None else min(FFN_SUBTILES * TOKEN_TILE, n)
    ck = FFN_CHUNK
    row = pl.BlockSpec((tm, d), lambda i: (i, 0))
    in_specs = [row]
    args = [x]
    if pos is not None:
        per_seq = seq_len // tm
        in_specs.append(pl.BlockSpec((tm, d), lambda i: (i % per_seq, 0)))
        args.append(pos)
    if mixer_out is not None:
        y, w_out, mix_g, mix_b = mixer_out
        in_specs += [row, _const_spec(w_out.shape), _const_spec((1, d)), _const_spec((1, d))]
        args += [y, w_out, mix_g.reshape(1, d), mix_b.reshape(1, d)]
    in_specs += [_mod_spec(mod.shape[0], d, tm, seq_len), _const_spec(w_up.shape[2:], which),
                 _const_spec(w_dn.shape[2:], which), _const_spec((1, d)), _const_spec((1, d))]
    args += [mod, w_up, w_dn, ln_g.reshape(1, d), ln_b.reshape(1, d)]
    perm_rows = TOKEN_TILE // S5_CHUNK * PERM_PITCH
    perm_shape = (tm // TOKEN_TILE * (d // LANES), perm_rows, LANES) if row_order else (1, SUBLANES, LANES)
    return pl.pallas_call(
        functools.partial(_ffn_kernel, sub=sub, alpha=alpha, d_ff=d_ff, ck=ck, has_pos=pos is not None,
                          has_mixer_out=mixer_out is not None, row_order=row_order),
        grid=(n // tm,),
        in_specs=in_specs,
        out_specs=row,
        out_shape=jax.ShapeDtypeStruct((n, d), F32),
        scratch_shapes=[pltpu.VMEM((tm, d_ff), BF16), pltpu.VMEM(perm_shape, F32)],
        compiler_params=_params("arbitrary"),
        name="ffn_sublayer",
    )(*args)


def _inproj_kernel(x_ref, mod_ref, w_ref, o_ref, *, sub):
    h = _modulate(x_ref[...], mod_ref, sub).astype(BF16)
    o_ref[...] = jnp.dot(h, w_ref[...], preferred_element_type=F32)


def _inproj(x, mod, w, *, sub, seq_len):
    n, d = x.shape
    dn = w.shape[1]
    tm = min(TOKEN_TILE, n)
    return pl.pallas_call(
        functools.partial(_inproj_kernel, sub=sub),
        grid=(n // tm,),
        in_specs=[pl.BlockSpec((tm, d), lambda i: (i, 0)), _mod_spec(mod.shape[0], d, tm, seq_len),
                  _const_spec(w.shape)],
        out_specs=pl.BlockSpec((tm, dn), lambda i: (i, 0)),
        out_shape=jax.ShapeDtypeStruct((n, dn), F32),
        compiler_params=_params("arbitrary"),
        name="mixer_in_proj",
    )(x, mod, w)


def _s5_discretize(are, aim, ldt):
    dt = jnp.exp(ldt)
    mag = jnp.exp(dt * are)
    abr = mag * jnp.cos(dt * aim)
    abi = mag * jnp.sin(dt * aim)
    den = are * are + aim * aim
    nr, ni = abr - 1.0, abi
    qr = (nr * are + ni * aim) / den
    qi = (ni * are - nr * aim) / den
    return abr, abi, qr, qi


def _split_bf16(x):
    hi = x.astype(BF16)
    return hi, (x - hi.astype(F32)).astype(BF16)


def _s5_ops_kernel(arow_ref, btr_ref, bti_ref, ctr_ref, cti_ref, m_ref, mst_ref, min_ref, a16_ref):
    for g in range(arow_ref.shape[0]):
        m_ref[g], mst_ref[g], min_ref[g], a16_ref[g] = _s5_group_operators(
            arow_ref[g], btr_ref[g], bti_ref[g], ctr_ref[g], cti_ref[g])


def _s5_group_operators(arow, btr, bti, ctr, cti):
    t_chunk, grp, w = S5_CHUNK, S5_GROUP, S5_CW
    colblk = lax.broadcasted_iota(jnp.int32, (w, w), 1) // grp
    pow_row = lax.broadcasted_iota(jnp.int32, (t_chunk, w), 0)
    pow_blk = lax.broadcasted_iota(jnp.int32, (t_chunk, w), 1) // grp
    tn_dims = (((0,), (0,)), ((), ()))

    abr2, abi2, qr2, qi2 = _s5_discretize(arow[0:2], arow[2:4], arow[4:6])
    lag_tabs, decay, carry_in = [], [], []
    for d in range(2):
        abr, abi, qr, qi = abr2[d:d + 1], abi2[d:d + 1], qr2[d:d + 1], qi2[d:d + 1]
        lr = [qr * btr - qi * bti]
        li = [qr * bti + qi * btr]
        pr, pi = [abr], [abi]
        for _ in range(t_chunk - 1):
            lr.append(lr[-1] * abr - li[-1] * abi)
            li.append(lr[-2] * abi + li[-1] * abr)
            pr.append(pr[-1] * abr - pi[-1] * abi)
            pi.append(pr[-2] * abi + pi[-1] * abr)
        decay.append((pr[-1], pi[-1]))
        if d == 0:
            lr, li = lr[::-1], li[::-1]
        lag_tabs.append((jnp.concatenate(lr, axis=0), jnp.concatenate(li, axis=0)))

        place = (pow_blk == (pow_row if d == 0 else t_chunk - 1 - pow_row)).astype(BF16)
        e_t = []
        for tbl in (jnp.concatenate(pr, axis=0), jnp.concatenate(pi, axis=0)):
            hi, lo = _split_bf16(tbl)
            e_t.append(lax.dot_general(hi, place, tn_dims, preferred_element_type=F32)
                       + lax.dot_general(lo, place, tn_dims, preferred_element_type=F32))
        er_t, ei_t = e_t
        carry_in.append((ctr * er_t - cti * ei_t, ctr * ei_t + cti * er_t))

    lhs = jnp.concatenate([jnp.concatenate([l_re, -l_im], axis=1) for l_re, l_im in lag_tabs], axis=0)
    l_hi, l_lo = _split_bf16(lhs)
    c_hi, c_lo = _split_bf16(jnp.concatenate([ctr, cti], axis=0))
    tab = (jnp.dot(l_hi, c_hi, preferred_element_type=F32) + jnp.dot(l_lo, c_hi, preferred_element_type=F32)
           + jnp.dot(l_hi, c_lo, preferred_element_type=F32))
    tab_f, tab_b = tab[:w], tab[w:]
    acc = jnp.zeros((w, w), F32)
    for tp in range(t_chunk):
        cut_f = (t_chunk - 1 - tp) * grp
        cut_b = tp * grp
        sh_f = tab_f if cut_f == 0 else jnp.concatenate([tab_f[cut_f:], jnp.zeros((cut_f, w), F32)], 0)
        sh_b = tab_b if cut_b == 0 else jnp.concatenate([jnp.zeros((cut_b, w), F32), tab_b[:w - cut_b]], 0)
        acc = jnp.where(colblk == tp, sh_f + sh_b, acc)
    (fr, fi), (br, bi) = lag_tabs
    (wfr, wfi), (wbr, wbi) = carry_in
    (efr, efi), (ebr, ebi) = decay
    return (acc.astype(BF16),
            jnp.concatenate([fr, br, fi, bi], axis=1).astype(BF16),
            jnp.concatenate([wfr, wbr, -wfi, -wbi], axis=0).astype(BF16),
            jnp.concatenate([efr, ebr, efi, ebi], axis=1))


def _s5_operators(a_re, a_im, log_dt, b_re, b_im, c_re, c_im):
    _, g, p = a_re.shape
    ldt = jnp.broadcast_to(log_dt[:, :, None], (2, g, p))
    arow = jnp.concatenate([a_re, a_im, ldt, jnp.zeros((2, g, p), F32)], axis=0).transpose(1, 0, 2)
    bt_re, bt_im = b_re.transpose(0, 2, 1), b_im.transpose(0, 2, 1)
    ct_re = jnp.tile(c_re.transpose(0, 2, 1), (1, 1, S5_CHUNK))
    ct_im = jnp.tile(c_im.transpose(0, 2, 1), (1, 1, S5_CHUNK))
    gb = S5_OPS_GROUP_BLOCK
    per_g = lambda *s: pl.BlockSpec((gb,) + s, lambda i: (i, 0, 0))
    w = S5_CW
    return pl.pallas_call(
        _s5_ops_kernel,
        grid=(g // gb,),
        in_specs=[per_g(SUBLANES, p), per_g(S5_GROUP, p), per_g(S5_GROUP, p), per_g(p, w), per_g(p, w)],
        out_specs=[per_g(w, w), per_g(w, S5_SW), per_g(S5_SW, w), per_g(1, S5_SW)],
        out_shape=[jax.ShapeDtypeStruct((g, w, w), BF16), jax.ShapeDtypeStruct((g, w, S5_SW), BF16),
                   jax.ShapeDtypeStruct((g, S5_SW, w), BF16), jax.ShapeDtypeStruct((g, 1, S5_SW), F32)],
        compiler_params=_params("arbitrary"),
        name="s5_operators",
    )(arow, bt_re, bt_im, ct_re, ct_im)


def _lane_block_transpose(vs):
    blk = lax.broadcasted_iota(jnp.int32, vs[0].shape, 1) // S5_GROUP
    for dist in (4, 2, 1):
        keep = (blk & dist) == 0
        out = list(vs)
        for r in range(len(vs)):
            if r & dist:
                continue
            lo, hi = vs[r], vs[r + dist]
            out[r] = jnp.where(keep, lo, pltpu.roll(hi, dist * S5_GROUP, axis=1))
            out[r + dist] = jnp.where(keep, pltpu.roll(lo, LANES - dist * S5_GROUP, axis=1), hi)
        vs = out
    return vs


def _s5_in_kernel(x_ref, mod_ref, w_ref, o_ref):
    nch = TOKEN_TILE // S5_CHUNK
    d = w_ref.shape[0]
    per = LANES // S5_GROUP
    n_sub = x_ref.shape[0] // TOKEN_TILE
    us = [jnp.dot(_modulate(x_ref[j * TOKEN_TILE:(j + 1) * TOKEN_TILE, :], mod_ref, 1).astype(BF16), w_ref[...],
                  preferred_element_type=F32) for j in range(n_sub)]
    for j, u in enumerate(us):
        for v in range(d // LANES):
            for th in range(S5_CHUNK // per):
                src = [u[(per * th + r) * nch:(per * th + r + 1) * nch, v * LANES:(v + 1) * LANES]
                       for r in range(per)]
                dst = _lane_block_transpose(src)
                for q in range(per):
                    o_ref[per * v + q, j * nch:(j + 1) * nch, th * LANES:(th + 1) * LANES] = dst[q]


def _s5_in(x, mod, w, *, seq_len):
    n, d = x.shape
    g = d // S5_GROUP
    tm = TOKEN_TILE
    nch = tm // S5_CHUNK
    nc = n // S5_CHUNK
    return pl.pallas_call(
        _s5_in_kernel,
        grid=(n // tm,),
        in_specs=[pl.BlockSpec((tm, d), lambda i: (i, 0)),
                  _mod_spec(mod.shape[0], d, tm, seq_len), _const_spec(w.shape)],
        out_specs=pl.BlockSpec((g, nch, S5_CW), lambda i: (0, i, 0)),
        out_shape=jax.ShapeDtypeStruct((g, nc, S5_CW), F32),
        compiler_params=_params("arbitrary"),
        name="s5_in_proj",
    )(x, mod, w)


def _s5_mix_kernel(u_ref, m_ref, mst_ref, min_ref, a16_ref, dsk_ref, s0_ref, y_ref, fin_ref,
                   loc_ref, sa_ref, sb_ref, *, n_seq, ncs):
    gb = u_ref.shape[0]
    half = S5_SW // 2
    pitch = ncs + SCAN_PAD
    for g in range(gb):
        loc = jnp.dot(u_ref[g].astype(BF16), mst_ref[g], preferred_element_type=F32)
        for s in range(n_seq):
            loc_ref[0, g, s * pitch:s * pitch + ncs, :] = loc[s * ncs:(s + 1) * ncs, :half]
            loc_ref[1, g, s * pitch:s * pitch + ncs, :] = loc[s * ncs:(s + 1) * ncs, half:]
    is_fwd = lax.broadcasted_iota(jnp.int32, (n_seq, half), 1) < S5_STATE
    decay = [(a16_ref[g][:, :half], a16_ref[g][:, half:]) for g in range(gb)]

    def step(k, carry):
        rows_f = pl.ds(k, n_seq, stride=pitch)
        rows_b = pl.ds(ncs - 1 - k, n_seq, stride=pitch)
        out = []
        for g in range(gb):
            sr, si = carry[2 * g], carry[2 * g + 1]
            ar, ai = decay[g]
            sa_ref[0, g, rows_f, :] = sr
            sa_ref[1, g, rows_f, :] = si
            sb_ref[0, g, rows_b, :] = sr
            sb_ref[1, g, rows_b, :] = si
            lr = jnp.where(is_fwd, loc_ref[0, g, rows_f, :], loc_ref[0, g, rows_b, :])
            li = jnp.where(is_fwd, loc_ref[1, g, rows_f, :], loc_ref[1, g, rows_b, :])
            out += [ar * sr - ai * si + lr, ar * si + ai * sr + li]
        return tuple(out)

    init = []
    for g in range(gb):
        init += [s0_ref[g][:, :half], s0_ref[g][:, half:]]
    fin = lax.fori_loop(0, ncs, step, tuple(init))
    sel = lax.broadcasted_iota(jnp.int32, (ncs, half), 1) < S5_STATE
    for g in range(gb):
        fin_ref[g] = jnp.concatenate([fin[2 * g], fin[2 * g + 1]], axis=1)
        parts = []
        for ri in range(2):
            parts.append(jnp.concatenate(
                [jnp.where(sel, sa_ref[ri, g, s * pitch:s * pitch + ncs, :], sb_ref[ri, g, s * pitch:s * pitch + ncs, :])
                 for s in range(n_seq)], axis=0))
        s_in = jnp.concatenate(parts, axis=1).astype(BF16)
        u = u_ref[g]
        y_ref[g] = (jnp.dot(u.astype(BF16), m_ref[g], preferred_element_type=F32)
                    + jnp.dot(s_in, min_ref[g], preferred_element_type=F32)
                    + dsk_ref[g] * u)


def _s5_mix(u_chunks, ops, d_skip, s0, *, n_seq, ncs):
    g, nc, w = u_chunks.shape
    m, mst, m_in, a16 = ops
    gb = S5_GROUP_BLOCK
    dsk = jnp.tile(d_skip.reshape(g, 1, S5_GROUP), (1, 1, S5_CHUNK))
    per_g = lambda *s: pl.BlockSpec((gb,) + s, lambda i: (i, 0, 0))
    rows = n_seq * (ncs + SCAN_PAD)
    return pl.pallas_call(
        functools.partial(_s5_mix_kernel, n_seq=n_seq, ncs=ncs),
        grid=(g // gb,),
        in_specs=[per_g(nc, w), per_g(w, w), per_g(w, S5_SW), per_g(S5_SW, w), per_g(1, S5_SW), per_g(1, w),
                  per_g(n_seq, S5_SW)],
        out_specs=[per_g(nc, w), per_g(n_seq, S5_SW)],
        out_shape=[jax.ShapeDtypeStruct((g, nc, w), F32), jax.ShapeDtypeStruct((g, n_seq, S5_SW), F32)],
        scratch_shapes=[pltpu.VMEM((2, gb, rows, S5_SW // 2), F32)] * 3,
        compiler_params=_params("arbitrary"),
        name="s5_chunk_scan",
    )(u_chunks, m, mst, m_in, a16, dsk, s0)


def _s5_out_kernel(y_ref, x_ref, mod_ref, w_ref, g_ref, b_ref, o_ref, *, alpha):
    nch = TOKEN_TILE // S5_CHUNK
    d = w_ref.shape[0]
    per = LANES // S5_GROUP
    n_sub = x_ref.shape[0] // TOKEN_TILE
    lhs = []
    for j in range(n_sub):
        phase = [[None] * (d // LANES) for _ in range(S5_CHUNK)]
        for v in range(d // LANES):
            for th in range(S5_CHUNK // per):
                src = [y_ref[per * v + q, j * nch:(j + 1) * nch, th * LANES:(th + 1) * LANES] for q in range(per)]
                dst = _lane_block_transpose(src)
                for r in range(per):
                    phase[per * th + r][v] = dst[r]
        y = jnp.concatenate([jnp.concatenate(p, axis=1) for p in phase], axis=0)
        lhs.append(_gelu(y).astype(BF16))
    zs = [jnp.dot(a, w_ref[...], preferred_element_type=F32) for a in lhs]
    for j, z in enumerate(zs):
        rows = slice(j * TOKEN_TILE, (j + 1) * TOKEN_TILE)
        out = z[:, :d] * _sigmoid(z[:, d:])
        o_ref[rows, :] = _layer_norm(alpha * x_ref[rows, :] + _gate(mod_ref, 1) * out, g_ref[...], b_ref[...])


def _s5_out(y_chunks, x, mod, w_glu, ln_g, ln_b, *, alpha, seq_len):
    n, d = x.shape
    g = d // S5_GROUP
    tm = S5_PROJ_SUBTILES * TOKEN_TILE
    nch = tm // S5_CHUNK
    row = pl.BlockSpec((tm, d), lambda i: (i, 0))
    return pl.pallas_call(
        functools.partial(_s5_out_kernel, alpha=alpha),
        grid=(n // tm,),
        in_specs=[pl.BlockSpec((g, nch, S5_CW), lambda i: (0, i, 0)), row,
                  _mod_spec(mod.shape[0], d, tm, seq_len), _const_spec(w_glu.shape),
                  _const_spec((1, d)), _const_spec((1, d))],
        out_specs=row,
        out_shape=jax.ShapeDtypeStruct((n, d), F32),
        compiler_params=_params("arbitrary"),
        name="s5_out_proj",
    )(y_chunks, x, mod, w_glu, ln_g.reshape(1, d), ln_b.reshape(1, d))


def _s5_mixer(x, mod, w_in, ops, d_skip, w_glu, ln_g, ln_b, init, *, alpha, n_seq, seq_len):
    n, d = x.shape
    g = d // S5_GROUP
    ncs = seq_len // S5_CHUNK
    u_chunks = _s5_in(x, mod, w_in, seq_len=seq_len)
    if init is None:
        s0 = jnp.zeros((g, n_seq, S5_SW), F32)
    else:
        s0 = init.transpose(3, 0, 2, 1, 4).reshape(g, n_seq, S5_SW)
    y_chunks, fin = _s5_mix(u_chunks, ops, d_skip, s0, n_seq=n_seq, ncs=ncs)
    x_new = _s5_out(y_chunks, x, mod, w_glu, ln_g, ln_b, alpha=alpha, seq_len=seq_len)
    state = fin.reshape(g, n_seq, 2, 2, S5_STATE).transpose(1, 3, 2, 0, 4)
    return x_new, state


def _softplus(x):
    return jnp.maximum(x, 0.0) + jnp.log1p(jnp.exp(-jnp.abs(x)))


def _lru_scan_kernel(xr_ref, gate_ref, cw_ref, cb_ref, w4_ref, b4_ref, lam_ref, h0_ref, y_ref, fin_ref,
                     af_ref, bf_ref, ab_ref, bb_ref, end_ref, xpad_ref, *, n_seq, seq_len, seg):
    n = n_seq * seq_len
    n_seg = seq_len // seg
    ns = n_seq * n_seg
    pitch = seg + SCAN_PAD
    c = LANES
    chained = n_seg > 1
    half_rate = (0.5 * LRU_C) * _softplus(-lam_ref[0])
    cw, cb = cw_ref[0], cb_ref[0]
    w4_half = (0.5 * w4_ref[0].astype(F32)).astype(BF16)
    b4_half = 0.5 * b4_ref[0]

    zero_tile = jnp.zeros((SUBLANES, c), F32)
    xpad_ref[0:SUBLANES, :] = zero_tile
    xpad_ref[SUBLANES + n:2 * SUBLANES + n, :] = zero_tile

    def pad_copy(s, _):
        r0 = pl.multiple_of(s * seg, seg)
        xpad_ref[pl.ds(SUBLANES + r0, seg), :] = xr_ref[pl.ds(r0, seg), :]
        return 0

    lax.fori_loop(0, ns, pad_copy, 0)
    row8 = lax.broadcasted_iota(jnp.int32, (SUBLANES, c), 0)

    def gates(s, _):
        r0 = pl.multiple_of(s * seg, seg)
        p0 = pl.multiple_of(s * pitch, SCAN_PAD)
        seq_start = s % n_seg == 0
        seq_end = s % n_seg == n_seg - 1
        xc = cb
        for k in range(CONV_W):
            off = k - CONV_LEFT
            tap = xpad_ref[pl.ds(SUBLANES + r0 + off, seg), :]
            if off < 0:
                head = jnp.where(seq_start & (row8 < -off), 0.0, tap[:SUBLANES])
                tap = jnp.concatenate([head, tap[SUBLANES:]], axis=0)
            elif off > 0:
                tail = jnp.where(seq_end & (row8 >= SUBLANES - off), 0.0, tap[seg - SUBLANES:])
                tap = jnp.concatenate([tap[:seg - SUBLANES], tail], axis=0)
            xc = xc + tap * cw[k:k + 1]
        half_pre = jnp.dot(xc.astype(BF16), w4_half, preferred_element_type=F32) + b4_half
        xc_half = 0.5 * xc
        for d, (a_ref, b_ref) in enumerate(((af_ref, bf_ref), (ab_ref, bb_ref))):
            tanh_a = jnp.tanh(half_pre[:, 2 * d * c:(2 * d + 1) * c])
            tanh_x = jnp.tanh(half_pre[:, (2 * d + 1) * c:(2 * d + 2) * c])
            neg_log_a = half_rate[d:d + 1] * tanh_a + half_rate[d:d + 1]
            a = jnp.exp2(neg_log_a * -LOG2_E)
            a_ref[pl.ds(p0, seg), :] = a
            v = jnp.tanh(neg_log_a) * (a * a + 1.0)
            root = jnp.where(v > 0.0, v * lax.rsqrt(v), 0.0)
            b_ref[pl.ds(p0, seg), :] = root * (xc_half * tanh_x + xc_half)
        return 0

    lax.fori_loop(0, ns, gates, 0, unroll=2)

    def step(k, carry):
        hf, hb, pf, pb = carry
        rows_f = pl.ds(k, ns, stride=pitch)
        rows_b = pl.ds(seg - 1 - k, ns, stride=pitch)
        af, ab = af_ref[rows_f, :], ab_ref[rows_b, :]
        hf = af * hf + bf_ref[rows_f, :]
        hb = ab * hb + bb_ref[rows_b, :]
        bf_ref[rows_f, :] = hf
        bb_ref[rows_b, :] = hb
        if chained:
            pf, pb = pf * af, pb * ab
            af_ref[rows_f, :] = pf
            ab_ref[rows_b, :] = pb
        return hf, hb, pf, pb

    one = jnp.ones((ns, c), F32)
    if chained:
        init = (jnp.zeros((ns, c), F32), jnp.zeros((ns, c), F32), one, one)
    else:
        init = (h0_ref[0], h0_ref[1], one, one)
    hf, hb, pf, pb = lax.fori_loop(0, seg, step, init, unroll=2)

    if chained:
        end_ref[0], end_ref[1], end_ref[2], end_ref[3] = hf, hb, pf, pb
        hin_f, hin_b = h0_ref[0], h0_ref[1]
        for j in range(n_seg):
            jf, jb = pl.ds(j, n_seq, stride=n_seg), pl.ds(n_seg - 1 - j, n_seq, stride=n_seg)
            end_ref[4, jf, :] = hin_f
            end_ref[5, jb, :] = hin_b
            hin_f = end_ref[2, jf, :] * hin_f + end_ref[0, jf, :]
            hin_b = end_ref[3, jb, :] * hin_b + end_ref[1, jb, :]
        fin_ref[0] = hin_f
        fin_ref[1] = hin_b
    else:
        fin_ref[0] = hf
        fin_ref[1] = hb

    def combine(s, _):
        rows = pl.ds(pl.multiple_of(s * seg, seg), seg)
        prow = pl.ds(pl.multiple_of(s * pitch, SCAN_PAD), seg)
        h = bf_ref[prow, :] + bb_ref[prow, :]
        if chained:
            h = h + af_ref[prow, :] * end_ref[4, pl.ds(s, 1), :] + ab_ref[prow, :] * end_ref[5, pl.ds(s, 1), :]
        y_ref[rows, :] = (h * _gelu(gate_ref[rows, :])).astype(BF16)
        return 0

    lax.fori_loop(0, ns, combine, 0)


def _lru_scan(xg, conv_w, conv_b, w4, b4, lam, h0, *, n_seq, seq_len):
    n, d2 = xg.shape
    d = d2 // 2
    nt = d // LANES
    seg = min(LRU_SEGMENT, seq_len)
    ns = n // seg
    stripe = lambda off: pl.BlockSpec((n, LANES), lambda j: (0, j + off))
    per_t = lambda *s: pl.BlockSpec((1,) + s, lambda j: (j, 0, 0))
    return pl.pallas_call(
        functools.partial(_lru_scan_kernel, n_seq=n_seq, seq_len=seq_len, seg=seg),
        grid=(nt,),
        in_specs=[stripe(0), stripe(nt), per_t(CONV_W, LANES), per_t(1, LANES), per_t(LANES, 4 * LANES),
                  per_t(1, 4 * LANES), per_t(2, LANES), pl.BlockSpec((2, n_seq, LANES), lambda j: (0, 0, j))],
        out_specs=[pl.BlockSpec((n, LANES), lambda j: (0, j)),
                   pl.BlockSpec((2, n_seq, LANES), lambda j: (0, 0, j))],
        out_shape=[jax.ShapeDtypeStruct((n, d), BF16), jax.ShapeDtypeStruct((2, n_seq, d), F32)],
        scratch_shapes=([pltpu.VMEM((ns * (seg + SCAN_PAD), LANES), F32)] * 4
                        + [pltpu.VMEM((6, ns, LANES), F32), pltpu.VMEM((n + 2 * SUBLANES, LANES), F32)]),
        compiler_params=_params("arbitrary"),
        name="rglru_scan",
    )(xg, xg, conv_w, conv_b, w4, b4, lam, h0)


def _lru_stripe_weights(conv_w, conv_b, w_a, b_a, w_x, b_x, lam):
    d = conv_w.shape[1]
    nt = d // LANES
    per = LANES // LRU_BLOCK

    def blockdiag(w):
        wb = w.reshape(nt, per, LRU_BLOCK, LRU_BLOCK)
        eye = jnp.eye(per, dtype=w.dtype)
        return jnp.einsum('tpij,pq->tpiqj', wb, eye).reshape(nt, LANES, LANES)

    w4 = jnp.concatenate([blockdiag(w_a[0]), blockdiag(w_x[0]), blockdiag(w_a[1]), blockdiag(w_x[1])], axis=2)
    b4 = jnp.concatenate([v.reshape(nt, 1, LANES) for v in (b_a[0], b_x[0], b_a[1], b_x[1])], axis=2)
    cw = conv_w.reshape(CONV_W, nt, LANES).transpose(1, 0, 2)
    cb = conv_b.reshape(nt, 1, LANES)
    lam_t = lam.reshape(2, nt, LANES).transpose(1, 0, 2)
    return cw, cb, w4.astype(BF16), b4, lam_t


def _lru_mixer(x, mod, w_in, stripe_w, init, *, n_seq, seq_len):
    n, d = x.shape
    xg = _inproj(x, mod, w_in, sub=1, seq_len=seq_len)
    h0 = jnp.zeros((2, n_seq, d), F32) if init is None else init.transpose(1, 0, 2)
    y, fin = _lru_scan(xg, *stripe_w, h0, n_seq=n_seq, seq_len=seq_len)
    return y, fin.transpose(1, 0, 2)


def _grid_pos_embed(rows, d):
    quarter = d // 4
    omega = 1.0 / (10000.0 ** (jnp.arange(quarter, dtype=F32) / quarter))

    def emb(p):
        ang = p[:, None] * omega[None, :]
        return jnp.concatenate([jnp.sin(ang), jnp.cos(ang)], -1)

    by_row = emb(jnp.arange(rows, dtype=F32))[:, None, :]
    by_col = emb(jnp.arange(GRID_W, dtype=F32))[None, :, :]
    half = (rows, GRID_W, d // 2)
    table = jnp.concatenate([jnp.broadcast_to(by_row, half), jnp.broadcast_to(by_col, half)], -1)
    return table.reshape(rows * GRID_W, d)


def _trunk(x3, mod, init_s5, init_lru, wts, pos):
    n_seq, seq_len, d = x3.shape
    depth = mod.shape[0]
    alpha = (2.0 * depth) ** 0.25
    x = x3.reshape(n_seq * seq_len, d)
    st_s5, st_lru = [], []
    for i in range(depth):
        j = i // 2
        m = mod[i]
        x = _ffn(x, m, wts["up"], wts["dn"], wts["ln_g"][i, 0], wts["ln_b"][i, 0],
                 layer=i, sub=0, alpha=alpha, seq_len=seq_len, pos=pos if i == 0 else None,
                 row_order="to_phase_major" if i % 2 == 0 else None)
        mixer_out = None
        if i % 2 == 0:
            init = None if init_s5 is None else init_s5[:, j]
            x, st = _s5_mixer(x, m, wts["s5_w_in"][j], wts["s5_ops"][j], wts["s5_d"][j], wts["s5_w_glu"][j],
                              wts["ln_g"][i, 1], wts["ln_b"][i, 1], init,
                              alpha=alpha, n_seq=n_seq, seq_len=seq_len)
            st_s5.append(st)
        else:
            init = None if init_lru is None else init_lru[:, j]
            y, st = _lru_mixer(x, m, wts["lru_w_in"][j], wts["lru_stripe"][j], init, n_seq=n_seq, seq_len=seq_len)
            mixer_out = (y, wts["lru_w_out"][j], wts["ln_g"][i, 1], wts["ln_b"][i, 1])
            st_lru.append(st)
        x = _ffn(x, m, wts["up"], wts["dn"], wts["ln_g"][i, 2], wts["ln_b"][i, 2],
                 layer=i, sub=2, alpha=alpha, seq_len=seq_len, row_order="to_natural" if i % 2 == 0 else None,
                 mixer_out=mixer_out)
    return x.reshape(n_seq, seq_len, d), st_s5, st_lru


def kernel(x_prompt, x_sample, c, state_s5, state_lru, c_ctx, ada_w, ada_b, ln_g, ln_b, ffn_w_up, ffn_w_down, s5_w_in, s5_a_re, s5_a_im, s5_log_dt, s5_b_re, s5_b_im, s5_c_re, s5_c_im, s5_d, s5_w_glu, lru_w_in, lru_conv_w, lru_conv_b, lru_w_a, lru_b_a, lru_w_x, lru_b_x, lru_lambda, lru_w_out):
    depth, d, _ = ada_w.shape
    n_dec = c.shape[0]
    assert 1 + n_dec <= SUBLANES
    cond8 = jnp.concatenate([c_ctx[None, :], c, jnp.zeros((SUBLANES - 1 - n_dec, d), F32)], axis=0)
    mod = _modulation(cond8, ada_w, ada_b).reshape(depth, SUBLANES, 9, d)
    mod_ctx, mod_dec = mod[:, 0:1], mod[:, 1:1 + n_dec]

    wts = {
        "up": ffn_w_up.astype(BF16), "dn": ffn_w_down.astype(BF16),
        "ln_g": ln_g, "ln_b": ln_b,
        "s5_w_in": s5_w_in.astype(BF16), "s5_d": s5_d, "s5_w_glu": s5_w_glu.astype(BF16),
        "s5_ops": [_s5_operators(s5_a_re[j], s5_a_im[j], s5_log_dt[j], s5_b_re[j], s5_b_im[j],
                                 s5_c_re[j], s5_c_im[j]) for j in range(s5_w_in.shape[0])],
        "lru_w_in": lru_w_in.astype(BF16), "lru_w_out": lru_w_out.astype(BF16),
        "lru_stripe": [_lru_stripe_weights(lru_conv_w[j], lru_conv_b[j], lru_w_a[j], lru_b_a[j], lru_w_x[j],
                                           lru_b_x[j], lru_lambda[j]) for j in range(lru_w_in.shape[0])],
    }

    y_prompt, st_s5, st_lru = _trunk(x_prompt, mod_ctx, None, None, wts, None)
    new_state_s5 = jnp.stack(st_s5, 1).astype(x_prompt.dtype)
    new_state_lru = jnp.stack(st_lru, 1).astype(x_prompt.dtype)

    pos = _grid_pos_embed(x_sample.shape[1] // GRID_W, d).astype(x_sample.dtype)
    y_sample, _, _ = _trunk(x_sample, mod_dec, state_s5, state_lru, wts, pos)
    return (y_prompt, y_sample, new_state_s5, new_state_lru)
```

```python
import functools

import jax
import jax.numpy as jnp
from jax import lax
from jax.experimental import pallas as pl
from jax.experimental.pallas import tpu as pltpu

F32 = jnp.float32
BF16 = jnp.bfloat16

GRID_W = 64
S5_GROUP = 16
S5_STATE = 64
LRU_BLOCK = 64
LRU_C = 8.0
CONV_W = 4
CONV_LEFT = 2
LN_EPS = 1e-5
LOG2_E = 1.4426950408889634

LANES = 128
SUBLANES = 8
VMEM_LIMIT_BYTES = 56 * 1024 * 1024

S5_CHUNK = 16
S5_CW = S5_CHUNK * S5_GROUP
S5_SW = 4 * S5_STATE
S5_GROUP_BLOCK = 8
PERM_PITCH = S5_CHUNK + 4
S5_OPS_GROUP_BLOCK = 4
SCAN_PAD = 4
TOKEN_TILE = 512
FFN_SUBTILES = 2
FFN_CHUNK = 256
S5_PROJ_SUBTILES = 2
LRU_SEGMENT = 256


def _params(*sem):
    return pltpu.CompilerParams(dimension_semantics=sem, vmem_limit_bytes=VMEM_LIMIT_BYTES)


def _const_spec(shape, lead=()):
    idx = tuple(lead) + (0,) * len(shape)
    return pl.BlockSpec((None,) * len(lead) + tuple(shape), lambda *_: idx, pipeline_mode=pl.Buffered(1))


def _layer_norm(z, g, b):
    mu = jnp.mean(z, axis=-1, keepdims=True)
    zc = z - mu
    var = jnp.mean(zc * zc, axis=-1, keepdims=True)
    return zc * lax.rsqrt(var + LN_EPS) * g + b


def _sigmoid(x):
    return 0.5 * jnp.tanh(0.5 * x) + 0.5


def _gelu(x):
    c = 0.7978845608028654
    h = 0.5 * x
    return h + h * jnp.tanh(x * ((c * 0.044715) * (x * x) + c))


def _modulate(x, mod_ref, sub):
    sh = mod_ref[0, 3 * sub:3 * sub + 1, :]
    sc = mod_ref[0, 3 * sub + 1:3 * sub + 2, :]
    return x * (1.0 + sc) + sh


def _gate(mod_ref, sub):
    return mod_ref[0, 3 * sub + 2:3 * sub + 3, :]


def _mod_spec(n_cond, d, tm, seq_len):
    if n_cond == 1:
        return pl.BlockSpec((1, 9, d), lambda i: (0, 0, 0))
    return pl.BlockSpec((1, 9, d), lambda i: ((i * tm) // seq_len, 0, 0))


def _mod_kernel(c_ref, w_ref, b_ref, o_ref):
    s = jax.nn.silu(c_ref[...]).astype(BF16)
    o_ref[0] = jnp.dot(s, w_ref[0].astype(BF16), preferred_element_type=F32) + b_ref[0]


def _modulation(cond8, ada_w, ada_b):
    depth, d, n9 = ada_w.shape
    tn = n9 // 4
    return pl.pallas_call(
        _mod_kernel,
        grid=(depth, n9 // tn),
        in_specs=[pl.BlockSpec((SUBLANES, d), lambda l, n: (0, 0)),
                  pl.BlockSpec((1, d, tn), lambda l, n: (l, 0, n)),
                  pl.BlockSpec((1, 1, tn), lambda l, n: (l, 0, n))],
        out_specs=pl.BlockSpec((1, SUBLANES, tn), lambda l, n: (l, 0, n)),
        out_shape=jax.ShapeDtypeStruct((depth, SUBLANES, n9), F32),
        compiler_params=_params("arbitrary", "arbitrary"),
        name="adaln_modulation",
    )(cond8, ada_w, ada_b.reshape(depth, 1, n9))


def _store_rows_permuted(res, o_ref, perm_ref, *, to_phase_major):
    tm, d = res.shape
    nch = tm // S5_CHUNK
    for v in range(d // LANES):
        lanes = slice(v * LANES, (v + 1) * LANES)
        if to_phase_major:
            for c in range(nch):
                perm_ref[v, c * PERM_PITCH:c * PERM_PITCH + S5_CHUNK, :] = res[c * S5_CHUNK:(c + 1) * S5_CHUNK, lanes]
            for t in range(S5_CHUNK):
                o_ref[t * nch:(t + 1) * nch, lanes] = perm_ref[v, pl.ds(t, nch, stride=PERM_PITCH), :]
        else:
            for t in range(S5_CHUNK):
                perm_ref[v, pl.ds(t, nch, stride=PERM_PITCH), :] = res[t * nch:(t + 1) * nch, lanes]
            for c in range(nch):
                o_ref[c * S5_CHUNK:(c + 1) * S5_CHUNK, lanes] = perm_ref[v, c * PERM_PITCH:c * PERM_PITCH + S5_CHUNK, :]


def _ffn_kernel(*refs, sub, alpha, d_ff, ck, has_pos, has_mixer_out, row_order):
    x_ref, refs = refs[0], refs[1:]
    if has_pos:
        pos_ref, refs = refs[0], refs[1:]
    if has_mixer_out:
        (y_ref, wout_ref, mg_ref, mb_ref), refs = refs[:4], refs[4:]
    mod_ref, wup_ref, wdn_ref, g_ref, b_ref, o_ref, act_ref, perm_ref = refs
    slabs = perm_ref.shape[0] * TOKEN_TILE // x_ref.shape[0]
    for j in range(x_ref.shape[0] // TOKEN_TILE):
        rows = slice(j * TOKEN_TILE, (j + 1) * TOKEN_TILE)
        x = x_ref[rows, :] + pos_ref[rows, :] if has_pos else x_ref[rows, :]
        if has_mixer_out:
            mixed = jnp.dot(y_ref[rows, :], wout_ref[...], preferred_element_type=F32)
            x = _layer_norm(alpha * x + _gate(mod_ref, 1) * mixed, mg_ref[...], mb_ref[...])
        h = _modulate(x, mod_ref, sub).astype(BF16)
        for c0 in range(0, d_ff, ck):
            c1 = min(c0 + ck, d_ff)
            a = jnp.dot(h, wup_ref[:, c0:c1], preferred_element_type=F32)
            g = jnp.dot(h, wup_ref[:, d_ff + c0:d_ff + c1], preferred_element_type=F32)
            act_ref[rows, c0:c1] = (jax.nn.silu(g) * a).astype(BF16)
        f = jnp.dot(act_ref[rows, :], wdn_ref[...], preferred_element_type=F32)
        z = alpha * x + (0.5 * _gate(mod_ref, sub)) * f
        res = _layer_norm(z, g_ref[...], b_ref[...])
        if row_order is None:
            o_ref[rows, :] = res
        else:
            _store_rows_permuted(res, o_ref.at[rows], perm_ref.at[j * slabs:(j + 1) * slabs],
                                 to_phase_major=row_order == "to_phase_major")


def _ffn(x, mod, w_up, w_dn, ln_g, ln_b, *, layer, sub, alpha, seq_len, pos=None, row_order=None, mixer_out=None):
    n, d = x.shape
    d_ff = w_dn.shape[2]
    which = (layer, sub // 2)
    tm = TOKEN_TILE if mixer_out is not None else min(FFN_SUBTILES * TOKEN_TILE, n)
    ck = FFN_CHUNK
    row = pl.BlockSpec((tm, d), lambda i: (i, 0))
    in_specs = [row]
    args = [x]
    if pos is not None:
        per_seq = seq_len // tm
        in_specs.append(pl.BlockSpec((tm, d), lambda i: (i % per_seq, 0)))
        args.append(pos)
    if mixer_out is not None:
        y, w_out, mix_g, mix_b = mixer_out
        in_specs += [row, _const_spec(w_out.shape), _const_spec((1, d)), _const_spec((1, d))]
        args += [y, w_out, mix_g.reshape(1, d), mix_b.reshape(1, d)]
    in_specs += [_mod_spec(mod.shape[0], d, tm, seq_len), _const_spec(w_up.shape[2:], which),
                 _const_spec(w_dn.shape[2:], which), _const_spec((1, d)), _const_spec((1, d))]
    args += [mod, w_up, w_dn, ln_g.reshape(1, d), ln_b.reshape(1, d)]
    perm_rows = TOKEN_TILE // S5_CHUNK * PERM_PITCH
    perm_shape = (tm // TOKEN_TILE * (d // LANES), perm_rows, LANES) if row_order else (1, SUBLANES, LANES)
    return pl.pallas_call(
        functools.partial(_ffn_kernel, sub=sub, alpha=alpha, d_ff=d_ff, ck=ck, has_pos=pos is not None,
                          has_mixer_out=mixer_out is not None, row_order=row_order),
        grid=(n // tm,),
        in_specs=in_specs,
        out_specs=row,
        out_shape=jax.ShapeDtypeStruct((n, d), F32),
        scratch_shapes=[pltpu.VMEM((tm, d_ff), BF16), pltpu.VMEM(perm_shape, F32)],
        compiler_params=_params("arbitrary"),
        name="ffn_sublayer",
    )(*args)


INPROJ_RING = 3


def _inproj_kernel(x_hbm, mod_ref, w_ref, o_ref, xbuf_ref, sems, *, sub):
    tm = xbuf_ref.shape[1]
    step, n_steps = pl.program_id(0), pl.num_programs(0)

    def fetch(s):
        slot = s % INPROJ_RING
        return pltpu.make_async_copy(x_hbm.at[pl.ds(s * tm, tm), :], xbuf_ref.at[slot], sems.at[slot])

    @pl.when(step == 0)
    def _():
        for s in range(INPROJ_RING - 1):
            fetch(s).start()

    @pl.when(step + INPROJ_RING - 1 < n_steps)
    def _():
        fetch(step + INPROJ_RING - 1).start()

    fetch(step).wait()
    h = _modulate(xbuf_ref[step % INPROJ_RING], mod_ref, sub).astype(BF16)
    o_ref[...] = jnp.dot(h, w_ref[...], preferred_element_type=F32)


def _inproj(x, mod, w, *, sub, seq_len):
    n, d = x.shape
    dn = w.shape[1]
    tm = min(TOKEN_TILE, n)
    return pl.pallas_call(
        functools.partial(_inproj_kernel, sub=sub),
        grid=(n // tm,),
        in_specs=[pl.BlockSpec(memory_space=pl.ANY), _mod_spec(mod.shape[0], d, tm, seq_len), _const_spec(w.shape)],
        out_specs=pl.BlockSpec((tm, dn), lambda i: (i, 0)),
        out_shape=jax.ShapeDtypeStruct((n, dn), F32),
        scratch_shapes=[pltpu.VMEM((INPROJ_RING, tm, d), F32), pltpu.SemaphoreType.DMA((INPROJ_RING,))],
        compiler_params=_params("arbitrary"),
        name="mixer_in_proj",
    )(x, mod, w)


def _s5_discretize(are, aim, ldt):
    dt = jnp.exp(ldt)
    mag = jnp.exp(dt * are)
    abr = mag * jnp.cos(dt * aim)
    abi = mag * jnp.sin(dt * aim)
    den = are * are + aim * aim
    nr, ni = abr - 1.0, abi
    qr = (nr * are + ni * aim) / den
    qi = (ni * are - nr * aim) / den
    return abr, abi, qr, qi


def _split_bf16(x):
    hi = x.astype(BF16)
    return hi, (x - hi.astype(F32)).astype(BF16)


def _s5_ops_kernel(arow_ref, btr_ref, bti_ref, ctr_ref, cti_ref, m_ref, mst_ref, min_ref, a16_ref):
    for g in range(arow_ref.shape[0]):
        m_ref[g], mst_ref[g], min_ref[g], a16_ref[g] = _s5_group_operators(
            arow_ref[g], btr_ref[g], bti_ref[g], ctr_ref[g], cti_ref[g])


def _s5_group_operators(arow, btr, bti, ctr, cti):
    t_chunk, grp, w = S5_CHUNK, S5_GROUP, S5_CW
    colblk = lax.broadcasted_iota(jnp.int32, (w, w), 1) // grp
    pow_row = lax.broadcasted_iota(jnp.int32, (t_chunk, w), 0)
    pow_blk = lax.broadcasted_iota(jnp.int32, (t_chunk, w), 1) // grp
    tn_dims = (((0,), (0,)), ((), ()))

    abr2, abi2, qr2, qi2 = _s5_discretize(arow[0:2], arow[2:4], arow[4:6])
    lag_tabs, decay, carry_in = [], [], []
    for d in range(2):
        abr, abi, qr, qi = abr2[d:d + 1], abi2[d:d + 1], qr2[d:d + 1], qi2[d:d + 1]
        lr = [qr * btr - qi * bti]
        li = [qr * bti + qi * btr]
        pr, pi = [abr], [abi]
        for _ in range(t_chunk - 1):
            lr.append(lr[-1] * abr - li[-1] * abi)
            li.append(lr[-2] * abi + li[-1] * abr)
            pr.append(pr[-1] * abr - pi[-1] * abi)
            pi.append(pr[-2] * abi + pi[-1] * abr)
        decay.append((pr[-1], pi[-1]))
        if d == 0:
            lr, li = lr[::-1], li[::-1]
        lag_tabs.append((jnp.concatenate(lr, axis=0), jnp.concatenate(li, axis=0)))

        place = (pow_blk == (pow_row if d == 0 else t_chunk - 1 - pow_row)).astype(BF16)
        e_t = []
        for tbl in (jnp.concatenate(pr, axis=0), jnp.concatenate(pi, axis=0)):
            hi, lo = _split_bf16(tbl)
            e_t.append(lax.dot_general(hi, place, tn_dims, preferred_element_type=F32)
                       + lax.dot_general(lo, place, tn_dims, preferred_element_type=F32))
        er_t, ei_t = e_t
        carry_in.append((ctr * er_t - cti * ei_t, ctr * ei_t + cti * er_t))

    lhs = jnp.concatenate([jnp.concatenate([l_re, -l_im], axis=1) for l_re, l_im in lag_tabs], axis=0)
    l_hi, l_lo = _split_bf16(lhs)
    c_hi, c_lo = _split_bf16(jnp.concatenate([ctr, cti], axis=0))
    tab = (jnp.dot(l_hi, c_hi, preferred_element_type=F32) + jnp.dot(l_lo, c_hi, preferred_element_type=F32)
           + jnp.dot(l_hi, c_lo, preferred_element_type=F32))
    tab_f, tab_b = tab[:w], tab[w:]
    acc = jnp.zeros((w, w), F32)
    for tp in range(t_chunk):
        cut_f = (t_chunk - 1 - tp) * grp
        cut_b = tp * grp
        sh_f = tab_f if cut_f == 0 else jnp.concatenate([tab_f[cut_f:], jnp.zeros((cut_f, w), F32)], 0)
        sh_b = tab_b if cut_b == 0 else jnp.concatenate([jnp.zeros((cut_b, w), F32), tab_b[:w - cut_b]], 0)
        acc = jnp.where(colblk == tp, sh_f + sh_b, acc)
    (fr, fi), (br, bi) = lag_tabs
    (wfr, wfi), (wbr, wbi) = carry_in
    (efr, efi), (ebr, ebi) = decay
    return (acc.astype(BF16),
            jnp.concatenate([fr, br, fi, bi], axis=1).astype(BF16),
            jnp.concatenate([wfr, wbr, -wfi, -wbi], axis=0).astype(BF16),
            jnp.concatenate([efr, ebr, efi, ebi], axis=1))


def _s5_operators(a_re, a_im, log_dt, b_re, b_im, c_re, c_im):
    _, g, p = a_re.shape
    ldt = jnp.broadcast_to(log_dt[:, :, None], (2, g, p))
    arow = jnp.concatenate([a_re, a_im, ldt, jnp.zeros((2, g, p), F32)], axis=0).transpose(1, 0, 2)
    bt_re, bt_im = b_re.transpose(0, 2, 1), b_im.transpose(0, 2, 1)
    ct_re = jnp.tile(c_re.transpose(0, 2, 1), (1, 1, S5_CHUNK))
    ct_im = jnp.tile(c_im.transpose(0, 2, 1), (1, 1, S5_CHUNK))
    gb = S5_OPS_GROUP_BLOCK
    per_g = lambda *s: pl.BlockSpec((gb,) + s, lambda i: (i, 0, 0))
    w = S5_CW
    return pl.pallas_call(
        _s5_ops_kernel,
        grid=(g // gb,),
        in_specs=[per_g(SUBLANES, p), per_g(S5_GROUP, p), per_g(S5_GROUP, p), per_g(p, w), per_g(p, w)],
        out_specs=[per_g(w, w), per_g(w, S5_SW), per_g(S5_SW, w), per_g(1, S5_SW)],
        out_shape=[jax.ShapeDtypeStruct((g, w, w), BF16), jax.ShapeDtypeStruct((g, w, S5_SW), BF16),
                   jax.ShapeDtypeStruct((g, S5_SW, w), BF16), jax.ShapeDtypeStruct((g, 1, S5_SW), F32)],
        compiler_params=_params("arbitrary"),
        name="s5_operators",
    )(arow, bt_re, bt_im, ct_re, ct_im)


def _lane_block_transpose(vs):
    blk = lax.broadcasted_iota(jnp.int32, vs[0].shape, 1) // S5_GROUP
    for dist in (4, 2, 1):
        keep = (blk & dist) == 0
        out = list(vs)
        for r in range(len(vs)):
            if r & dist:
                continue
            lo, hi = vs[r], vs[r + dist]
            out[r] = jnp.where(keep, lo, pltpu.roll(hi, dist * S5_GROUP, axis=1))
            out[r + dist] = jnp.where(keep, pltpu.roll(lo, LANES - dist * S5_GROUP, axis=1), hi)
        vs = out
    return vs


def _s5_in_kernel(x_ref, mod_ref, w_ref, o_ref):
    nch = TOKEN_TILE // S5_CHUNK
    d = w_ref.shape[0]
    per = LANES // S5_GROUP
    n_sub = x_ref.shape[0] // TOKEN_TILE
    us = [jnp.dot(_modulate(x_ref[j * TOKEN_TILE:(j + 1) * TOKEN_TILE, :], mod_ref, 1).astype(BF16), w_ref[...],
                  preferred_element_type=F32) for j in range(n_sub)]
    for j, u in enumerate(us):
        for v in range(d // LANES):
            for th in range(S5_CHUNK // per):
                src = [u[(per * th + r) * nch:(per * th + r + 1) * nch, v * LANES:(v + 1) * LANES]
                       for r in range(per)]
                dst = _lane_block_transpose(src)
                for q in range(per):
                    o_ref[per * v + q, j * nch:(j + 1) * nch, th * LANES:(th + 1) * LANES] = dst[q]


def _s5_in(x, mod, w, *, seq_len):
    n, d = x.shape
    g = d // S5_GROUP
    tm = TOKEN_TILE
    nch = tm // S5_CHUNK
    nc = n // S5_CHUNK
    return pl.pallas_call(
        _s5_in_kernel,
        grid=(n // tm,),
        in_specs=[pl.BlockSpec((tm, d), lambda i: (i, 0)),
                  _mod_spec(mod.shape[0], d, tm, seq_len), _const_spec(w.shape)],
        out_specs=pl.BlockSpec((g, nch, S5_CW), lambda i: (0, i, 0)),
        out_shape=jax.ShapeDtypeStruct((g, nc, S5_CW), F32),
        compiler_params=_params("arbitrary"),
        name="s5_in_proj",
    )(x, mod, w)


def _s5_mix_kernel(u_ref, m_ref, mst_ref, min_ref, a16_ref, dsk_ref, s0_ref, y_ref, fin_ref,
                   loc_ref, sa_ref, sb_ref, *, n_seq, ncs):
    gb = u_ref.shape[0]
    half = S5_SW // 2
    pitch = ncs + SCAN_PAD
    for g in range(gb):
        loc = jnp.dot(u_ref[g].astype(BF16), mst_ref[g], preferred_element_type=F32)
        for s in range(n_seq):
            loc_ref[0, g, s * pitch:s * pitch + ncs, :] = loc[s * ncs:(s + 1) * ncs, :half]
            loc_ref[1, g, s * pitch:s * pitch + ncs, :] = loc[s * ncs:(s + 1) * ncs, half:]
    is_fwd = lax.broadcasted_iota(jnp.int32, (n_seq, half), 1) < S5_STATE
    decay = [(a16_ref[g][:, :half], a16_ref[g][:, half:]) for g in range(gb)]

    def step(k, carry):
        rows_f = pl.ds(k, n_seq, stride=pitch)
        rows_b = pl.ds(ncs - 1 - k, n_seq, stride=pitch)
        out = []
        for g in range(gb):
            sr, si = carry[2 * g], carry[2 * g + 1]
            ar, ai = decay[g]
            sa_ref[0, g, rows_f, :] = sr
            sa_ref[1, g, rows_f, :] = si
            sb_ref[0, g, rows_b, :] = sr
            sb_ref[1, g, rows_b, :] = si
            lr = jnp.where(is_fwd, loc_ref[0, g, rows_f, :], loc_ref[0, g, rows_b, :])
            li = jnp.where(is_fwd, loc_ref[1, g, rows_f, :], loc_ref[1, g, rows_b, :])
            out += [ar * sr - ai * si + lr, ar * si + ai * sr + li]
        return tuple(out)

    init = []
    for g in range(gb):
        init += [s0_ref[g][:, :half], s0_ref[g][:, half:]]
    fin = lax.fori_loop(0, ncs, step, tuple(init))
    sel = lax.broadcasted_iota(jnp.int32, (ncs, half), 1) < S5_STATE
    for g in range(gb):
        fin_ref[g] = jnp.concatenate([fin[2 * g], fin[2 * g + 1]], axis=1)
        parts = []
        for ri in range(2):
            parts.append(jnp.concatenate(
                [jnp.where(sel, sa_ref[ri, g, s * pitch:s * pitch + ncs, :], sb_ref[ri, g, s * pitch:s * pitch + ncs, :])
                 for s in range(n_seq)], axis=0))
        s_in = jnp.concatenate(parts, axis=1).astype(BF16)
        u = u_ref[g]
        y_ref[g] = (jnp.dot(u.astype(BF16), m_ref[g], preferred_element_type=F32)
                    + jnp.dot(s_in, min_ref[g], preferred_element_type=F32)
                    + dsk_ref[g] * u)


def _s5_mix(u_chunks, ops, d_skip, s0, *, n_seq, ncs):
    g, nc, w = u_chunks.shape
    m, mst, m_in, a16 = ops
    gb = S5_GROUP_BLOCK
    dsk = jnp.tile(d_skip.reshape(g, 1, S5_GROUP), (1, 1, S5_CHUNK))
    per_g = lambda *s: pl.BlockSpec((gb,) + s, lambda i: (i, 0, 0))
    rows = n_seq * (ncs + SCAN_PAD)
    return pl.pallas_call(
        functools.partial(_s5_mix_kernel, n_seq=n_seq, ncs=ncs),
        grid=(g // gb,),
        in_specs=[per_g(nc, w), per_g(w, w), per_g(w, S5_SW), per_g(S5_SW, w), per_g(1, S5_SW), per_g(1, w),
                  per_g(n_seq, S5_SW)],
        out_specs=[per_g(nc, w), per_g(n_seq, S5_SW)],
        out_shape=[jax.ShapeDtypeStruct((g, nc, w), F32), jax.ShapeDtypeStruct((g, n_seq, S5_SW), F32)],
        scratch_shapes=[pltpu.VMEM((2, gb, rows, S5_SW // 2), F32)] * 3,
        compiler_params=_params("arbitrary"),
        name="s5_chunk_scan",
    )(u_chunks, m, mst, m_in, a16, dsk, s0)


def _s5_out_kernel(y_ref, x_ref, mod_ref, w_ref, g_ref, b_ref, o_ref, *, alpha):
    nch = TOKEN_TILE // S5_CHUNK
    d = w_ref.shape[0]
    per = LANES // S5_GROUP
    n_sub = x_ref.shape[0] // TOKEN_TILE
    lhs = []
    for j in range(n_sub):
        phase = [[None] * (d // LANES) for _ in range(S5_CHUNK)]
        for v in range(d // LANES):
            for th in range(S5_CHUNK // per):
                src = [y_ref[per * v + q, j * nch:(j + 1) * nch, th * LANES:(th + 1) * LANES] for q in range(per)]
                dst = _lane_block_transpose(src)
                for r in range(per):
                    phase[per * th + r][v] = dst[r]
        y = jnp.concatenate([jnp.concatenate(p, axis=1) for p in phase], axis=0)
        lhs.append(_gelu(y).astype(BF16))
    zs = [jnp.dot(a, w_ref[...], preferred_element_type=F32) for a in lhs]
    for j, z in enumerate(zs):
        rows = slice(j * TOKEN_TILE, (j + 1) * TOKEN_TILE)
        out = z[:, :d] * _sigmoid(z[:, d:])
        o_ref[rows, :] = _layer_norm(alpha * x_ref[rows, :] + _gate(mod_ref, 1) * out, g_ref[...], b_ref[...])


def _s5_out(y_chunks, x, mod, w_glu, ln_g, ln_b, *, alpha, seq_len):
    n, d = x.shape
    g = d // S5_GROUP
    tm = S5_PROJ_SUBTILES * TOKEN_TILE
    nch = tm // S5_CHUNK
    row = pl.BlockSpec((tm, d), lambda i: (i, 0))
    return pl.pallas_call(
        functools.partial(_s5_out_kernel, alpha=alpha),
        grid=(n // tm,),
        in_specs=[pl.BlockSpec((g, nch, S5_CW), lambda i: (0, i, 0)), row,
                  _mod_spec(mod.shape[0], d, tm, seq_len), _const_spec(w_glu.shape),
                  _const_spec((1, d)), _const_spec((1, d))],
        out_specs=row,
        out_shape=jax.ShapeDtypeStruct((n, d), F32),
        compiler_params=_params("arbitrary"),
        name="s5_out_proj",
    )(y_chunks, x, mod, w_glu, ln_g.reshape(1, d), ln_b.reshape(1, d))


def _s5_mixer(x, mod, w_in, ops, d_skip, w_glu, ln_g, ln_b, init, *, alpha, n_seq, seq_len):
    n, d = x.shape
    g = d // S5_GROUP
    ncs = seq_len // S5_CHUNK
    u_chunks = _s5_in(x, mod, w_in, seq_len=seq_len)
    if init is None:
        s0 = jnp.zeros((g, n_seq, S5_SW), F32)
    else:
        s0 = init.transpose(3, 0, 2, 1, 4).reshape(g, n_seq, S5_SW)
    y_chunks, fin = _s5_mix(u_chunks, ops, d_skip, s0, n_seq=n_seq, ncs=ncs)
    x_new = _s5_out(y_chunks, x, mod, w_glu, ln_g, ln_b, alpha=alpha, seq_len=seq_len)
    state = fin.reshape(g, n_seq, 2, 2, S5_STATE).transpose(1, 3, 2, 0, 4)
    return x_new, state


def _softplus(x):
    return jnp.maximum(x, 0.0) + jnp.log1p(jnp.exp(-jnp.abs(x)))


def _lru_scan_kernel(xr_ref, gate_ref, cw_ref, cb_ref, w4_ref, b4_ref, lam_ref, h0_ref, y_ref, fin_ref,
                     af_ref, bf_ref, ab_ref, bb_ref, end_ref, xpad_ref, *, n_seq, seq_len, seg):
    n = n_seq * seq_len
    n_seg = seq_len // seg
    ns = n_seq * n_seg
    pitch = seg + SCAN_PAD
    c = LANES
    chained = n_seg > 1
    half_rate = (0.5 * LRU_C) * _softplus(-lam_ref[0])
    cw, cb = cw_ref[0], cb_ref[0]
    w4_half = (0.5 * w4_ref[0].astype(F32)).astype(BF16)
    b4_half = 0.5 * b4_ref[0]

    zero_tile = jnp.zeros((SUBLANES, c), F32)
    xpad_ref[0:SUBLANES, :] = zero_tile
    xpad_ref[SUBLANES + n:2 * SUBLANES + n, :] = zero_tile

    def pad_copy(s, _):
        r0 = pl.multiple_of(s * seg, seg)
        xpad_ref[pl.ds(SUBLANES + r0, seg), :] = xr_ref[pl.ds(r0, seg), :]
        return 0

    lax.fori_loop(0, ns, pad_copy, 0)
    row8 = lax.broadcasted_iota(jnp.int32, (SUBLANES, c), 0)

    def gates(s, _):
        r0 = pl.multiple_of(s * seg, seg)
        p0 = pl.multiple_of(s * pitch, SCAN_PAD)
        seq_start = s % n_seg == 0
        seq_end = s % n_seg == n_seg - 1
        xc = cb
        for k in range(CONV_W):
            off = k - CONV_LEFT
            tap = xpad_ref[pl.ds(SUBLANES + r0 + off, seg), :]
            if off < 0:
                head = jnp.where(seq_start & (row8 < -off), 0.0, tap[:SUBLANES])
                tap = jnp.concatenate([head, tap[SUBLANES:]], axis=0)
            elif off > 0:
                tail = jnp.where(seq_end & (row8 >= SUBLANES - off), 0.0, tap[seg - SUBLANES:])
                tap = jnp.concatenate([tap[:seg - SUBLANES], tail], axis=0)
            xc = xc + tap * cw[k:k + 1]
        half_pre = jnp.dot(xc.astype(BF16), w4_half, preferred_element_type=F32) + b4_half
        xc_half = 0.5 * xc
        for d, (a_ref, b_ref) in enumerate(((af_ref, bf_ref), (ab_ref, bb_ref))):
            tanh_a = jnp.tanh(half_pre[:, 2 * d * c:(2 * d + 1) * c])
            tanh_x = jnp.tanh(half_pre[:, (2 * d + 1) * c:(2 * d + 2) * c])
            neg_log_a = half_rate[d:d + 1] * tanh_a + half_rate[d:d + 1]
            a = jnp.exp2(neg_log_a * -LOG2_E)
            a_ref[pl.ds(p0, seg), :] = a
            v = jnp.tanh(neg_log_a) * (a * a + 1.0)
            root = jnp.where(v > 0.0, v * lax.rsqrt(v), 0.0)
            b_ref[pl.ds(p0, seg), :] = root * (xc_half * tanh_x + xc_half)
        return 0

    lax.fori_loop(0, ns, gates, 0, unroll=2)

    def step(k, carry):
        hf, hb, pf, pb = carry
        rows_f = pl.ds(k, ns, stride=pitch)
        rows_b = pl.ds(seg - 1 - k, ns, stride=pitch)
        af, ab = af_ref[rows_f, :], ab_ref[rows_b, :]
        hf = af * hf + bf_ref[rows_f, :]
        hb = ab * hb + bb_ref[rows_b, :]
        bf_ref[rows_f, :] = hf
        bb_ref[rows_b, :] = hb
        if chained:
            pf, pb = pf * af, pb * ab
            af_ref[rows_f, :] = pf
            ab_ref[rows_b, :] = pb
        return hf, hb, pf, pb

    one = jnp.ones((ns, c), F32)
    if chained:
        init = (jnp.zeros((ns, c), F32), jnp.zeros((ns, c), F32), one, one)
    else:
        init = (h0_ref[0], h0_ref[1], one, one)
    hf, hb, pf, pb = lax.fori_loop(0, seg, step, init, unroll=2)

    if chained:
        end_ref[0], end_ref[1], end_ref[2], end_ref[3] = hf, hb, pf, pb
        hin_f, hin_b = h0_ref[0], h0_ref[1]
        for j in range(n_seg):
            jf, jb = pl.ds(j, n_seq, stride=n_seg), pl.ds(n_seg - 1 - j, n_seq, stride=n_seg)
            end_ref[4, jf, :] = hin_f
            end_ref[5, jb, :] = hin_b
            hin_f = end_ref[2, jf, :] * hin_f + end_ref[0, jf, :]
            hin_b = end_ref[3, jb, :] * hin_b + end_ref[1, jb, :]
        fin_ref[0] = hin_f
        fin_ref[1] = hin_b
    else:
        fin_ref[0] = hf
        fin_ref[1] = hb

    def combine(s, _):
        rows = pl.ds(pl.multiple_of(s * seg, seg), seg)
        prow = pl.ds(pl.multiple_of(s * pitch, SCAN_PAD), seg)
        h = bf_ref[prow, :] + bb_ref[prow, :]
        if chained:
            h = h + af_ref[prow, :] * end_ref[4, pl.ds(s, 1), :] + ab_ref[prow, :] * end_ref[5, pl.ds(s, 1), :]
        y_ref[rows, :] = (h * _gelu(gate_ref[rows, :])).astype(BF16)
        return 0

    lax.fori_loop(0, ns, combine, 0)


def _lru_scan(xg, conv_w, conv_b, w4, b4, lam, h0, *, n_seq, seq_len):
    n, d2 = xg.shape
    d = d2 // 2
    nt = d // LANES
    seg = min(LRU_SEGMENT, seq_len)
    ns = n // seg
    stripe = lambda off: pl.BlockSpec((n, LANES), lambda j: (0, j + off))
    per_t = lambda *s: pl.BlockSpec((1,) + s, lambda j: (j, 0, 0))
    return pl.pallas_call(
        functools.partial(_lru_scan_kernel, n_seq=n_seq, seq_len=seq_len, seg=seg),
        grid=(nt,),
        in_specs=[stripe(0), stripe(nt), per_t(CONV_W, LANES), per_t(1, LANES), per_t(LANES, 4 * LANES),
                  per_t(1, 4 * LANES), per_t(2, LANES), pl.BlockSpec((2, n_seq, LANES), lambda j: (0, 0, j))],
        out_specs=[pl.BlockSpec((n, LANES), lambda j: (0, j)),
                   pl.BlockSpec((2, n_seq, LANES), lambda j: (0, 0, j))],
        out_shape=[jax.ShapeDtypeStruct((n, d), BF16), jax.ShapeDtypeStruct((2, n_seq, d), F32)],
        scratch_shapes=([pltpu.VMEM((ns * (seg + SCAN_PAD), LANES), F32)] * 4
                        + [pltpu.VMEM((6, ns, LANES), F32), pltpu.VMEM((n + 2 * SUBLANES, LANES), F32)]),
        compiler_params=_params("arbitrary"),
        name="rglru_scan",
    )(xg, xg, conv_w, conv_b, w4, b4, lam, h0)


def _lru_stripe_weights(conv_w, conv_b, w_a, b_a, w_x, b_x, lam):
    d = conv_w.shape[1]
    nt = d // LANES
    per = LANES // LRU_BLOCK

    def blockdiag(w):
        wb = w.reshape(nt, per, LRU_BLOCK, LRU_BLOCK)
        eye = jnp.eye(per, dtype=w.dtype)
        return jnp.einsum('tpij,pq->tpiqj', wb, eye).reshape(nt, LANES, LANES)

    w4 = jnp.concatenate([blockdiag(w_a[0]), blockdiag(w_x[0]), blockdiag(w_a[1]), blockdiag(w_x[1])], axis=2)
    b4 = jnp.concatenate([v.reshape(nt, 1, LANES) for v in (b_a[0], b_x[0], b_a[1], b_x[1])], axis=2)
    cw = conv_w.reshape(CONV_W, nt, LANES).transpose(1, 0, 2)
    cb = conv_b.reshape(nt, 1, LANES)
    lam_t = lam.reshape(2, nt, LANES).transpose(1, 0, 2)
    return cw, cb, w4.astype(BF16), b4, lam_t


def _lru_mixer(x, mod, w_in, stripe_w, init, *, n_seq, seq_len):
    n, d = x.shape
    xg = _inproj(x, mod, w_in, sub=1, seq_len=seq_len)
    h0 = jnp.zeros((2, n_seq, d), F32) if init is None else init.transpose(1, 0, 2)
    y, fin = _lru_scan(xg, *stripe_w, h0, n_seq=n_seq, seq_len=seq_len)
    return y, fin.transpose(1, 0, 2)


def _grid_pos_embed(rows, d):
    quarter = d // 4
    omega = 1.0 / (10000.0 ** (jnp.arange(quarter, dtype=F32) / quarter))

    def emb(p):
        ang = p[:, None] * omega[None, :]
        return jnp.concatenate([jnp.sin(ang), jnp.cos(ang)], -1)

    by_row = emb(jnp.arange(rows, dtype=F32))[:, None, :]
    by_col = emb(jnp.arange(GRID_W, dtype=F32))[None, :, :]
    half = (rows, GRID_W, d // 2)
    table = jnp.concatenate([jnp.broadcast_to(by_row, half), jnp.broadcast_to(by_col, half)], -1)
    return table.reshape(rows * GRID_W, d)


def _trunk(x3, mod, init_s5, init_lru, wts, pos):
    n_seq, seq_len, d = x3.shape
    depth = mod.shape[0]
    alpha = (2.0 * depth) ** 0.25
    x = x3.reshape(n_seq * seq_len, d)
    st_s5, st_lru = [], []
    for i in range(depth):
        j = i // 2
        m = mod[i]
        x = _ffn(x, m, wts["up"], wts["dn"], wts["ln_g"][i, 0], wts["ln_b"][i, 0],
                 layer=i, sub=0, alpha=alpha, seq_len=seq_len, pos=pos if i == 0 else None,
                 row_order="to_phase_major" if i % 2 == 0 else None)
        mixer_out = None
        if i % 2 == 0:
            init = None if init_s5 is None else init_s5[:, j]
            x, st = _s5_mixer(x, m, wts["s5_w_in"][j], wts["s5_ops"][j], wts["s5_d"][j], wts["s5_w_glu"][j],
                              wts["ln_g"][i, 1], wts["ln_b"][i, 1], init,
                              alpha=alpha, n_seq=n_seq, seq_len=seq_len)
            st_s5.append(st)
        else:
            init = None if init_lru is None else init_lru[:, j]
            y, st = _lru_mixer(x, m, wts["lru_w_in"][j], wts["lru_stripe"][j], init, n_seq=n_seq, seq_len=seq_len)
            mixer_out = (y, wts["lru_w_out"][j], wts["ln_g"][i, 1], wts["ln_b"][i, 1])
            st_lru.append(st)
        x = _ffn(x, m, wts["up"], wts["dn"], wts["ln_g"][i, 2], wts["ln_b"][i, 2],
                 layer=i, sub=2, alpha=alpha, seq_len=seq_len, row_order="to_natural" if i % 2 == 0 else None,
                 mixer_out=mixer_out)
    return x.reshape(n_seq, seq_len, d), st_s5, st_lru


def kernel(x_prompt, x_sample, c, state_s5, state_lru, c_ctx, ada_w, ada_b, ln_g, ln_b, ffn_w_up, ffn_w_down, s5_w_in, s5_a_re, s5_a_im, s5_log_dt, s5_b_re, s5_b_im, s5_c_re, s5_c_im, s5_d, s5_w_glu, lru_w_in, lru_conv_w, lru_conv_b, lru_w_a, lru_b_a, lru_w_x, lru_b_x, lru_lambda, lru_w_out):
    depth, d, _ = ada_w.shape
    n_dec = c.shape[0]
    assert 1 + n_dec <= SUBLANES
    cond8 = jnp.concatenate([c_ctx[None, :], c, jnp.zeros((SUBLANES - 1 - n_dec, d), F32)], axis=0)
    mod = _modulation(cond8, ada_w, ada_b).reshape(depth, SUBLANES, 9, d)
    mod_ctx, mod_dec = mod[:, 0:1], mod[:, 1:1 + n_dec]

    wts = {
        "up": ffn_w_up.astype(BF16), "dn": ffn_w_down.astype(BF16),
        "ln_g": ln_g, "ln_b": ln_b,
        "s5_w_in": s5_w_in.astype(BF16), "s5_d": s5_d, "s5_w_glu": s5_w_glu.astype(BF16),
        "s5_ops": [_s5_operators(s5_a_re[j], s5_a_im[j], s5_log_dt[j], s5_b_re[j], s5_b_im[j],
                                 s5_c_re[j], s5_c_im[j]) for j in range(s5_w_in.shape[0])],
        "lru_w_in": lru_w_in.astype(BF16), "lru_w_out": lru_w_out.astype(BF16),
        "lru_stripe": [_lru_stripe_weights(lru_conv_w[j], lru_conv_b[j], lru_w_a[j], lru_b_a[j], lru_w_x[j],
                                           lru_b_x[j], lru_lambda[j]) for j in range(lru_w_in.shape[0])],
    }

    y_prompt, st_s5, st_lru = _trunk(x_prompt, mod_ctx, None, None, wts, None)
    new_state_s5 = jnp.stack(st_s5, 1).astype(x_prompt.dtype)
    new_state_lru = jnp.stack(st_lru, 1).astype(x_prompt.dtype)

    pos = _grid_pos_embed(x_sample.shape[1] // GRID_W, d).astype(x_sample.dtype)
    y_sample, _, _ = _trunk(x_sample, mod_dec, state_s5, state_lru, wts, pos)
    return (y_prompt, y_sample, new_state_s5, new_state_lru)
```
